```python
import jax, jax.numpy as jnp
from jax import lax
import numpy as np


D_MODEL = 1024
BATCH = 2
SEQ = 16384
DEPTH = 4

N_MIXERS = 3
N_META = 16
BLOCK = 128
FRONT_PAD = BLOCK - N_META
EPS = 1e-6
ROPE_THETA = 10000.0

A_HEADS = 16
A_KV = 4
A_HD = 64
A_WIN = 128
A_W = A_HEADS * A_HD
A_KVW = A_KV * A_HD
A_IN = A_W + 2 * A_KVW + A_W

B_HEADS = 8
B_DK = 128
B_DV = 256
B_CONV = 4
B_QK = B_HEADS * B_DK
B_W = B_HEADS * B_DV
B_SIZES = (B_QK, B_QK, B_W, B_HEADS, B_HEADS, B_W, B_W)
B_IN = sum(B_SIZES)

C_HEADS = 8
C_DK = 128
C_DV = 128
C_CHUNK = 64
C_K = C_HEADS * C_DK
C_W = C_HEADS * C_DV
C_SIZES = (C_K, C_K, C_W, C_W)
C_IN = sum(C_SIZES)

N_A = (DEPTH + 2) // 3
N_B = (DEPTH + 1) // 3
N_C = DEPTH // 3

kernel_name = 'hybrid_swa_mlstm_hgrn2_meta'


def _split(t, sizes):
    pts = [int(p) for p in np.cumsum(sizes)[:-1]]
    return jnp.split(t, pts, axis=-1)


def rmsnorm(x, g):
    xf = x.astype(jnp.float32)
    y = xf * lax.rsqrt(jnp.mean(xf * xf, axis=-1, keepdims=True) + EPS)
    return (y * g.astype(jnp.float32)).astype(x.dtype)


def rope(x, pos):
    half = x.shape[-1] // 2
    inv = ROPE_THETA ** (-jnp.arange(half, dtype=jnp.float32) / half)
    ang = pos.astype(jnp.float32)[:, None] * inv[None, :]
    cos = jnp.cos(ang)[None, :, None, :]
    sin = jnp.sin(ang)[None, :, None, :]
    xf = x.astype(jnp.float32)
    x1, x2 = xf[..., :half], xf[..., half:]
    return jnp.concatenate([x1 * cos - x2 * sin, x2 * cos + x1 * sin], axis=-1).astype(x.dtype)


def causal_conv(x, w, b):
    K = w.shape[0]
    L = x.shape[1]
    xp = jnp.pad(x, ((0, 0), (K - 1, 0), (0, 0)))
    return sum(xp[:, j:j + L] * w[j] for j in range(K)) + b


def _to_chunks(t, size):
    b, h, lp = t.shape[:3]
    return jnp.moveaxis(t.reshape(b, h, lp // size, size, *t.shape[3:]), 2, 0)


def _from_chunks(t):
    t = jnp.moveaxis(t, 0, 2)
    return t.reshape(t.shape[0], t.shape[1], t.shape[2] * t.shape[3], t.shape[4])


def sliding_window_attention(q, k, v, sinks, valid):
    bsz, lp = q.shape[:2]
    nb = lp // BLOCK
    g = A_HEADS // A_KV
    qb = q.astype(jnp.float32).reshape(bsz, nb, BLOCK, A_KV, g, A_HD)
    kb = k.astype(jnp.float32).reshape(bsz, nb, BLOCK, A_KV, A_HD)
    vb = v.astype(jnp.float32).reshape(bsz, nb, BLOCK, A_KV, A_HD)
    prev = lambda t: jnp.pad(t, ((0, 0), (1, 0)) + ((0, 0),) * (t.ndim - 2))[:, :-1]
    kk = jnp.concatenate([prev(kb), kb], axis=2)
    vv = jnp.concatenate([prev(vb), vb], axis=2)
    vb_mask = valid.reshape(nb, BLOCK)
    kvalid = jnp.concatenate([jnp.pad(vb_mask, ((1, 0), (0, 0)))[:-1], vb_mask], axis=1)
    s = jnp.einsum('bnqhgd,bnkhd->bnhgqk', qb, kk) * (A_HD ** -0.5)
    dist = (jnp.arange(BLOCK)[:, None] + BLOCK) - jnp.arange(2 * BLOCK)[None, :]
    band = (dist >= 0) & (dist < A_WIN)
    mask = band[None] & kvalid[:, None, :]
    s = jnp.where(mask[None, :, None, None], s, -jnp.inf)
    sink = sinks.astype(jnp.float32).reshape(A_KV, g, 1)
    m = jnp.maximum(s.max(axis=-1), sink)
    e = jnp.exp(s - m[..., None])
    p = e / (e.sum(axis=-1) + jnp.exp(sink - m))[..., None]
    o = jnp.einsum('bnhgqk,bnkhd->bnqhgd', p, vv)
    return o.reshape(bsz, lp, A_W)


def attention_branch(u, w_in, q_gain, k_gain, sinks, w_out, valid, pos):
    bsz, lp, _ = u.shape
    q, k, v, z = _split(u @ w_in, (A_W, A_KVW, A_KVW, A_W))
    q = rope(rmsnorm(q.reshape(bsz, lp, A_HEADS, A_HD), q_gain), pos)
    k = rope(rmsnorm(k.reshape(bsz, lp, A_KV, A_HD), k_gain), pos)
    v = v.reshape(bsz, lp, A_KV, A_HD)
    o = sliding_window_attention(q, k, v, sinks, valid)
    return (o * jax.nn.silu(z.astype(jnp.float32))).astype(u.dtype) @ w_out


def mlstm_chunkwise(q, k, v, log_i, log_f):
    bsz, nh, _, dk = q.shape
    dv = v.shape[-1]
    causal = jnp.tril(jnp.ones((BLOCK, BLOCK), dtype=bool))

    def step(carry, inp):
        C, n, m = carry
        q_, k_, v_, li, lf = inp
        a = jnp.cumsum(lf, axis=-1)
        g = a[..., -1]
        D = jnp.where(causal, a[..., :, None] - a[..., None, :] + li[..., None, :], -jnp.inf)
        inter = a + m[..., None]
        m_t = jnp.maximum(inter, D.max(axis=-1))
        w_inter = jnp.exp(inter - m_t)
        qk = jnp.einsum('bhtd,bhsd->bhts', q_, k_) * jnp.exp(D - m_t[..., None])
        num = w_inter[..., None] * jnp.einsum('bhtd,bhde->bhte', q_, C) + jnp.einsum('bhts,bhse->bhte', qk, v_)
        den = w_inter * jnp.einsum('bhtd,bhd->bht', q_, n) + qk.sum(axis=-1)
        h = num / jnp.maximum(jnp.abs(den), jnp.exp(-m_t))[..., None]
        uu = g[..., None] - a + li
        m_new = jnp.maximum(g + m, uu.max(axis=-1))
        w_old = jnp.exp(g + m - m_new)
        w_s = jnp.exp(uu - m_new[..., None])
        C_new = w_old[..., None, None] * C + jnp.einsum('bhs,bhsd,bhse->bhde', w_s, k_, v_)
        n_new = w_old[..., None] * n + jnp.einsum('bhs,bhsd->bhd', w_s, k_)
        return (C_new, n_new, m_new), h

    init = (jnp.zeros((bsz, nh, dk, dv), jnp.float32),
            jnp.zeros((bsz, nh, dk), jnp.float32),
            jnp.zeros((bsz, nh), jnp.float32))
    xs = (_to_chunks(q, BLOCK), _to_chunks(k, BLOCK), _to_chunks(v, BLOCK),
          _to_chunks(log_i, BLOCK), _to_chunks(log_f, BLOCK))
    _, hs = lax.scan(step, init, xs)
    return _from_chunks(hs)


def mlstm_branch(u, w_in, conv_w, conv_b, gate_bias, h_gain, w_out, valid):
    bsz, lp, _ = u.shape
    q, k, v, ig, fg, og, z = _split(u @ w_in, B_SIZES)
    qk = jax.nn.silu(causal_conv(jnp.concatenate([q, k], axis=-1), conv_w, conv_b))
    q, k = qk[..., :B_QK], qk[..., B_QK:]
    heads = lambda t, d: jnp.transpose(t.astype(jnp.float32).reshape(bsz, lp, B_HEADS, d), (0, 2, 1, 3))
    qh = heads(q, B_DK)
    kh = heads(k, B_DK) * (B_DK ** -0.5)
    vh = heads(v, B_DV)
    gb = gate_bias.astype(jnp.float32)
    vmask = valid[None, :, None]
    log_i = jnp.where(vmask, ig.astype(jnp.float32) + gb[:B_HEADS], -jnp.inf)
    log_f = jnp.where(vmask, jax.nn.log_sigmoid(fg.astype(jnp.float32) + gb[B_HEADS:]), 0.0)
    h = mlstm_chunkwise(qh, kh, vh, jnp.transpose(log_i, (0, 2, 1)), jnp.transpose(log_f, (0, 2, 1)))
    h = jnp.transpose(h, (0, 2, 1, 3))
    h = jax.nn.sigmoid(og.astype(jnp.float32)).reshape(bsz, lp, B_HEADS, B_DV) * h
    h = rmsnorm(h, h_gain.reshape(B_HEADS, B_DV)).reshape(bsz, lp, B_W)
    return (h * jax.nn.silu(z.astype(jnp.float32))).astype(u.dtype) @ w_out


def hgrn2_chunkwise(q, k, v, log_f):
    bsz, nh, _, dk = q.shape
    dv = v.shape[-1]
    causal = jnp.tril(jnp.ones((C_CHUNK, C_CHUNK), dtype=bool))

    def step(S, inp):
        q_, k_, v_, lf = inp
        A = jnp.cumsum(lf, axis=2)
        o_inter = jnp.einsum('bhtd,bhde->bhte', q_ * jnp.exp(A), S)
        dec = jnp.where(causal[..., None], A[:, :, :, None, :] - A[:, :, None, :, :], -jnp.inf)
        att = jnp.einsum('bhtd,bhtsd,bhsd->bhts', q_, jnp.exp(dec), k_)
        o = o_inter + jnp.einsum('bhts,bhse->bhte', att, v_)
        A_last = A[:, :, -1]
        S_new = jnp.exp(A_last)[..., None] * S + jnp.einsum('bhsd,bhse->bhde', k_ * jnp.exp(A_last[:, :, None] - A), v_)
        return S_new, o

    xs = (_to_chunks(q, C_CHUNK), _to_chunks(k, C_CHUNK), _to_chunks(v, C_CHUNK), _to_chunks(log_f, C_CHUNK))
    _, os_ = lax.scan(step, jnp.zeros((bsz, nh, dk, dv), jnp.float32), xs)
    return _from_chunks(os_)


def hgrn2_branch(u, w_in, lower_bound, o_gain, w_out, valid):
    bsz, lp, _ = u.shape
    q, fpre, inp, z = _split(u @ w_in, C_SIZES)
    lb = lower_bound.astype(jnp.float32)
    lf = jnp.logaddexp(jnp.log(lb), jnp.log1p(-lb) + jax.nn.log_sigmoid(fpre.astype(jnp.float32)))
    kk = -jnp.expm1(lf)
    vmask = valid[None, :, None]
    lf = jnp.where(vmask, lf, 0.0)
    kk = jnp.where(vmask, kk, 0.0)
    heads = lambda t, d: jnp.transpose(t.astype(jnp.float32).reshape(bsz, lp, C_HEADS, d), (0, 2, 1, 3))
    o = hgrn2_chunkwise(heads(jax.nn.silu(q.astype(jnp.float32)), C_DK), heads(kk, C_DK),
                        heads(inp, C_DV), heads(lf, C_DK))
    o = jnp.transpose(o, (0, 2, 1, 3))
    o = rmsnorm(o, o_gain.reshape(C_HEADS, C_DV)).reshape(bsz, lp, C_W)
    return (o * jax.nn.silu(z.astype(jnp.float32))).astype(u.dtype) @ w_out


def setup_inputs(seed: int = 0) -> dict:
    key = jax.random.key(seed)
    ks = jax.random.split(key, 20)
    nrm = lambda k, shape, s: s * jax.random.normal(k, shape, jnp.float32)
    f_bias = jnp.linspace(3.0, 6.0, B_HEADS, dtype=jnp.float32)
    return {
        'x': nrm(ks[0], (BATCH, SEQ, D_MODEL), 1.0),
        'meta': nrm(ks[1], (N_META, D_MODEL), 1.0),
        'norm_gain': 1.0 + nrm(ks[2], (DEPTH, D_MODEL), 0.05),
        'a_w_in': nrm(ks[3], (N_A, D_MODEL, A_IN), D_MODEL ** -0.5),
        'a_q_gain': 1.0 + nrm(ks[4], (N_A, A_HD), 0.05),
        'a_k_gain': 1.0 + nrm(ks[5], (N_A, A_HD), 0.05),
        'a_sinks': nrm(ks[6], (N_A, A_HEADS), 0.5),
        'a_w_out': nrm(ks[7], (N_A, A_W, D_MODEL), A_W ** -0.5),
        'b_w_in': nrm(ks[8], (N_B, D_MODEL, B_IN), D_MODEL ** -0.5),
        'b_conv_w': nrm(ks[9], (N_B, B_CONV, 2 * B_QK), B_CONV ** -0.5),
        'b_conv_b': nrm(ks[10], (N_B, 2 * B_QK), 0.02),
        'b_gate_bias': jnp.concatenate([nrm(ks[11], (N_B, B_HEADS), 0.1),
                                        f_bias[None] + nrm(ks[12], (N_B, B_HEADS), 0.1)], axis=-1),
        'b_h_gain': 1.0 + nrm(ks[13], (N_B, B_W), 0.05),
        'b_w_out': nrm(ks[14], (N_B, B_W, D_MODEL), B_W ** -0.5),
        'c_w_in': nrm(ks[15], (N_C, D_MODEL, C_IN), D_MODEL ** -0.5),
        'c_gamma': nrm(ks[16], (DEPTH, C_K), 0.1),
        'c_o_gain': 1.0 + nrm(ks[17], (N_C, C_W), 0.05),
        'c_w_out': nrm(ks[18], (N_C, C_W, D_MODEL), C_W ** -0.5),
    }


def reference(x, meta, norm_gain, a_w_in, a_q_gain, a_k_gain, a_sinks, a_w_out,
              b_w_in, b_conv_w, b_conv_b, b_gate_bias, b_h_gain, b_w_out,
              c_w_in, c_gamma, c_o_gain, c_w_out):
    bsz = x.shape[0]
    h = jnp.concatenate([jnp.zeros((bsz, FRONT_PAD, D_MODEL), x.dtype),
                         jnp.broadcast_to(meta.astype(x.dtype)[None], (bsz, N_META, D_MODEL)), x], axis=1)
    lp = h.shape[1]
    idx = jnp.arange(lp)
    valid = idx >= FRONT_PAD
    pos = (idx - FRONT_PAD).astype(jnp.int32)
    P = jax.nn.softmax(c_gamma.astype(jnp.float32), axis=0)
    lower_bounds = jnp.cumsum(P, axis=0) - P
    for i in range(DEPTH):
        kind = i % N_MIXERS
        j = i // N_MIXERS
        u = rmsnorm(h, norm_gain[i])
        if kind == 0:
            y = attention_branch(u, a_w_in[j], a_q_gain[j], a_k_gain[j], a_sinks[j], a_w_out[j], valid, pos)
        elif kind == 1:
            y = mlstm_branch(u, b_w_in[j], b_conv_w[j], b_conv_b[j], b_gate_bias[j], b_h_gain[j], b_w_out[j], valid)
        else:
            y = hgrn2_branch(u, c_w_in[j], lower_bounds[i], c_o_gain[j], c_w_out[j], valid)
        h = h + jnp.where(valid[None, :, None], y.astype(h.dtype), 0.0).astype(h.dtype)
    return h[:, FRONT_PAD + N_META:]
```

```python
import functools

import jax
import jax.numpy as jnp
from jax import lax
from jax.experimental import pallas as pl
from jax.experimental.pallas import tpu as pltpu

F32 = jnp.float32
BF16 = jnp.bfloat16

D_MODEL = 1024
BLOCK = 128
N_META = 16
FRONT_PAD = BLOCK - N_META
EPS = 1e-6
ROPE_THETA = 10000.0
LANES = 128

A_HEADS, A_KV, A_HD = 16, 4, 64
A_GROUP = A_HEADS // A_KV
A_W = A_HEADS * A_HD
A_KVW = A_KV * A_HD
A_IN = 2 * A_W + 2 * A_KVW

B_HEADS, B_DK, B_DV, B_CONV = 8, 128, 256, 4
B_QK = B_HEADS * B_DK
B_W = B_HEADS * B_DV
B_HIST = 8

C_HEADS, C_DK, C_DV, C_CHUNK = 8, 128, 128, 64
C_K = C_HEADS * C_DK
C_W = C_HEADS * C_DV
C_LEVELS = (1, 2, 4, 8, 16, 32)

VMEM_LIMIT = 56 * 1024 * 1024


def _sigmoid(x):
    return 1.0 / (1.0 + jnp.exp(-x))


def _silu(x):
    return x * _sigmoid(x)


def _log_sigmoid(x):
    return jnp.minimum(x, 0.0) - jnp.log(1.0 + jnp.exp(-jnp.abs(x)))


def _rmsnorm(x, g):
    ms = jnp.mean(x * x, axis=-1, keepdims=True)
    return x * lax.rsqrt(ms + EPS) * g


def _dot(a, b):
    return jnp.dot(a, b, preferred_element_type=F32)


def _dot_nt(a, b):
    return lax.dot_general(a, b, (((1,), (1,)), ((), ())), preferred_element_type=F32)


def _dot_tn(a, b):
    return lax.dot_general(a, b, (((0,), (0,)), ((), ())), preferred_element_type=F32)


def _valid_rows(rows, cols):
    return lax.broadcasted_iota(jnp.int32, (rows, cols), 0) >= FRONT_PAD


def _attn_rows(h, rows, meta, cos, sin, gain_ref, win_ref, qg_ref, kg_ref, sink_ref, wout_ref,
               proj, qbuf, kbuf, vbuf, obuf):
    u = _rmsnorm(h, gain_ref[...]).astype(BF16)
    proj[0:rows, :] = _dot(u, win_ref[...])

    lane = lax.broadcasted_iota(jnp.int32, (rows, LANES), 1)
    low_head = lane < A_HD
    first_half = (lane % A_HD) < (A_HD // 2)

    def norm_rope(x, g):
        sq = x * x
        s_lo = jnp.sum(jnp.where(low_head, sq, 0.0), axis=-1, keepdims=True)
        s_hi = jnp.sum(jnp.where(low_head, 0.0, sq), axis=-1, keepdims=True)
        r = jnp.where(low_head, lax.rsqrt(s_lo / A_HD + EPS), lax.rsqrt(s_hi / A_HD + EPS))
        xn = x * r * g
        rot = jnp.where(first_half, pltpu.roll(xn, LANES - A_HD // 2, 1), pltpu.roll(xn, A_HD // 2, 1))
        return xn * cos + rot * sin

    qg = qg_ref[...]
    kg = kg_ref[...]
    for g in range(A_W // LANES):
        x = proj[0:rows, g * LANES:(g + 1) * LANES]
        qbuf[0:rows, g * LANES:(g + 1) * LANES] = (norm_rope(x, qg) * (A_HD ** -0.5)).astype(BF16)
    for g in range(A_KVW // LANES):
        x = proj[0:rows, A_W + g * LANES:A_W + (g + 1) * LANES]
        kbuf[BLOCK:BLOCK + rows, g * LANES:(g + 1) * LANES] = norm_rope(x, kg).astype(BF16)
    vbuf[BLOCK:BLOCK + rows, :] = proj[0:rows, A_W + A_KVW:A_W + 2 * A_KVW].astype(BF16)

    t_idx = lax.broadcasted_iota(jnp.int32, (BLOCK, 2 * BLOCK), 0)
    c_idx = lax.broadcasted_iota(jnp.int32, (BLOCK, 2 * BLOCK), 1)
    dist = t_idx + BLOCK - c_idx
    mask = (dist >= 0) & (dist < BLOCK)
    if meta:
        mask = mask & (c_idx >= BLOCK + FRONT_PAD)
        first_mask = mask
    else:
        first_valid = jnp.where(pl.program_id(1) == 0, FRONT_PAD, 0)
        first_mask = mask & (c_idx >= first_valid)

    for n in range(rows // BLOCK):
        r0 = n * BLOCK
        blk_mask = first_mask if n == 0 else mask
        for j in range(A_KV):
            kk = kbuf[r0:r0 + 2 * BLOCK, j * A_HD:(j + 1) * A_HD]
            vv = vbuf[r0:r0 + 2 * BLOCK, j * A_HD:(j + 1) * A_HD]
            q4 = jnp.concatenate(
                [qbuf[r0:r0 + BLOCK, (A_GROUP * j + g) * A_HD:(A_GROUP * j + g + 1) * A_HD]
                 for g in range(A_GROUP)], axis=0)
            s = _dot_nt(q4, kk)
            es, dens = [], []
            for g in range(A_GROUP):
                sg = jnp.where(blk_mask, s[g * BLOCK:(g + 1) * BLOCK], -jnp.inf)
                sink = sink_ref[A_GROUP * j + g]
                m = jnp.maximum(jnp.max(sg, axis=-1, keepdims=True), sink)
                e = jnp.exp(sg - m)
                dens.append(jnp.sum(e, axis=-1, keepdims=True) + jnp.exp(sink - m))
                es.append(e.astype(BF16))
            o = _dot(jnp.concatenate(es, axis=0), vv)
            for g in range(A_GROUP):
                hq = A_GROUP * j + g
                obuf[r0:r0 + BLOCK, hq * A_HD:(hq + 1) * A_HD] = o[g * BLOCK:(g + 1) * BLOCK] / dens[g]

    kbuf[0:BLOCK, :] = kbuf[rows:rows + BLOCK, :]
    vbuf[0:BLOCK, :] = vbuf[rows:rows + BLOCK, :]

    z = proj[0:rows, A_W + 2 * A_KVW:A_IN]
    y = _dot((obuf[0:rows, :] * _silu(z)).astype(BF16), wout_ref[...])
    if meta:
        y = jnp.where(_valid_rows(rows, D_MODEL), y, 0.0)
    return h + y


def _attn_kernel(x_ref, hm_ref, cos_ref, sin_ref, cosm_ref, sinm_ref, gain_ref, win_ref, qg_ref,
                 kg_ref, sink_ref, wout_ref, y_ref, hm_out_ref, proj, qbuf, kbuf, vbuf, obuf, *, tb):
    params = (gain_ref, win_ref, qg_ref, kg_ref, sink_ref, wout_ref, proj, qbuf, kbuf, vbuf, obuf)

    @pl.when(pl.program_id(1) == 0)
    def _():
        kbuf[0:BLOCK, :] = jnp.zeros((BLOCK, A_KVW), BF16)
        vbuf[0:BLOCK, :] = jnp.zeros((BLOCK, A_KVW), BF16)
        hm_out_ref[...] = _attn_rows(hm_ref[...], BLOCK, True, cosm_ref[...], sinm_ref[...], *params)

    y_ref[0] = _attn_rows(x_ref[0], tb, False, cos_ref[...], sin_ref[...], *params)


def _const_spec(shape):
    zeros = (0,) * len(shape)
    return pl.BlockSpec(shape, lambda b, i: zeros)


def _attn_layer(x, hm, cos, sin, cosm, sinm, gain, w_in, q_gain, k_gain, sinks, w_out, tb):
    bsz, seq, _ = x.shape
    row_spec = pl.BlockSpec((1, tb, D_MODEL), lambda b, i: (b, i, 0))
    tab_spec = pl.BlockSpec((tb, LANES), lambda b, i: (i, 0))
    return pl.pallas_call(
        functools.partial(_attn_kernel, tb=tb),
        grid=(bsz, seq // tb),
        in_specs=[row_spec, _const_spec((BLOCK, D_MODEL)), tab_spec, tab_spec,
                  _const_spec((BLOCK, LANES)), _const_spec((BLOCK, LANES)),
                  _const_spec((1, D_MODEL)), _const_spec((D_MODEL, A_IN)),
                  _const_spec((1, LANES)), _const_spec((1, LANES)),
                  pl.BlockSpec(memory_space=pltpu.SMEM), _const_spec((A_W, D_MODEL))],
        out_specs=[row_spec, _const_spec((BLOCK, D_MODEL))],
        out_shape=[jax.ShapeDtypeStruct(x.shape, F32), jax.ShapeDtypeStruct((BLOCK, D_MODEL), F32)],
        scratch_shapes=[pltpu.VMEM((tb, A_IN), F32), pltpu.VMEM((tb, A_W), BF16),
                        pltpu.VMEM((tb + BLOCK, A_KVW), BF16), pltpu.VMEM((tb + BLOCK, A_KVW), BF16),
                        pltpu.VMEM((tb, A_W), F32)],
        compiler_params=pltpu.CompilerParams(dimension_semantics=("arbitrary", "arbitrary"),
                                             vmem_limit_bytes=VMEM_LIMIT),
        name="swa_layer",
    )(x, hm, cos, sin, cosm, sinm, gain, w_in, q_gain, k_gain, sinks, w_out)


def _mlstm_rows(h, rows, meta, gain_ref, wqk_ref, wv_ref, wo_ref, wz_ref, wg_ref, cw_ref, cb_ref,
                bi_ref, bf_ref, hg_ref, wout_ref, qkraw, qs, ks, vs, ob, zb, gates, gated, cst, nst, mst):
    u = _rmsnorm(h, gain_ref[...]).astype(BF16)
    qkraw[B_HIST:B_HIST + rows, :] = _dot(u, wqk_ref[...])
    vs[0:rows, :] = _dot(u, wv_ref[...]).astype(BF16)
    ob[0:rows, :] = _dot(u, wo_ref[...])
    zb[0:rows, :] = _dot(u, wz_ref[...])
    gates[0:rows, :] = _dot(u, wg_ref[...])

    acc = cb_ref[...]
    for j in range(B_CONV):
        start = B_HIST - (B_CONV - 1) + j
        acc = acc + qkraw[start:start + rows, :] * cw_ref[j:j + 1, :]
    qk = _silu(acc)
    qs[0:rows, :] = qk[:, :B_QK].astype(BF16)
    ks[0:rows, :] = (qk[:, B_QK:] * (B_DK ** -0.5)).astype(BF16)
    qkraw[0:B_HIST, :] = qkraw[rows:rows + B_HIST, :]

    t_idx = lax.broadcasted_iota(jnp.int32, (BLOCK, BLOCK), 0)
    s_idx = lax.broadcasted_iota(jnp.int32, (BLOCK, BLOCK), 1)
    causal = s_idx <= t_idx
    lane8 = lax.broadcasted_iota(jnp.int32, (B_HEADS, BLOCK), 1)

    for c in range(rows // BLOCK):
        r0 = c * BLOCK
        g_t = gates[r0:r0 + BLOCK, :].T
        li = g_t[0:B_HEADS] + bi_ref[...]
        lf = _log_sigmoid(g_t[B_HEADS:2 * B_HEADS] + bf_ref[...])
        if meta:
            li = jnp.where(lane8 >= FRONT_PAD, li, -jnp.inf)
            lf = jnp.where(lane8 >= FRONT_PAD, lf, 0.0)
        a = lf
        shift = 1
        while shift < BLOCK:
            a = a + jnp.where(lane8 >= shift, pltpu.roll(a, shift, 1), 0.0)
            shift *= 2
        g_tot = a[:, BLOCK - 1:BLOCK]
        cols = jnp.concatenate(
            [a, li, jnp.zeros((BLOCK - 2 * B_HEADS, BLOCK), F32)], axis=0).T

        for hd in range(B_HEADS):
            qh = qs[r0:r0 + BLOCK, hd * B_DK:(hd + 1) * B_DK]
            kh = ks[r0:r0 + BLOCK, hd * B_DK:(hd + 1) * B_DK]
            vh = vs[r0:r0 + BLOCK, hd * B_DV:(hd + 1) * B_DV]
            a_col = cols[:, hd:hd + 1]
            li_col = cols[:, B_HEADS + hd:B_HEADS + hd + 1]
            a_row = a[hd:hd + 1, :]
            li_row = li[hd:hd + 1, :]
            g = g_tot[hd:hd + 1, :]
            m_prev = mst[hd:hd + 1, 0:1]
            c_prev = cst[hd]
            n_prev = nst[hd:hd + 1, :]

            dmat = jnp.where(causal, a_col - a_row + li_row, -jnp.inf)
            inter = a_col + m_prev
            m_t = jnp.maximum(inter, jnp.max(dmat, axis=-1, keepdims=True))
            w_inter = jnp.exp(inter - m_t)
            sc = _dot_nt(qh, kh) * jnp.exp(dmat - m_t)
            num = w_inter * _dot(qh, c_prev.astype(BF16)) + _dot(sc.astype(BF16), vh)
            den = (w_inter * jnp.sum(qh.astype(F32) * n_prev, axis=-1, keepdims=True)
                   + jnp.sum(sc, axis=-1, keepdims=True))
            hh = num / jnp.maximum(jnp.abs(den), jnp.exp(-m_t))

            uu = g - a_col + li_col
            m_new = jnp.maximum(g + m_prev, jnp.max(uu, axis=0, keepdims=True))
            w_old = jnp.exp(g + m_prev - m_new)
            w_s = jnp.exp(uu - m_new)
            kw = kh.astype(F32) * w_s
            cst[hd] = w_old * c_prev + _dot_tn(kw.astype(BF16), vh)
            nst[hd:hd + 1, :] = w_old * n_prev + jnp.sum(kw, axis=0, keepdims=True)
            mst[hd:hd + 1, :] = jnp.broadcast_to(m_new, (1, LANES))

            hh = _sigmoid(ob[r0:r0 + BLOCK, hd * B_DV:(hd + 1) * B_DV]) * hh
            hn = _rmsnorm(hh, hg_ref[:, hd * B_DV:(hd + 1) * B_DV])
            zz = zb[r0:r0 + BLOCK, hd * B_DV:(hd + 1) * B_DV]
            gated[r0:r0 + BLOCK, hd * B_DV:(hd + 1) * B_DV] = (hn * _silu(zz)).astype(BF16)

    y = _dot(gated[0:rows, :], wout_ref[...])
    if meta:
        y = jnp.where(_valid_rows(rows, D_MODEL), y, 0.0)
    return h + y


def _mlstm_kernel(x_ref, hm_ref, gain_ref, wqk_ref, wv_ref, wo_ref, wz_ref, wg_ref, cw_ref, cb_ref,
                  bi_ref, bf_ref, hg_ref, wout_ref, y_ref, hm_out_ref,
                  qkraw, qs, ks, vs, ob, zb, gates, gated, cst, nst, mst, *, tb):
    params = (gain_ref, wqk_ref, wv_ref, wo_ref, wz_ref, wg_ref, cw_ref, cb_ref, bi_ref, bf_ref,
              hg_ref, wout_ref, qkraw, qs, ks, vs, ob, zb, gates, gated, cst, nst, mst)

    @pl.when(pl.program_id(1) == 0)
    def _():
        qkraw[0:B_HIST, :] = jnp.zeros((B_HIST, 2 * B_QK), F32)
        cst[...] = jnp.zeros(cst.shape, F32)
        nst[...] = jnp.zeros(nst.shape, F32)
        mst[...] = jnp.zeros(mst.shape, F32)
        hm_out_ref[...] = _mlstm_rows(hm_ref[...], BLOCK, True, *params)

    y_ref[0] = _mlstm_rows(x_ref[0], tb, False, *params)


def _mlstm_layer(x, hm, gain, wqk, wv, wo, wz, wg, conv_w, conv_b, bias_i, bias_f, h_gain, w_out, tb):
    bsz, seq, _ = x.shape
    row_spec = pl.BlockSpec((1, tb, D_MODEL), lambda b, i: (b, i, 0))
    return pl.pallas_call(
        functools.partial(_mlstm_kernel, tb=tb),
        grid=(bsz, seq // tb),
        in_specs=[row_spec, _const_spec((BLOCK, D_MODEL)), _const_spec((1, D_MODEL)),
                  _const_spec((D_MODEL, 2 * B_QK)), _const_spec((D_MODEL, B_W)),
                  _const_spec((D_MODEL, B_W)), _const_spec((D_MODEL, B_W)),
                  _const_spec((D_MODEL, LANES)), _const_spec((B_CONV, 2 * B_QK)),
                  _const_spec((1, 2 * B_QK)), _const_spec((B_HEADS, BLOCK)),
                  _const_spec((B_HEADS, BLOCK)), _const_spec((1, B_W)), _const_spec((B_W, D_MODEL))],
        out_specs=[row_spec, _const_spec((BLOCK, D_MODEL))],
        out_shape=[jax.ShapeDtypeStruct(x.shape, F32), jax.ShapeDtypeStruct((BLOCK, D_MODEL), F32)],
        scratch_shapes=[pltpu.VMEM((tb + B_HIST, 2 * B_QK), F32), pltpu.VMEM((tb, B_QK), BF16),
                        pltpu.VMEM((tb, B_QK), BF16), pltpu.VMEM((tb, B_W), BF16),
                        pltpu.VMEM((tb, B_W), F32), pltpu.VMEM((tb, B_W), F32),
                        pltpu.VMEM((tb, LANES), F32), pltpu.VMEM((tb, B_W), BF16),
                        pltpu.VMEM((B_HEADS, B_DK, B_DV), F32), pltpu.VMEM((B_HEADS, B_DK), F32),
                        pltpu.VMEM((B_HEADS, LANES), F32)],
        compiler_params=pltpu.CompilerParams(dimension_semantics=("arbitrary", "arbitrary"),
                                             vmem_limit_bytes=VMEM_LIMIT),
        name="mlstm_layer",
    )(x, hm, gain, wqk, wv, wo, wz, wg, conv_w, conv_b, bias_i, bias_f, h_gain, w_out)


def _hgrn_level_masks():
    t_idx = lax.broadcasted_iota(jnp.int32, (C_CHUNK, C_CHUNK), 0)
    s_idx = lax.broadcasted_iota(jnp.int32, (C_CHUNK, C_CHUNK), 1)
    masks = []
    for b in C_LEVELS:
        same = (t_idx // (2 * b)) == (s_idx // (2 * b))
        masks.append(same & ((t_idx % (2 * b)) >= b) & ((s_idx % (2 * b)) < b))
    return t_idx == s_idx, masks


def _hgrn_reference_rows(ab, r0, b):
    sub = lax.broadcasted_iota(jnp.int32, (8, C_K), 0)
    pieces = []
    if 2 * b >= 8:
        for p in range(C_CHUNK // (2 * b)):
            mid = r0 + 2 * b * p + b - 1
            pieces.append(jnp.broadcast_to(ab[pl.ds(mid, 1), :], (2 * b, C_K)))
    else:
        for v in range(C_CHUNK // 8):
            cand = [jnp.broadcast_to(ab[pl.ds(r0 + 8 * v + 2 * b * p + b - 1, 1), :], (8, C_K))
                    for p in range(8 // (2 * b))]
            sel = cand[-1]
            for p in reversed(range(len(cand) - 1)):
                sel = jnp.where(sub < 2 * b * (p + 1), cand[p], sel)
            pieces.append(sel)
    return jnp.concatenate(pieces, axis=0)


def _hgrn_rows(h, rows, meta, gain_ref, wq_ref, wf_ref, wi_ref, wz_ref, lb_ref, og_ref, wout_ref,
               qb, kb, ab, vb, zb, gated, st):
    u = _rmsnorm(h, gain_ref[...]).astype(BF16)
    qb[0:rows, :] = _silu(_dot(u, wq_ref[...]))
    fpre = _dot(u, wf_ref[...])
    lb = lb_ref[...]
    e = jnp.exp(-jnp.abs(fpre))
    inv = 1.0 / (1.0 + e)
    pos = fpre >= 0.0
    sig = jnp.where(pos, inv, e * inv)
    nsig = jnp.where(pos, e * inv, inv)
    lf = jnp.log(lb + (1.0 - lb) * sig)
    kk = (1.0 - lb) * nsig
    if meta:
        valid = _valid_rows(rows, C_K)
        lf = jnp.where(valid, lf, 0.0)
        kk = jnp.where(valid, kk, 0.0)
    row_in_chunk = lax.broadcasted_iota(jnp.int32, (rows, C_K), 0) % C_CHUNK
    a = lf
    shift = 1
    while shift < C_CHUNK:
        a = a + jnp.where(row_in_chunk >= shift, pltpu.roll(a, shift, 0), 0.0)
        shift *= 2
    ab[0:rows, :] = a
    kb[0:rows, :] = kk
    vb[0:rows, :] = _dot(u, wi_ref[...]).astype(BF16)
    zb[0:rows, :] = _dot(u, wz_ref[...])

    eye, masks = _hgrn_level_masks()

    def chunk(c, carry):
        r0 = pl.multiple_of(c * C_CHUNK, C_CHUNK)
        a_c = ab[pl.ds(r0, C_CHUNK), :]
        q = qb[pl.ds(r0, C_CHUNK), :]
        k = kb[pl.ds(r0, C_CHUNK), :]
        v = vb[pl.ds(r0, C_CHUNK), :]
        a_last = ab[pl.ds(r0 + C_CHUNK - 1, 1), :]
        q_in = (q * jnp.exp(a_c)).astype(BF16)
        k_out = (k * jnp.exp(a_last - a_c)).astype(BF16)
        decay = jnp.exp(a_last)
        q_lv, k_lv = [q.astype(BF16)], [k.astype(BF16)]
        for b in C_LEVELS:
            w = jnp.exp(-jnp.abs(a_c - _hgrn_reference_rows(ab, r0, b)))
            q_lv.append((q * w).astype(BF16))
            k_lv.append((k * w).astype(BF16))
        level_masks = [eye] + masks
        for hd in range(C_HEADS):
            ln = slice(hd * C_DK, (hd + 1) * C_DK)
            s_t = st[hd]
            o = _dot_nt(q_in[:, ln], s_t.astype(BF16))
            att = jnp.zeros((C_CHUNK, C_CHUNK), F32)
            for ql, kl, mk in zip(q_lv, k_lv, level_masks):
                att = att + jnp.where(mk, _dot_nt(ql[:, ln], kl[:, ln]), 0.0)
            o = o + _dot(att.astype(BF16), v[:, ln])
            st[hd] = s_t * decay[:, ln] + _dot_tn(v[:, ln], k_out[:, ln])
            on = _rmsnorm(o, og_ref[:, ln])
            gated[pl.ds(r0, C_CHUNK), ln] = (on * _silu(zb[pl.ds(r0, C_CHUNK), ln])).astype(BF16)
        return carry

    lax.fori_loop(0, rows // C_CHUNK, chunk, 0)

    y = _dot(gated[0:rows, :], wout_ref[...])
    if meta:
        y = jnp.where(_valid_rows(rows, D_MODEL), y, 0.0)
    return h + y


def _hgrn_kernel(x_ref, hm_ref, gain_ref, wq_ref, wf_ref, wi_ref, wz_ref, lb_ref, og_ref, wout_ref,
                 y_ref, hm_out_ref, qb, kb, ab, vb, zb, gated, st, *, tb):
    params = (gain_ref, wq_ref, wf_ref, wi_ref, wz_ref, lb_ref, og_ref, wout_ref,
              qb, kb, ab, vb, zb, gated, st)

    @pl.when(pl.program_id(1) == 0)
    def _():
        st[...] = jnp.zeros(st.shape, F32)
        hm_out_ref[...] = _hgrn_rows(hm_ref[...], BLOCK, True, *params)

    y_ref[0] = _hgrn_rows(x_ref[0], tb, False, *params)


def _hgrn_layer(x, hm, gain, wq, wf, wi, wz, lb, o_gain, w_out, tb):
    bsz, seq, _ = x.shape
    row_spec = pl.BlockSpec((1, tb, D_MODEL), lambda b, i: (b, i, 0))
    wspec = _const_spec((D_MODEL, C_K))
    return pl.pallas_call(
        functools.partial(_hgrn_kernel, tb=tb),
        grid=(bsz, seq // tb),
        in_specs=[row_spec, _const_spec((BLOCK, D_MODEL)), _const_spec((1, D_MODEL)),
                  wspec, wspec, wspec, wspec, _const_spec((1, C_K)), _const_spec((1, C_W)),
                  _const_spec((C_W, D_MODEL))],
        out_specs=[row_spec, _const_spec((BLOCK, D_MODEL))],
        out_shape=[jax.ShapeDtypeStruct(x.shape, F32), jax.ShapeDtypeStruct((BLOCK, D_MODEL), F32)],
        scratch_shapes=[pltpu.VMEM((tb, C_K), F32), pltpu.VMEM((tb, C_K), F32),
                        pltpu.VMEM((tb, C_K), F32), pltpu.VMEM((tb, C_W), BF16),
                        pltpu.VMEM((tb, C_W), F32), pltpu.VMEM((tb, C_W), BF16),
                        pltpu.VMEM((C_HEADS, C_DV, C_DK), F32)],
        compiler_params=pltpu.CompilerParams(dimension_semantics=("arbitrary", "arbitrary"),
                                             vmem_limit_bytes=VMEM_LIMIT),
        name="hgrn2_layer",
    )(x, hm, gain, wq, wf, wi, wz, lb, o_gain, w_out)


def _rope_tables(seq):
    half = A_HD // 2
    inv = ROPE_THETA ** (-jnp.arange(half, dtype=F32) / half)
    pos = (jnp.arange(seq + BLOCK) - FRONT_PAD).astype(jnp.int32)
    ang = pos.astype(F32)[:, None] * inv[None, :]
    cos, sin = jnp.cos(ang), jnp.sin(ang)
    reps = LANES // A_HD
    cos_t = jnp.tile(jnp.concatenate([cos, cos], axis=-1), (1, reps))
    sin_t = jnp.tile(jnp.concatenate([-sin, sin], axis=-1), (1, reps))
    return cos_t[BLOCK:], sin_t[BLOCK:], cos_t[:BLOCK], sin_t[:BLOCK]


def _row_tile(seq, want):
    tb = want
    while seq % tb:
        tb //= 2
    assert tb >= BLOCK, "sequence length must be a multiple of the 128-token block"
    return tb


def kernel(x, meta, norm_gain, a_w_in, a_q_gain, a_k_gain, a_sinks, a_w_out, b_w_in, b_conv_w,
           b_conv_b, b_gate_bias, b_h_gain, b_w_out, c_w_in, c_gamma, c_o_gain, c_w_out):
    bsz, seq, _ = x.shape
    depth = norm_gain.shape[0]
    cos, sin, cosm, sinm = _rope_tables(seq)
    hm = jnp.concatenate([jnp.zeros((FRONT_PAD, D_MODEL), x.dtype), meta.astype(x.dtype)], axis=0)
    p = jax.nn.softmax(c_gamma.astype(F32), axis=0)
    lower_bounds = jnp.cumsum(p, axis=0) - p
    reps = LANES // A_HD

    h = x
    for i in range(depth):
        kind, j = i % 3, i // 3
        gain = norm_gain[i].reshape(1, D_MODEL).astype(F32)
        if kind == 0:
            h, hm = _attn_layer(
                h, hm, cos, sin, cosm, sinm, gain, a_w_in[j].astype(BF16),
                jnp.tile(a_q_gain[j].astype(F32), reps).reshape(1, LANES),
                jnp.tile(a_k_gain[j].astype(F32), reps).reshape(1, LANES),
                a_sinks[j].astype(F32), a_w_out[j].astype(BF16), _row_tile(seq, 512))
        elif kind == 1:
            w = b_w_in[j]
            o_v, o_i, o_o, o_z = 2 * B_QK, 2 * B_QK + B_W, 2 * B_QK + B_W + 2 * B_HEADS, 2 * B_QK + 2 * B_W + 2 * B_HEADS
            wg = jnp.pad(w[:, o_i:o_o], ((0, 0), (0, LANES - 2 * B_HEADS)))
            gb = b_gate_bias[j].astype(F32)
            h, hm = _mlstm_layer(
                h, hm, gain, w[:, :o_v].astype(BF16), w[:, o_v:o_i].astype(BF16),
                w[:, o_o:o_z].astype(BF16), w[:, o_z:].astype(BF16), wg.astype(BF16),
                b_conv_w[j].astype(F32), b_conv_b[j].reshape(1, 2 * B_QK).astype(F32),
                jnp.broadcast_to(gb[:B_HEADS, None], (B_HEADS, BLOCK)),
                jnp.broadcast_to(gb[B_HEADS:, None], (B_HEADS, BLOCK)),
                b_h_gain[j].reshape(1, B_W).astype(F32), b_w_out[j].astype(BF16), _row_tile(seq, 256))
        else:
            w = c_w_in[j].astype(BF16)
            h, hm = _hgrn_layer(
                h, hm, gain, w[:, :C_K], w[:, C_K:2 * C_K], w[:, 2 * C_K:3 * C_K], w[:, 3 * C_K:],
                lower_bounds[i].reshape(1, C_K), c_o_gain[j].reshape(1, C_W).astype(F32),
                c_w_out[j].astype(BF16), _row_tile(seq, 512))
    return h
```

```python
import functools

import jax
import jax.numpy as jnp
from jax import lax
from jax.experimental import pallas as pl
from jax.experimental.pallas import tpu as pltpu

F32 = jnp.float32
BF16 = jnp.bfloat16

D_MODEL = 1024
BLOCK = 128
N_META = 16
FRONT_PAD = BLOCK - N_META
EPS = 1e-6
ROPE_THETA = 10000.0
LANES = 128
SUBLANES = 8

A_HEADS, A_KV, A_HD = 16, 4, 64
A_GROUP = A_HEADS // A_KV
A_W = A_HEADS * A_HD
A_KVW = A_KV * A_HD
A_IN = 2 * A_W + 2 * A_KVW

B_HEADS, B_DK, B_DV, B_CONV = 8, 128, 256, 4
B_QK = B_HEADS * B_DK
B_W = B_HEADS * B_DV
B_HIST = 8

C_HEADS, C_DK, C_DV, C_CHUNK = 8, 128, 128, 64
C_K = C_HEADS * C_DK
C_W = C_HEADS * C_DV
C_LEVELS = (1, 2, 4, 8, 16, 32)
C_SAFE_DECAY = 60.0

VMEM_LIMIT = 56 * 1024 * 1024


def _sigmoid(x):
    return 0.5 + 0.5 * jnp.tanh(0.5 * x)


def _silu(x):
    return x * _sigmoid(x)


def _log_sigmoid(x):
    return jnp.minimum(x, 0.0) - jnp.log(1.0 + jnp.exp(-jnp.abs(x)))


def _rmsnorm(x, g):
    ms = jnp.mean(x * x, axis=-1, keepdims=True)
    return x * lax.rsqrt(ms + EPS) * g


def _dot(a, b):
    return jnp.dot(a, b, preferred_element_type=F32)


def _dot_nt(a, b):
    return lax.dot_general(a, b, (((1,), (1,)), ((), ())), preferred_element_type=F32)


def _dot_tn(a, b):
    return lax.dot_general(a, b, (((0,), (0,)), ((), ())), preferred_element_type=F32)


def _valid_rows(rows, cols):
    return lax.broadcasted_iota(jnp.int32, (rows, cols), 0) >= FRONT_PAD


def _attn_rows(h, rows, meta, cos, sin, gain_ref, win_ref, qg_ref, kg_ref, sink_ref, wout_ref,
               proj, qbuf, kbuf, vbuf, obuf):
    u = _rmsnorm(h, gain_ref[...]).astype(BF16)
    proj[0:rows, :] = _dot(u, win_ref[...])

    lane = lax.broadcasted_iota(jnp.int32, (rows, LANES), 1)
    low_head = lane < A_HD
    first_half = (lane % A_HD) < (A_HD // 2)

    def norm_rope(x, g):
        sq = x * x
        s_lo = jnp.sum(jnp.where(low_head, sq, 0.0), axis=-1, keepdims=True)
        s_hi = jnp.sum(jnp.where(low_head, 0.0, sq), axis=-1, keepdims=True)
        r = jnp.where(low_head, lax.rsqrt(s_lo / A_HD + EPS), lax.rsqrt(s_hi / A_HD + EPS))
        xn = x * r * g
        rot = jnp.where(first_half, pltpu.roll(xn, LANES - A_HD // 2, 1), pltpu.roll(xn, A_HD // 2, 1))
        return xn * cos + rot * sin

    qg = qg_ref[...]
    kg = kg_ref[...]
    for g in range(A_W // LANES):
        x = proj[0:rows, g * LANES:(g + 1) * LANES]
        qbuf[0:rows, g * LANES:(g + 1) * LANES] = (norm_rope(x, qg) * (A_HD ** -0.5)).astype(BF16)
    for g in range(A_KVW // LANES):
        x = proj[0:rows, A_W + g * LANES:A_W + (g + 1) * LANES]
        kbuf[BLOCK:BLOCK + rows, g * LANES:(g + 1) * LANES] = norm_rope(x, kg).astype(BF16)
    vbuf[BLOCK:BLOCK + rows, :] = proj[0:rows, A_W + A_KVW:A_W + 2 * A_KVW].astype(BF16)

    t_idx = lax.broadcasted_iota(jnp.int32, (BLOCK, 2 * BLOCK), 0)
    c_idx = lax.broadcasted_iota(jnp.int32, (BLOCK, 2 * BLOCK), 1)
    dist = t_idx + BLOCK - c_idx
    mask = (dist >= 0) & (dist < BLOCK)
    if meta:
        mask = mask & (c_idx >= BLOCK + FRONT_PAD)
        first_mask = mask
    else:
        first_valid = jnp.where(pl.program_id(1) == 0, FRONT_PAD, 0)
        first_mask = mask & (c_idx >= first_valid)

    for n in range(rows // BLOCK):
        r0 = n * BLOCK
        blk_mask = first_mask if n == 0 else mask
        for j in range(A_KV):
            kk = kbuf[r0:r0 + 2 * BLOCK, j * A_HD:(j + 1) * A_HD]
            vv = vbuf[r0:r0 + 2 * BLOCK, j * A_HD:(j + 1) * A_HD]
            q4 = jnp.concatenate(
                [qbuf[r0:r0 + BLOCK, (A_GROUP * j + g) * A_HD:(A_GROUP * j + g + 1) * A_HD]
                 for g in range(A_GROUP)], axis=0)
            s = _dot_nt(q4, kk)
            es, dens = [], []
            for g in range(A_GROUP):
                sg = jnp.where(blk_mask, s[g * BLOCK:(g + 1) * BLOCK], -jnp.inf)
                sink = sink_ref[A_GROUP * j + g]
                m = jnp.maximum(jnp.max(sg, axis=-1, keepdims=True), sink)
                e = jnp.exp(sg - m)
                dens.append(jnp.sum(e, axis=-1, keepdims=True) + jnp.exp(sink - m))
                es.append(e.astype(BF16))
            o = _dot(jnp.concatenate(es, axis=0), vv)
            for g in range(A_GROUP):
                hq = A_GROUP * j + g
                obuf[r0:r0 + BLOCK, hq * A_HD:(hq + 1) * A_HD] = o[g * BLOCK:(g + 1) * BLOCK] / dens[g]

    kbuf[0:BLOCK, :] = kbuf[rows:rows + BLOCK, :]
    vbuf[0:BLOCK, :] = vbuf[rows:rows + BLOCK, :]

    z = proj[0:rows, A_W + 2 * A_KVW:A_IN]
    y = _dot((obuf[0:rows, :] * _silu(z)).astype(BF16), wout_ref[...])
    if meta:
        y = jnp.where(_valid_rows(rows, D_MODEL), y, 0.0)
    return h + y


def _attn_kernel(x_ref, hm_ref, cos_ref, sin_ref, cosm_ref, sinm_ref, gain_ref, win_ref, qg_ref,
                 kg_ref, sink_ref, wout_ref, y_ref, hm_out_ref, proj, qbuf, kbuf, vbuf, obuf, *, tb):
    params = (gain_ref, win_ref, qg_ref, kg_ref, sink_ref, wout_ref, proj, qbuf, kbuf, vbuf, obuf)

    @pl.when(pl.program_id(1) == 0)
    def _():
        kbuf[0:BLOCK, :] = jnp.zeros((BLOCK, A_KVW), BF16)
        vbuf[0:BLOCK, :] = jnp.zeros((BLOCK, A_KVW), BF16)
        hm_out_ref[...] = _attn_rows(hm_ref[...], BLOCK, True, cosm_ref[...], sinm_ref[...], *params)

    y_ref[0] = _attn_rows(x_ref[0], tb, False, cos_ref[...], sin_ref[...], *params)


def _const_spec(shape):
    zeros = (0,) * len(shape)
    return pl.BlockSpec(shape, lambda b, i: zeros, pipeline_mode=pl.Buffered(1))


def _meta_out_spec():
    return pl.BlockSpec((BLOCK, D_MODEL), lambda b, i: (0, 0))


def _attn_layer(x, hm, cos, sin, cosm, sinm, gain, w_in, q_gain, k_gain, sinks, w_out, tb):
    bsz, seq, _ = x.shape
    row_spec = pl.BlockSpec((1, tb, D_MODEL), lambda b, i: (b, i, 0))
    tab_spec = pl.BlockSpec((tb, LANES), lambda b, i: (i, 0))
    return pl.pallas_call(
        functools.partial(_attn_kernel, tb=tb),
        grid=(bsz, seq // tb),
        in_specs=[row_spec, _const_spec((BLOCK, D_MODEL)), tab_spec, tab_spec,
                  _const_spec((BLOCK, LANES)), _const_spec((BLOCK, LANES)),
                  _const_spec((1, D_MODEL)), _const_spec((D_MODEL, A_IN)),
                  _const_spec((1, LANES)), _const_spec((1, LANES)),
                  pl.BlockSpec(memory_space=pltpu.SMEM), _const_spec((A_W, D_MODEL))],
        out_specs=[row_spec, _meta_out_spec()],
        out_shape=[jax.ShapeDtypeStruct(x.shape, F32), jax.ShapeDtypeStruct((BLOCK, D_MODEL), F32)],
        scratch_shapes=[pltpu.VMEM((tb, A_IN), F32), pltpu.VMEM((tb, A_W), BF16),
                        pltpu.VMEM((tb + BLOCK, A_KVW), BF16), pltpu.VMEM((tb + BLOCK, A_KVW), BF16),
                        pltpu.VMEM((tb, A_W), F32)],
        compiler_params=pltpu.CompilerParams(dimension_semantics=("arbitrary", "arbitrary"),
                                             vmem_limit_bytes=VMEM_LIMIT),
        name="swa_layer",
    )(x, hm, cos, sin, cosm, sinm, gain, w_in, q_gain, k_gain, sinks, w_out)


def _mlstm_rows(h, rows, meta, gain_ref, wqk_ref, wv_ref, wo_ref, wz_ref, wg_ref, cw_ref, cb_ref,
                bi_ref, bf_ref, hg_ref, wout_ref, qkraw, qs, ks, vs, ob, zb, gates, gated, cst, nst, mst):
    u = _rmsnorm(h, gain_ref[...]).astype(BF16)
    qkraw[B_HIST:B_HIST + rows, :] = _dot(u, wqk_ref[...])
    vs[0:rows, :] = _dot(u, wv_ref[...]).astype(BF16)
    ob[0:rows, :] = _dot(u, wo_ref[...])
    zb[0:rows, :] = _dot(u, wz_ref[...])
    gates[0:rows, :] = _dot(u, wg_ref[...])

    acc = cb_ref[...]
    for j in range(B_CONV):
        start = B_HIST - (B_CONV - 1) + j
        acc = acc + qkraw[start:start + rows, :] * cw_ref[j:j + 1, :]
    qk = _silu(acc)
    qs[0:rows, :] = qk[:, :B_QK].astype(BF16)
    ks[0:rows, :] = (qk[:, B_QK:] * (B_DK ** -0.5)).astype(BF16)
    qkraw[0:B_HIST, :] = qkraw[rows:rows + B_HIST, :]

    t_idx = lax.broadcasted_iota(jnp.int32, (BLOCK, BLOCK), 0)
    s_idx = lax.broadcasted_iota(jnp.int32, (BLOCK, BLOCK), 1)
    causal = s_idx <= t_idx
    lane8 = lax.broadcasted_iota(jnp.int32, (B_HEADS, BLOCK), 1)

    for c in range(rows // BLOCK):
        r0 = c * BLOCK
        g_t = gates[r0:r0 + BLOCK, :].T
        li = g_t[0:B_HEADS] + bi_ref[...]
        lf = _log_sigmoid(g_t[B_HEADS:2 * B_HEADS] + bf_ref[...])
        if meta:
            li = jnp.where(lane8 >= FRONT_PAD, li, -jnp.inf)
            lf = jnp.where(lane8 >= FRONT_PAD, lf, 0.0)
        a = lf
        shift = 1
        while shift < BLOCK:
            a = a + jnp.where(lane8 >= shift, pltpu.roll(a, shift, 1), 0.0)
            shift *= 2
        g_tot = a[:, BLOCK - 1:BLOCK]
        cols = jnp.concatenate(
            [a, li, jnp.zeros((BLOCK - 2 * B_HEADS, BLOCK), F32)], axis=0).T

        for hd in range(B_HEADS):
            qh = qs[r0:r0 + BLOCK, hd * B_DK:(hd + 1) * B_DK]
            kh = ks[r0:r0 + BLOCK, hd * B_DK:(hd + 1) * B_DK]
            vh = vs[r0:r0 + BLOCK, hd * B_DV:(hd + 1) * B_DV]
            a_col = cols[:, hd:hd + 1]
            li_col = cols[:, B_HEADS + hd:B_HEADS + hd + 1]
            a_row = a[hd:hd + 1, :]
            li_row = li[hd:hd + 1, :]
            g = g_tot[hd:hd + 1, :]
            m_prev = mst[hd:hd + 1, 0:1]
            c_prev = cst[hd]
            n_prev = nst[hd:hd + 1, :]

            dmat = jnp.where(causal, a_col - a_row + li_row, -jnp.inf)
            inter = a_col + m_prev
            m_t = jnp.maximum(inter, jnp.max(dmat, axis=-1, keepdims=True))
            w_inter = jnp.exp(inter - m_t)
            sc = _dot_nt(qh, kh) * jnp.exp(dmat - m_t)
            num = w_inter * _dot(qh, c_prev.astype(BF16)) + _dot(sc.astype(BF16), vh)
            den = (w_inter * jnp.sum(qh.astype(F32) * n_prev, axis=-1, keepdims=True)
                   + jnp.sum(sc, axis=-1, keepdims=True))
            hh = num / jnp.maximum(jnp.abs(den), jnp.exp(-m_t))

            uu = g - a_col + li_col
            m_new = jnp.maximum(g + m_prev, jnp.max(uu, axis=0, keepdims=True))
            w_old = jnp.exp(g + m_prev - m_new)
            w_s = jnp.exp(uu - m_new)
            kw = kh.astype(F32) * w_s
            cst[hd] = w_old * c_prev + _dot_tn(kw.astype(BF16), vh)
            nst[hd:hd + 1, :] = w_old * n_prev + jnp.sum(kw, axis=0, keepdims=True)
            mst[hd:hd + 1, :] = jnp.broadcast_to(m_new, (1, LANES))

            hh = _sigmoid(ob[r0:r0 + BLOCK, hd * B_DV:(hd + 1) * B_DV]) * hh
            hn = _rmsnorm(hh, hg_ref[:, hd * B_DV:(hd + 1) * B_DV])
            zz = zb[r0:r0 + BLOCK, hd * B_DV:(hd + 1) * B_DV]
            gated[r0:r0 + BLOCK, hd * B_DV:(hd + 1) * B_DV] = (hn * _silu(zz)).astype(BF16)

    y = _dot(gated[0:rows, :], wout_ref[...])
    if meta:
        y = jnp.where(_valid_rows(rows, D_MODEL), y, 0.0)
    return h + y


def _mlstm_kernel(x_ref, hm_ref, gain_ref, wqk_ref, wv_ref, wo_ref, wz_ref, wg_ref, cw_ref, cb_ref,
                  bi_ref, bf_ref, hg_ref, wout_ref, y_ref, hm_out_ref,
                  qkraw, qs, ks, vs, ob, zb, gates, gated, cst, nst, mst, *, tb):
    params = (gain_ref, wqk_ref, wv_ref, wo_ref, wz_ref, wg_ref, cw_ref, cb_ref, bi_ref, bf_ref,
              hg_ref, wout_ref, qkraw, qs, ks, vs, ob, zb, gates, gated, cst, nst, mst)

    @pl.when(pl.program_id(1) == 0)
    def _():
        qkraw[0:B_HIST, :] = jnp.zeros((B_HIST, 2 * B_QK), F32)
        cst[...] = jnp.zeros(cst.shape, F32)
        nst[...] = jnp.zeros(nst.shape, F32)
        mst[...] = jnp.zeros(mst.shape, F32)
        hm_out_ref[...] = _mlstm_rows(hm_ref[...], BLOCK, True, *params)

    y_ref[0] = _mlstm_rows(x_ref[0], tb, False, *params)


def _mlstm_layer(x, hm, gain, wqk, wv, wo, wz, wg, conv_w, conv_b, bias_i, bias_f, h_gain, w_out, tb):
    bsz, seq, _ = x.shape
    row_spec = pl.BlockSpec((1, tb, D_MODEL), lambda b, i: (b, i, 0))
    return pl.pallas_call(
        functools.partial(_mlstm_kernel, tb=tb),
        grid=(bsz, seq // tb),
        in_specs=[row_spec, _const_spec((BLOCK, D_MODEL)), _const_spec((1, D_MODEL)),
                  _const_spec((D_MODEL, 2 * B_QK)), _const_spec((D_MODEL, B_W)),
                  _const_spec((D_MODEL, B_W)), _const_spec((D_MODEL, B_W)),
                  _const_spec((D_MODEL, LANES)), _const_spec((B_CONV, 2 * B_QK)),
                  _const_spec((1, 2 * B_QK)), _const_spec((B_HEADS, BLOCK)),
                  _const_spec((B_HEADS, BLOCK)), _const_spec((1, B_W)), _const_spec((B_W, D_MODEL))],
        out_specs=[row_spec, _meta_out_spec()],
        out_shape=[jax.ShapeDtypeStruct(x.shape, F32), jax.ShapeDtypeStruct((BLOCK, D_MODEL), F32)],
        scratch_shapes=[pltpu.VMEM((tb + B_HIST, 2 * B_QK), F32), pltpu.VMEM((tb, B_QK), BF16),
                        pltpu.VMEM((tb, B_QK), BF16), pltpu.VMEM((tb, B_W), BF16),
                        pltpu.VMEM((tb, B_W), F32), pltpu.VMEM((tb, B_W), F32),
                        pltpu.VMEM((tb, LANES), F32), pltpu.VMEM((tb, B_W), BF16),
                        pltpu.VMEM((B_HEADS, B_DK, B_DV), F32), pltpu.VMEM((B_HEADS, B_DK), F32),
                        pltpu.VMEM((B_HEADS, LANES), F32)],
        compiler_params=pltpu.CompilerParams(dimension_semantics=("arbitrary", "arbitrary"),
                                             vmem_limit_bytes=VMEM_LIMIT),
        name="mlstm_layer",
    )(x, hm, gain, wqk, wv, wo, wz, wg, conv_w, conv_b, bias_i, bias_f, h_gain, w_out)


def _hgrn_level_masks():
    t_idx = lax.broadcasted_iota(jnp.int32, (C_CHUNK, C_CHUNK), 0)
    s_idx = lax.broadcasted_iota(jnp.int32, (C_CHUNK, C_CHUNK), 1)
    masks = []
    for b in C_LEVELS:
        same = (t_idx // (2 * b)) == (s_idx // (2 * b))
        masks.append(same & ((t_idx % (2 * b)) >= b) & ((s_idx % (2 * b)) < b))
    return t_idx == s_idx, masks


def _hgrn_reference_rows(ab, r0, b):
    sub = lax.broadcasted_iota(jnp.int32, (8, C_K), 0)
    pieces = []
    if 2 * b >= 8:
        for p in range(C_CHUNK // (2 * b)):
            mid = r0 + 2 * b * p + b - 1
            pieces.append(jnp.broadcast_to(ab[pl.ds(mid, 1), :], (2 * b, C_K)))
    else:
        for v in range(C_CHUNK // 8):
            cand = [jnp.broadcast_to(ab[pl.ds(r0 + 8 * v + 2 * b * p + b - 1, 1), :], (8, C_K))
                    for p in range(8 // (2 * b))]
            sel = cand[-1]
            for p in reversed(range(len(cand) - 1)):
                sel = jnp.where(sub < 2 * b * (p + 1), cand[p], sel)
            pieces.append(sel)
    return jnp.concatenate(pieces, axis=0)


def _chunk_cumsum(x):
    rows = x.shape[0]
    x3 = x.reshape(rows // SUBLANES, SUBLANES, C_K)
    sub = lax.broadcasted_iota(jnp.int32, x3.shape, 1)
    for s in (1, 2, 4):
        x3 = x3 + jnp.where(sub >= s, pltpu.roll(x3, s, 1), 0.0)
    x = x3.reshape(rows, C_K)
    out = []
    for c in range(rows // C_CHUNK):
        run = None
        for g in range(C_CHUNK // SUBLANES):
            r = c * C_CHUNK + g * SUBLANES
            blk = x[r:r + SUBLANES]
            if run is not None:
                blk = blk + run
            out.append(blk)
            run = jnp.broadcast_to(blk[SUBLANES - 1:SUBLANES], (SUBLANES, C_K))
    return jnp.concatenate(out, axis=0)


def _hgrn_rows(h, rows, meta, gain_ref, wq_ref, wf_ref, wi_ref, wz_ref, lb_ref, og_ref, wout_ref,
               qb, kb, ab, vb, zb, gated, st, qin, kneg, kout, ob):
    u = _rmsnorm(h, gain_ref[...]).astype(BF16)
    qb[0:rows, :] = _silu(_dot(u, wq_ref[...]))
    fpre = _dot(u, wf_ref[...])
    log_lb, log_1mlb, one_mlb = lb_ref[0:1, :], lb_ref[1:2, :], lb_ref[2:3, :]
    soft = jnp.log(1.0 + jnp.exp(-jnp.abs(fpre)))
    log_sig = jnp.minimum(fpre, 0.0) - soft
    kk = one_mlb * jnp.exp(log_sig - fpre)
    grow = log_1mlb + log_sig
    lf = jnp.maximum(log_lb, grow) + jnp.log(1.0 + jnp.exp(-jnp.abs(log_lb - grow)))
    if meta:
        valid = _valid_rows(rows, C_K)
        lf = jnp.where(valid, lf, 0.0)
        kk = jnp.where(valid, kk, 0.0)
    ab[0:rows, :] = _chunk_cumsum(lf)
    kb[0:rows, :] = kk
    vb[0:rows, :] = _dot(u, wi_ref[...]).astype(BF16)
    zb[0:rows, :] = _dot(u, wz_ref[...])

    n_chunks = rows // C_CHUNK
    a_end = ab[C_CHUNK - 1:C_CHUNK, :]
    for c in range(1, n_chunks):
        a_end = jnp.minimum(a_end, ab[(c + 1) * C_CHUNK - 1:(c + 1) * C_CHUNK, :])
    mild = jnp.min(a_end) >= -C_SAFE_DECAY

    t_idx = lax.broadcasted_iota(jnp.int32, (C_CHUNK, C_CHUNK), 0)
    s_idx = lax.broadcasted_iota(jnp.int32, (C_CHUNK, C_CHUNK), 1)
    causal = s_idx <= t_idx
    eye, masks = _hgrn_level_masks()

    def chunk(c, carry, factored):
        r0 = pl.multiple_of(c * C_CHUNK, C_CHUNK)
        v = vb[pl.ds(r0, C_CHUNK), :]
        a_last = ab[pl.ds(r0 + C_CHUNK - 1, 1), :]
        decay = jnp.exp(a_last)
        if factored:
            q_in = qin[pl.ds(r0, C_CHUNK), :]
            k_out = kout[pl.ds(r0, C_CHUNK), :]
            q_lv, k_lv, level_masks = [q_in], [kneg[pl.ds(r0, C_CHUNK), :]], [causal]
        else:
            a_c = ab[pl.ds(r0, C_CHUNK), :]
            q = qb[pl.ds(r0, C_CHUNK), :]
            k = kb[pl.ds(r0, C_CHUNK), :]
            q_in = (q * jnp.exp(a_c)).astype(BF16)
            k_out = (k * jnp.exp(a_last - a_c)).astype(BF16)
            q_lv, k_lv, level_masks = [q.astype(BF16)], [k.astype(BF16)], [eye] + masks
            for b in C_LEVELS:
                w = jnp.exp(-jnp.abs(a_c - _hgrn_reference_rows(ab, r0, b)))
                q_lv.append((q * w).astype(BF16))
                k_lv.append((k * w).astype(BF16))
        lanes = [slice(hd * C_DK, (hd + 1) * C_DK) for hd in range(C_HEADS)]
        atts = []
        for ln in lanes:
            att = None
            for ql, kl, mk in zip(q_lv, k_lv, level_masks):
                part = jnp.where(mk, _dot_nt(ql[:, ln], kl[:, ln]), 0.0)
                att = part if att is None else att + part
            atts.append(att.astype(BF16))
        states = [st[hd] for hd in range(C_HEADS)]
        inter = [_dot_nt(q_in[:, ln], s_t.astype(BF16)) for ln, s_t in zip(lanes, states)]
        for hd, ln in enumerate(lanes):
            ob[pl.ds(r0, C_CHUNK), ln] = inter[hd] + _dot(atts[hd], v[:, ln])
        for hd, ln in enumerate(lanes):
            st[hd] = states[hd] * decay[:, ln] + _dot_tn(v[:, ln], k_out[:, ln])
        return carry

    @pl.when(mild)
    def _():
        a = ab[0:rows, :]
        k = kb[0:rows, :]
        a_last = jnp.concatenate(
            [jnp.broadcast_to(ab[(c + 1) * C_CHUNK - 1:(c + 1) * C_CHUNK, :], (C_CHUNK, C_K))
             for c in range(n_chunks)], axis=0)
        qin[0:rows, :] = (qb[0:rows, :] * jnp.exp(a)).astype(BF16)
        kneg[0:rows, :] = (k * jnp.exp(-a)).astype(BF16)
        kout[0:rows, :] = (k * jnp.exp(a_last - a)).astype(BF16)
        lax.fori_loop(0, n_chunks, functools.partial(chunk, factored=True), 0)

    @pl.when(jnp.logical_not(mild))
    def _():
        lax.fori_loop(0, n_chunks, functools.partial(chunk, factored=False), 0)

    for hd in range(C_HEADS):
        ln = slice(hd * C_DK, (hd + 1) * C_DK)
        on = _rmsnorm(ob[0:rows, ln], og_ref[:, ln])
        gated[0:rows, ln] = (on * _silu(zb[0:rows, ln])).astype(BF16)
    y = _dot(gated[0:rows, :], wout_ref[...])
    if meta:
        y = jnp.where(_valid_rows(rows, D_MODEL), y, 0.0)
    return h + y


def _hgrn_kernel(x_ref, hm_ref, gain_ref, wq_ref, wf_ref, wi_ref, wz_ref, lb_ref, og_ref, wout_ref,
                 y_ref, hm_out_ref, qb, kb, ab, vb, zb, gated, st, qin, kneg, kout, ob, *, tb):
    params = (gain_ref, wq_ref, wf_ref, wi_ref, wz_ref, lb_ref, og_ref, wout_ref,
              qb, kb, ab, vb, zb, gated, st, qin, kneg, kout, ob)

    @pl.when(pl.program_id(1) == 0)
    def _():
        st[...] = jnp.zeros(st.shape, F32)
        hm_out_ref[...] = _hgrn_rows(hm_ref[...], BLOCK, True, *params)

    y_ref[0] = _hgrn_rows(x_ref[0], tb, False, *params)


def _hgrn_layer(x, hm, gain, wq, wf, wi, wz, lb, o_gain, w_out, tb):
    bsz, seq, _ = x.shape
    row_spec = pl.BlockSpec((1, tb, D_MODEL), lambda b, i: (b, i, 0))
    wspec = _const_spec((D_MODEL, C_K))
    return pl.pallas_call(
        functools.partial(_hgrn_kernel, tb=tb),
        grid=(bsz, seq // tb),
        in_specs=[row_spec, _const_spec((BLOCK, D_MODEL)), _const_spec((1, D_MODEL)),
                  wspec, wspec, wspec, wspec, _const_spec((SUBLANES, C_K)), _const_spec((1, C_W)),
                  _const_spec((C_W, D_MODEL))],
        out_specs=[row_spec, _meta_out_spec()],
        out_shape=[jax.ShapeDtypeStruct(x.shape, F32), jax.ShapeDtypeStruct((BLOCK, D_MODEL), F32)],
        scratch_shapes=[pltpu.VMEM((tb, C_K), F32), pltpu.VMEM((tb, C_K), F32),
                        pltpu.VMEM((tb, C_K), F32), pltpu.VMEM((tb, C_W), BF16),
                        pltpu.VMEM((tb, C_W), F32), pltpu.VMEM((tb, C_W), BF16),
                        pltpu.VMEM((C_HEADS, C_DV, C_DK), F32),
                        pltpu.VMEM((tb, C_K), BF16), pltpu.VMEM((tb, C_K), BF16),
                        pltpu.VMEM((tb, C_K), BF16), pltpu.VMEM((tb, C_W), F32)],
        compiler_params=pltpu.CompilerParams(dimension_semantics=("arbitrary", "arbitrary"),
                                             vmem_limit_bytes=VMEM_LIMIT),
        name="hgrn2_layer",
    )(x, hm, gain, wq, wf, wi, wz, lb, o_gain, w_out)


def _rope_tables(seq):
    half = A_HD // 2
    inv = ROPE_THETA ** (-jnp.arange(half, dtype=F32) / half)
    pos = (jnp.arange(seq + BLOCK) - FRONT_PAD).astype(jnp.int32)
    ang = pos.astype(F32)[:, None] * inv[None, :]
    cos, sin = jnp.cos(ang), jnp.sin(ang)
    reps = LANES // A_HD
    cos_t = jnp.tile(jnp.concatenate([cos, cos], axis=-1), (1, reps))
    sin_t = jnp.tile(jnp.concatenate([-sin, sin], axis=-1), (1, reps))
    return cos_t[BLOCK:], sin_t[BLOCK:], cos_t[:BLOCK], sin_t[:BLOCK]


def _row_tile(seq, want):
    tb = want
    while seq % tb:
        tb //= 2
    assert tb >= BLOCK, "sequence length must be a multiple of the 128-token block"
    return tb


def kernel(x, meta, norm_gain, a_w_in, a_q_gain, a_k_gain, a_sinks, a_w_out, b_w_in, b_conv_w,
           b_conv_b, b_gate_bias, b_h_gain, b_w_out, c_w_in, c_gamma, c_o_gain, c_w_out):
    bsz, seq, _ = x.shape
    depth = norm_gain.shape[0]
    cos, sin, cosm, sinm = _rope_tables(seq)
    hm = jnp.concatenate([jnp.zeros((FRONT_PAD, D_MODEL), x.dtype), meta.astype(x.dtype)], axis=0)
    p = jax.nn.softmax(c_gamma.astype(F32), axis=0)
    lower_bounds = jnp.cumsum(p, axis=0) - p
    reps = LANES // A_HD

    h = x
    for i in range(depth):
        kind, j = i % 3, i // 3
        gain = norm_gain[i].reshape(1, D_MODEL).astype(F32)
        if kind == 0:
            h, hm = _attn_layer(
                h, hm, cos, sin, cosm, sinm, gain, a_w_in[j].astype(BF16),
                jnp.tile(a_q_gain[j].astype(F32), reps).reshape(1, LANES),
                jnp.tile(a_k_gain[j].astype(F32), reps).reshape(1, LANES),
                a_sinks[j].astype(F32), a_w_out[j].astype(BF16), _row_tile(seq, 512))
        elif kind == 1:
            w = b_w_in[j]
            o_v, o_i, o_o, o_z = 2 * B_QK, 2 * B_QK + B_W, 2 * B_QK + B_W + 2 * B_HEADS, 2 * B_QK + 2 * B_W + 2 * B_HEADS
            wg = jnp.pad(w[:, o_i:o_o], ((0, 0), (0, LANES - 2 * B_HEADS)))
            gb = b_gate_bias[j].astype(F32)
            h, hm = _mlstm_layer(
                h, hm, gain, w[:, :o_v].astype(BF16), w[:, o_v:o_i].astype(BF16),
                w[:, o_o:o_z].astype(BF16), w[:, o_z:].astype(BF16), wg.astype(BF16),
                b_conv_w[j].astype(F32), b_conv_b[j].reshape(1, 2 * B_QK).astype(F32),
                jnp.broadcast_to(gb[:B_HEADS, None], (B_HEADS, BLOCK)),
                jnp.broadcast_to(gb[B_HEADS:, None], (B_HEADS, BLOCK)),
                b_h_gain[j].reshape(1, B_W).astype(F32), b_w_out[j].astype(BF16), _row_tile(seq, 256))
        else:
            w = c_w_in[j].astype(BF16)
            lb = lower_bounds[i]
            lb_rows = jnp.zeros((SUBLANES, C_K), F32)
            lb_rows = lb_rows.at[0].set(jnp.log(lb)).at[1].set(jnp.log1p(-lb)).at[2].set(1.0 - lb)
            h, hm = _hgrn_layer(
                h, hm, gain, w[:, :C_K], w[:, C_K:2 * C_K], w[:, 2 * C_K:3 * C_K], w[:, 3 * C_K:],
                lb_rows, c_o_gain[j].reshape(1, C_W).astype(F32),
                c_w_out[j].astype(BF16), _row_tile(seq, 512))
    return h
```

```python
import functools

import jax
import jax.numpy as jnp
from jax import lax
from jax.experimental import pallas as pl
from jax.experimental.pallas import tpu as pltpu

F32 = jnp.float32
BF16 = jnp.bfloat16

D_MODEL = 1024
BLOCK = 128
N_META = 16
FRONT_PAD = BLOCK - N_META
EPS = 1e-6
ROPE_THETA = 10000.0
LANES = 128
SUBLANES = 8

A_HEADS, A_KV, A_HD = 16, 4, 64
A_GROUP = A_HEADS // A_KV
A_W = A_HEADS * A_HD
A_KVW = A_KV * A_HD
A_IN = 2 * A_W + 2 * A_KVW

B_HEADS, B_DK, B_DV, B_CONV = 8, 128, 256, 4
B_QK = B_HEADS * B_DK
B_W = B_HEADS * B_DV
B_HIST = 8

C_HEADS, C_DK, C_DV, C_CHUNK = 8, 128, 128, 64
C_K = C_HEADS * C_DK
C_W = C_HEADS * C_DV
C_LEVELS = (1, 2, 4, 8, 16, 32)
C_SAFE_DECAY = 60.0

VMEM_LIMIT = 56 * 1024 * 1024


def _sigmoid(x):
    return 0.5 + 0.5 * jnp.tanh(0.5 * x)


def _silu(x):
    return x * _sigmoid(x)


def _log_sigmoid(x):
    return jnp.minimum(x, 0.0) - jnp.log(1.0 + jnp.exp(-jnp.abs(x)))


def _rmsnorm(x, g):
    ms = jnp.mean(x * x, axis=-1, keepdims=True)
    return x * lax.rsqrt(ms + EPS) * g


def _dot(a, b):
    return jnp.dot(a, b, preferred_element_type=F32)


def _dot_nt(a, b):
    return lax.dot_general(a, b, (((1,), (1,)), ((), ())), preferred_element_type=F32)


def _dot_tn(a, b):
    return lax.dot_general(a, b, (((0,), (0,)), ((), ())), preferred_element_type=F32)


def _valid_rows(rows, cols):
    return lax.broadcasted_iota(jnp.int32, (rows, cols), 0) >= FRONT_PAD


def _attn_head_order():
    order = []
    for p in range(A_W // LANES):
        m, g = divmod(p, A_GROUP)
        order += [A_GROUP * (2 * m) + g, A_GROUP * (2 * m + 1) + g]
    return order


def _attn_rows(h, rows, meta, tab, gain_ref, win_ref, sink_ref, wout_ref, proj, qbuf, klo, khi, vt, obuf):
    u = _rmsnorm(h, gain_ref[...]).astype(BF16)
    proj[0:rows, :] = _dot(u, win_ref[...])

    lane = lax.broadcasted_iota(jnp.int32, (rows, LANES), 1)
    low_head = lane < A_HD
    first_half = (lane % A_HD) < (A_HD // 2)

    def norm_rope(x, ctab, stab):
        sq = x * x
        s_lo = jnp.sum(jnp.where(low_head, sq, 0.0), axis=-1, keepdims=True)
        s_hi = jnp.sum(jnp.where(low_head, 0.0, sq), axis=-1, keepdims=True)
        r = jnp.where(low_head, lax.rsqrt(s_lo / A_HD + EPS), lax.rsqrt(s_hi / A_HD + EPS))
        rot = jnp.where(first_half, pltpu.roll(x, LANES - A_HD // 2, 1), pltpu.roll(x, A_HD // 2, 1))
        return (x * ctab + rot * stab) * r

    q_cos, q_sin = tab[:, 0:LANES], tab[:, LANES:2 * LANES]
    k_cos, k_sin = tab[:, 2 * LANES:3 * LANES], tab[:, 3 * LANES:4 * LANES]
    for p in range(A_W // LANES):
        x = proj[0:rows, p * LANES:(p + 1) * LANES]
        qbuf[0:rows, p * LANES:(p + 1) * LANES] = norm_rope(x, q_cos, q_sin).astype(BF16)
    for m in range(A_KVW // LANES):
        k = norm_rope(proj[0:rows, A_W + m * LANES:A_W + (m + 1) * LANES], k_cos, k_sin)
        klo[BLOCK:BLOCK + rows, m * LANES:(m + 1) * LANES] = jnp.where(low_head, k, 0.0).astype(BF16)
        khi[BLOCK:BLOCK + rows, m * LANES:(m + 1) * LANES] = jnp.where(low_head, 0.0, k).astype(BF16)
    for n in range(rows // BLOCK):
        for m in range(A_KVW // LANES):
            v = proj[n * BLOCK:(n + 1) * BLOCK, A_W + A_KVW + m * LANES:A_W + A_KVW + (m + 1) * LANES]
            vt[m * LANES:(m + 1) * LANES, (n + 1) * BLOCK:(n + 2) * BLOCK] = v.T.astype(BF16)

    c_idx = lax.broadcasted_iota(jnp.int32, (2 * BLOCK, BLOCK), 0)
    t_idx = lax.broadcasted_iota(jnp.int32, (2 * BLOCK, BLOCK), 1)
    dist = t_idx + BLOCK - c_idx
    mask = (dist >= 0) & (dist < BLOCK)
    if meta:
        mask = mask & (c_idx >= BLOCK + FRONT_PAD)
        first_mask = mask
    else:
        first_valid = jnp.where(pl.program_id(1) == 0, FRONT_PAD, 0)
        first_mask = mask & (c_idx >= first_valid)

    def scores(n):
        r0 = n * BLOCK
        out = []
        for m in range(A_KVW // LANES):
            keys = jnp.concatenate([klo[r0:r0 + 2 * BLOCK, m * LANES:(m + 1) * LANES],
                                    khi[r0:r0 + 2 * BLOCK, m * LANES:(m + 1) * LANES]], axis=0)
            qs = jnp.concatenate([qbuf[r0:r0 + BLOCK, (A_GROUP * m + g) * LANES:(A_GROUP * m + g + 1) * LANES]
                                  for g in range(A_GROUP)], axis=0)
            out.append(_dot_nt(keys, qs))
        return out

    n_blocks = rows // BLOCK
    pending = scores(0)
    for n in range(n_blocks):
        r0 = n * BLOCK
        blk_mask = first_mask if n == 0 else mask
        s_now = pending
        if n + 1 < n_blocks:
            pending = scores(n + 1)
        for m in range(A_KVW // LANES):
            outs = []
            for half in range(2):
                j = 2 * m + half
                es, inv = [], []
                for g in range(A_GROUP):
                    tile = s_now[m][half * 2 * BLOCK:(half + 1) * 2 * BLOCK, g * BLOCK:(g + 1) * BLOCK]
                    tile = jnp.where(blk_mask, tile, -jnp.inf)
                    sink = sink_ref[A_GROUP * j + g]
                    mx = jnp.maximum(jnp.max(tile, axis=0, keepdims=True), sink)
                    e = jnp.exp(tile - mx)
                    inv.append(1.0 / (jnp.sum(e, axis=0, keepdims=True) + jnp.exp(sink - mx)))
                    es.append(e.astype(BF16))
                o_t = _dot(vt[j * A_HD:(j + 1) * A_HD, r0:r0 + 2 * BLOCK], jnp.concatenate(es, axis=1))
                outs.append(o_t * jnp.concatenate(inv, axis=1))
            for g in range(A_GROUP):
                pair = jnp.concatenate([outs[0][:, g * BLOCK:(g + 1) * BLOCK],
                                        outs[1][:, g * BLOCK:(g + 1) * BLOCK]], axis=0)
                p = A_GROUP * m + g
                obuf[r0:r0 + BLOCK, p * LANES:(p + 1) * LANES] = pair.T

    klo[0:BLOCK, :] = klo[rows:rows + BLOCK, :]
    khi[0:BLOCK, :] = khi[rows:rows + BLOCK, :]
    vt[:, 0:BLOCK] = vt[:, rows:rows + BLOCK]

    z = proj[0:rows, A_W + 2 * A_KVW:A_IN]
    y = _dot((obuf[0:rows, :] * _silu(z)).astype(BF16), wout_ref[...])
    if meta:
        y = jnp.where(_valid_rows(rows, D_MODEL), y, 0.0)
    return h + y


def _attn_kernel(x_ref, hm_ref, tab_ref, tabm_ref, gain_ref, win_ref, sink_ref, wout_ref,
                 y_ref, hm_out_ref, proj, qbuf, klo, khi, vt, obuf, *, tb):
    params = (gain_ref, win_ref, sink_ref, wout_ref, proj, qbuf, klo, khi, vt, obuf)

    @pl.when(pl.program_id(1) == 0)
    def _():
        klo[0:BLOCK, :] = jnp.zeros((BLOCK, A_KVW), BF16)
        khi[0:BLOCK, :] = jnp.zeros((BLOCK, A_KVW), BF16)
        vt[:, 0:BLOCK] = jnp.zeros((A_KVW, BLOCK), BF16)
        hm_out_ref[...] = _attn_rows(hm_ref[...], BLOCK, True, tabm_ref[...], *params)

    y_ref[0] = _attn_rows(x_ref[0], tb, False, tab_ref[...], *params)


def _const_spec(shape):
    zeros = (0,) * len(shape)
    return pl.BlockSpec(shape, lambda b, i: zeros, pipeline_mode=pl.Buffered(1))


def _meta_out_spec():
    return pl.BlockSpec((BLOCK, D_MODEL), lambda b, i: (0, 0))


def _attn_layer(x, hm, tab, tabm, gain, w_in, sinks, w_out, tb):
    bsz, seq, _ = x.shape
    row_spec = pl.BlockSpec((1, tb, D_MODEL), lambda b, i: (b, i, 0))
    return pl.pallas_call(
        functools.partial(_attn_kernel, tb=tb),
        grid=(bsz, seq // tb),
        in_specs=[row_spec, _const_spec((BLOCK, D_MODEL)),
                  pl.BlockSpec((tb, 4 * LANES), lambda b, i: (i, 0)), _const_spec((BLOCK, 4 * LANES)),
                  _const_spec((1, D_MODEL)), _const_spec((D_MODEL, A_IN)),
                  pl.BlockSpec(memory_space=pltpu.SMEM), _const_spec((A_W, D_MODEL))],
        out_specs=[row_spec, _meta_out_spec()],
        out_shape=[jax.ShapeDtypeStruct(x.shape, F32), jax.ShapeDtypeStruct((BLOCK, D_MODEL), F32)],
        scratch_shapes=[pltpu.VMEM((tb, A_IN), F32), pltpu.VMEM((tb, A_W), BF16),
                        pltpu.VMEM((tb + BLOCK, A_KVW), BF16), pltpu.VMEM((tb + BLOCK, A_KVW), BF16),
                        pltpu.VMEM((A_KVW, tb + BLOCK), BF16), pltpu.VMEM((tb, A_W), F32)],
        compiler_params=pltpu.CompilerParams(dimension_semantics=("arbitrary", "arbitrary"),
                                             vmem_limit_bytes=VMEM_LIMIT),
        name="swa_layer",
    )(x, hm, tab, tabm, gain, w_in, sinks, w_out)


def _mlstm_rows(h, rows, meta, gain_ref, wqk_ref, wv_ref, wo_ref, wz_ref, wg_ref, cw_ref, cb_ref,
                bi_ref, bf_ref, hg_ref, wout_ref, ub, qkraw, qs, ks, vs, hb, gates, gated, cst, mst):
    ub[0:rows, :] = _rmsnorm(h, gain_ref[...]).astype(BF16)
    u = ub[0:rows, :]
    qkraw[B_HIST:B_HIST + rows, :] = _dot(u, wqk_ref[...])
    vs[0:rows, :] = _dot(u, wv_ref[...]).astype(BF16)
    gates[0:rows, :] = _dot(u, wg_ref[...])

    acc = cb_ref[...]
    for j in range(B_CONV):
        start = B_HIST - (B_CONV - 1) + j
        acc = acc + qkraw[start:start + rows, :] * cw_ref[j:j + 1, :]
    qk = _silu(acc)
    qs[0:rows, :] = qk[:, :B_QK].astype(BF16)
    ks[0:rows, :] = (qk[:, B_QK:] * (B_DK ** -0.5)).astype(BF16)
    qkraw[0:B_HIST, :] = qkraw[rows:rows + B_HIST, :]

    t_idx = lax.broadcasted_iota(jnp.int32, (BLOCK, BLOCK), 0)
    s_idx = lax.broadcasted_iota(jnp.int32, (BLOCK, BLOCK), 1)
    causal = s_idx <= t_idx
    lane8 = lax.broadcasted_iota(jnp.int32, (B_HEADS, BLOCK), 1)

    def chunk(c, carry):
        r0 = pl.multiple_of(c * BLOCK, BLOCK)
        g_t = gates[pl.ds(r0, BLOCK), :].T
        li = g_t[0:B_HEADS] + bi_ref[...]
        lf = _log_sigmoid(g_t[B_HEADS:2 * B_HEADS] + bf_ref[...])
        if meta:
            li = jnp.where(lane8 >= FRONT_PAD, li, -jnp.inf)
            lf = jnp.where(lane8 >= FRONT_PAD, lf, 0.0)
        a = lf
        shift = 1
        while shift < BLOCK:
            a = a + jnp.where(lane8 >= shift, pltpu.roll(a, shift, 1), 0.0)
            shift *= 2
        run = li - a
        shift = 1
        while shift < BLOCK:
            run = jnp.maximum(run, jnp.where(lane8 >= shift, pltpu.roll(run, shift, 1), -jnp.inf))
            shift *= 2
        m_prev = mst[:, 0:1]
        decay_max = jnp.maximum(m_prev, run)
        m_t = a + decay_max
        g_tot = a[:, BLOCK - 1:BLOCK]
        uu = g_tot - a + li
        m_new = jnp.maximum(g_tot + m_prev, jnp.max(uu, axis=1, keepdims=True))
        w_old = jnp.exp(g_tot + m_prev - m_new)
        w_s = jnp.exp(uu - m_new)
        b_row = li - a
        mst[...] = jnp.broadcast_to(m_new, (B_HEADS, LANES))

        def col_tile(row):
            return jnp.broadcast_to(row, (BLOCK, BLOCK)).T

        heads = range(B_HEADS)
        ones = jnp.ones((BLOCK, LANES), BF16)
        qh = [qs[pl.ds(r0, BLOCK), hd * B_DK:(hd + 1) * B_DK] for hd in heads]
        kh = [ks[pl.ds(r0, BLOCK), hd * B_DK:(hd + 1) * B_DK] for hd in heads]
        vh = [jnp.concatenate([vs[pl.ds(r0, BLOCK), hd * B_DV:(hd + 1) * B_DV], ones], axis=1) for hd in heads]
        c_prev = [cst[hd] for hd in heads]
        qk = [_dot_nt(qh[hd], kh[hd]) for hd in heads]
        inter_c = [_dot(qh[hd], c_prev[hd].astype(BF16)) for hd in heads]

        sc, kw, w_inter, floor = [], [], [], []
        for hd in heads:
            dm_t = col_tile(decay_max[hd:hd + 1, :])
            w = jnp.exp(jnp.where(causal, b_row[hd:hd + 1, :] - dm_t, -jnp.inf))
            sc.append((qk[hd] * w).astype(BF16))
            kw.append((kh[hd].astype(F32) * col_tile(w_s[hd:hd + 1, :])).astype(BF16))
            w_inter.append(jnp.exp(m_prev[hd:hd + 1, :] - dm_t))
            floor.append(jnp.exp(-col_tile(m_t[hd:hd + 1, :])))

        pv = [_dot(sc[hd], vh[hd]) for hd in heads]
        kv = [_dot_tn(kw[hd], vh[hd]) for hd in heads]
        for hd in heads:
            den = w_inter[hd] * inter_c[hd][:, B_DV:] + pv[hd][:, B_DV:]
            scale = 1.0 / jnp.maximum(jnp.abs(den), floor[hd])
            for half in range(B_DV // LANES):
                ls = slice(half * LANES, (half + 1) * LANES)
                hb[pl.ds(r0, BLOCK), hd * B_DV + half * LANES:hd * B_DV + (half + 1) * LANES] = (
                    (w_inter[hd] * inter_c[hd][:, ls] + pv[hd][:, ls]) * scale)
            cst[hd] = w_old[hd:hd + 1, :] * c_prev[hd] + kv[hd]
        return carry

    lax.fori_loop(0, rows // BLOCK, chunk, 0)

    for hd in range(B_HEADS):
        cs = slice(hd * B_DV, (hd + 1) * B_DV)
        hh = _sigmoid(_dot(u, wo_ref[:, cs])) * hb[0:rows, cs]
        gated[0:rows, cs] = (_rmsnorm(hh, hg_ref[:, cs]) * _silu(_dot(u, wz_ref[:, cs]))).astype(BF16)
    y = _dot(gated[0:rows, :], wout_ref[...])
    if meta:
        y = jnp.where(_valid_rows(rows, D_MODEL), y, 0.0)
    return h + y


def _mlstm_kernel(x_ref, hm_ref, gain_ref, wqk_ref, wv_ref, wo_ref, wz_ref, wg_ref, cw_ref, cb_ref,
                  bi_ref, bf_ref, hg_ref, wout_ref, y_ref, hm_out_ref,
                  ub, qkraw, qs, ks, vs, hb, gates, gated, cst, mst, *, tb):
    params = (gain_ref, wqk_ref, wv_ref, wo_ref, wz_ref, wg_ref, cw_ref, cb_ref, bi_ref, bf_ref,
              hg_ref, wout_ref, ub, qkraw, qs, ks, vs, hb, gates, gated, cst, mst)

    @pl.when(pl.program_id(1) == 0)
    def _():
        qkraw[0:B_HIST, :] = jnp.zeros((B_HIST, 2 * B_QK), F32)
        cst[...] = jnp.zeros(cst.shape, F32)
        mst[...] = jnp.zeros(mst.shape, F32)
        hm_out_ref[...] = _mlstm_rows(hm_ref[...], BLOCK, True, *params)

    y_ref[0] = _mlstm_rows(x_ref[0], tb, False, *params)


def _mlstm_layer(x, hm, gain, wqk, wv, wo, wz, wg, conv_w, conv_b, bias_i, bias_f, h_gain, w_out, tb):
    bsz, seq, _ = x.shape
    row_spec = pl.BlockSpec((1, tb, D_MODEL), lambda b, i: (b, i, 0))
    return pl.pallas_call(
        functools.partial(_mlstm_kernel, tb=tb),
        grid=(bsz, seq // tb),
        in_specs=[row_spec, _const_spec((BLOCK, D_MODEL)), _const_spec((1, D_MODEL)),
                  _const_spec((D_MODEL, 2 * B_QK)), _const_spec((D_MODEL, B_W)),
                  _const_spec((D_MODEL, B_W)), _const_spec((D_MODEL, B_W)),
                  _const_spec((D_MODEL, LANES)), _const_spec((B_CONV, 2 * B_QK)),
                  _const_spec((1, 2 * B_QK)), _const_spec((B_HEADS, BLOCK)),
                  _const_spec((B_HEADS, BLOCK)), _const_spec((1, B_W)), _const_spec((B_W, D_MODEL))],
        out_specs=[row_spec, _meta_out_spec()],
        out_shape=[jax.ShapeDtypeStruct(x.shape, F32), jax.ShapeDtypeStruct((BLOCK, D_MODEL), F32)],
        scratch_shapes=[pltpu.VMEM((tb, D_MODEL), BF16),
                        pltpu.VMEM((tb + B_HIST, 2 * B_QK), F32), pltpu.VMEM((tb, B_QK), BF16),
                        pltpu.VMEM((tb, B_QK), BF16), pltpu.VMEM((tb, B_W), BF16),
                        pltpu.VMEM((tb, B_W), F32),
                        pltpu.VMEM((tb, LANES), F32), pltpu.VMEM((tb, B_W), BF16),
                        pltpu.VMEM((B_HEADS, B_DK, B_DV + LANES), F32), pltpu.VMEM((B_HEADS, LANES), F32)],
        compiler_params=pltpu.CompilerParams(dimension_semantics=("arbitrary", "arbitrary"),
                                             vmem_limit_bytes=VMEM_LIMIT),
        name="mlstm_layer",
    )(x, hm, gain, wqk, wv, wo, wz, wg, conv_w, conv_b, bias_i, bias_f, h_gain, w_out)


def _hgrn_level_masks():
    t_idx = lax.broadcasted_iota(jnp.int32, (C_CHUNK, C_CHUNK), 0)
    s_idx = lax.broadcasted_iota(jnp.int32, (C_CHUNK, C_CHUNK), 1)
    masks = []
    for b in C_LEVELS:
        same = (t_idx // (2 * b)) == (s_idx // (2 * b))
        masks.append(same & ((t_idx % (2 * b)) >= b) & ((s_idx % (2 * b)) < b))
    return t_idx == s_idx, masks


def _hgrn_reference_rows(ab, r0, b):
    sub = lax.broadcasted_iota(jnp.int32, (8, C_K), 0)
    pieces = []
    if 2 * b >= 8:
        for p in range(C_CHUNK // (2 * b)):
            mid = r0 + 2 * b * p + b - 1
            pieces.append(jnp.broadcast_to(ab[pl.ds(mid, 1), :], (2 * b, C_K)))
    else:
        for v in range(C_CHUNK // 8):
            cand = [jnp.broadcast_to(ab[pl.ds(r0 + 8 * v + 2 * b * p + b - 1, 1), :], (8, C_K))
                    for p in range(8 // (2 * b))]
            sel = cand[-1]
            for p in reversed(range(len(cand) - 1)):
                sel = jnp.where(sub < 2 * b * (p + 1), cand[p], sel)
            pieces.append(sel)
    return jnp.concatenate(pieces, axis=0)


def _chunk_cumsum(x):
    rows = x.shape[0]
    x3 = x.reshape(rows // SUBLANES, SUBLANES, C_K)
    sub = lax.broadcasted_iota(jnp.int32, x3.shape, 1)
    for s in (1, 2, 4):
        x3 = x3 + jnp.where(sub >= s, pltpu.roll(x3, s, 1), 0.0)
    x = x3.reshape(rows, C_K)
    out = []
    for c in range(rows // C_CHUNK):
        run = None
        for g in range(C_CHUNK // SUBLANES):
            r = c * C_CHUNK + g * SUBLANES
            blk = x[r:r + SUBLANES]
            if run is not None:
                blk = blk + run
            out.append(blk)
            run = jnp.broadcast_to(blk[SUBLANES - 1:SUBLANES], (SUBLANES, C_K))
    return jnp.concatenate(out, axis=0)


def _hgrn_rows(h, rows, meta, gain_ref, wq_ref, wf_ref, wi_ref, wz_ref, lb_ref, og_ref, wout_ref,
               qb, kb, ab, vb, zb, gated, st, qin, kneg, kout, ob):
    u = _rmsnorm(h, gain_ref[...]).astype(BF16)
    qb[0:rows, :] = _silu(_dot(u, wq_ref[...]))
    fpre = _dot(u, wf_ref[...])
    log_lb, log_1mlb, one_mlb = lb_ref[0:1, :], lb_ref[1:2, :], lb_ref[2:3, :]
    soft = jnp.log(1.0 + jnp.exp(-jnp.abs(fpre)))
    log_sig = jnp.minimum(fpre, 0.0) - soft
    kk = one_mlb * jnp.exp(log_sig - fpre)
    grow = log_1mlb + log_sig
    lf = jnp.maximum(log_lb, grow) + jnp.log(1.0 + jnp.exp(-jnp.abs(log_lb - grow)))
    if meta:
        valid = _valid_rows(rows, C_K)
        lf = jnp.where(valid, lf, 0.0)
        kk = jnp.where(valid, kk, 0.0)
    ab[0:rows, :] = _chunk_cumsum(lf)
    kb[0:rows, :] = kk
    vb[0:rows, :] = _dot(u, wi_ref[...]).astype(BF16)
    zb[0:rows, :] = _dot(u, wz_ref[...])

    n_chunks = rows // C_CHUNK
    a_end = ab[C_CHUNK - 1:C_CHUNK, :]
    for c in range(1, n_chunks):
        a_end = jnp.minimum(a_end, ab[(c + 1) * C_CHUNK - 1:(c + 1) * C_CHUNK, :])
    mild = jnp.min(a_end) >= -C_SAFE_DECAY

    t_idx = lax.broadcasted_iota(jnp.int32, (C_CHUNK, C_CHUNK), 0)
    s_idx = lax.broadcasted_iota(jnp.int32, (C_CHUNK, C_CHUNK), 1)
    causal = s_idx <= t_idx
    eye, masks = _hgrn_level_masks()

    def chunk(c, carry, factored):
        r0 = pl.multiple_of(c * C_CHUNK, C_CHUNK)
        v = vb[pl.ds(r0, C_CHUNK), :]
        a_last = ab[pl.ds(r0 + C_CHUNK - 1, 1), :]
        decay = jnp.exp(a_last)
        if factored:
            q_in = qin[pl.ds(r0, C_CHUNK), :]
            k_out = kout[pl.ds(r0, C_CHUNK), :]
            q_lv, k_lv, level_masks = [q_in], [kneg[pl.ds(r0, C_CHUNK), :]], [causal]
        else:
            a_c = ab[pl.ds(r0, C_CHUNK), :]
            q = qb[pl.ds(r0, C_CHUNK), :]
            k = kb[pl.ds(r0, C_CHUNK), :]
            q_in = (q * jnp.exp(a_c)).astype(BF16)
            k_out = (k * jnp.exp(a_last - a_c)).astype(BF16)
            q_lv, k_lv, level_masks = [q.astype(BF16)], [k.astype(BF16)], [eye] + masks
            for b in C_LEVELS:
                w = jnp.exp(-jnp.abs(a_c - _hgrn_reference_rows(ab, r0, b)))
                q_lv.append((q * w).astype(BF16))
                k_lv.append((k * w).astype(BF16))
        lanes = [slice(hd * C_DK, (hd + 1) * C_DK) for hd in range(C_HEADS)]
        atts = []
        for ln in lanes:
            att = None
            for ql, kl, mk in zip(q_lv, k_lv, level_masks):
                part = jnp.where(mk, _dot_nt(ql[:, ln], kl[:, ln]), 0.0)
                att = part if att is None else att + part
            atts.append(att.astype(BF16))
        states = [st[hd] for hd in range(C_HEADS)]
        inter = [_dot_nt(q_in[:, ln], s_t.astype(BF16)) for ln, s_t in zip(lanes, states)]
        for hd, ln in enumerate(lanes):
            ob[pl.ds(r0, C_CHUNK), ln] = inter[hd] + _dot(atts[hd], v[:, ln])
        for hd, ln in enumerate(lanes):
            st[hd] = states[hd] * decay[:, ln] + _dot_tn(v[:, ln], k_out[:, ln])
        return carry

    @pl.when(mild)
    def _():
        a = ab[0:rows, :]
        k = kb[0:rows, :]
        a_last = jnp.concatenate(
            [jnp.broadcast_to(ab[(c + 1) * C_CHUNK - 1:(c + 1) * C_CHUNK, :], (C_CHUNK, C_K))
             for c in range(n_chunks)], axis=0)
        qin[0:rows, :] = (qb[0:rows, :] * jnp.exp(a)).astype(BF16)
        kneg[0:rows, :] = (k * jnp.exp(-a)).astype(BF16)
        kout[0:rows, :] = (k * jnp.exp(a_last - a)).astype(BF16)
        lax.fori_loop(0, n_chunks, functools.partial(chunk, factored=True), 0)

    @pl.when(jnp.logical_not(mild))
    def _():
        lax.fori_loop(0, n_chunks, functools.partial(chunk, factored=False), 0)

    for hd in range(C_HEADS):
        ln = slice(hd * C_DK, (hd + 1) * C_DK)
        on = _rmsnorm(ob[0:rows, ln], og_ref[:, ln])
        gated[0:rows, ln] = (on * _silu(zb[0:rows, ln])).astype(BF16)
    y = _dot(gated[0:rows, :], wout_ref[...])
    if meta:
        y = jnp.where(_valid_rows(rows, D_MODEL), y, 0.0)
    return h + y


def _hgrn_kernel(x_ref, hm_ref, gain_ref, wq_ref, wf_ref, wi_ref, wz_ref, lb_ref, og_ref, wout_ref,
                 y_ref, hm_out_ref, qb, kb, ab, vb, zb, gated, st, qin, kneg, kout, ob, *, tb):
    params = (gain_ref, wq_ref, wf_ref, wi_ref, wz_ref, lb_ref, og_ref, wout_ref,
              qb, kb, ab, vb, zb, gated, st, qin, kneg, kout, ob)

    @pl.when(pl.program_id(1) == 0)
    def _():
        st[...] = jnp.zeros(st.shape, F32)
        hm_out_ref[...] = _hgrn_rows(hm_ref[...], BLOCK, True, *params)

    y_ref[0] = _hgrn_rows(x_ref[0], tb, False, *params)


def _hgrn_layer(x, hm, gain, wq, wf, wi, wz, lb, o_gain, w_out, tb):
    bsz, seq, _ = x.shape
    row_spec = pl.BlockSpec((1, tb, D_MODEL), lambda b, i: (b, i, 0))
    wspec = _const_spec((D_MODEL, C_K))
    return pl.pallas_call(
        functools.partial(_hgrn_kernel, tb=tb),
        grid=(bsz, seq // tb),
        in_specs=[row_spec, _const_spec((BLOCK, D_MODEL)), _const_spec((1, D_MODEL)),
                  wspec, wspec, wspec, wspec, _const_spec((SUBLANES, C_K)), _const_spec((1, C_W)),
                  _const_spec((C_W, D_MODEL))],
        out_specs=[row_spec, _meta_out_spec()],
        out_shape=[jax.ShapeDtypeStruct(x.shape, F32), jax.ShapeDtypeStruct((BLOCK, D_MODEL), F32)],
        scratch_shapes=[pltpu.VMEM((tb, C_K), F32), pltpu.VMEM((tb, C_K), F32),
                        pltpu.VMEM((tb, C_K), F32), pltpu.VMEM((tb, C_W), BF16),
                        pltpu.VMEM((tb, C_W), F32), pltpu.VMEM((tb, C_W), BF16),
                        pltpu.VMEM((C_HEADS, C_DV, C_DK), F32),
                        pltpu.VMEM((tb, C_K), BF16), pltpu.VMEM((tb, C_K), BF16),
                        pltpu.VMEM((tb, C_K), BF16), pltpu.VMEM((tb, C_W), F32)],
        compiler_params=pltpu.CompilerParams(dimension_semantics=("arbitrary", "arbitrary"),
                                             vmem_limit_bytes=VMEM_LIMIT),
        name="hgrn2_layer",
    )(x, hm, gain, wq, wf, wi, wz, lb, o_gain, w_out)


def _rope_tables(seq):
    half = A_HD // 2
    inv = ROPE_THETA ** (-jnp.arange(half, dtype=F32) / half)
    pos = (jnp.arange(seq + BLOCK) - FRONT_PAD).astype(jnp.int32)
    ang = pos.astype(F32)[:, None] * inv[None, :]
    cos, sin = jnp.cos(ang), jnp.sin(ang)
    return jnp.concatenate([cos, cos], axis=-1), jnp.concatenate([-sin, sin], axis=-1)


def _attn_tables(cos, sin, q_gain, k_gain):
    half = A_HD // 2
    reps = LANES // A_HD
    cols = []
    for g, scale in ((q_gain.astype(F32), A_HD ** -0.5), (k_gain.astype(F32), 1.0)):
        g_rot = jnp.concatenate([g[half:], g[:half]])
        cols += [jnp.tile(cos * (g * scale), (1, reps)), jnp.tile(sin * (g_rot * scale), (1, reps))]
    tab = jnp.concatenate(cols, axis=-1)
    return tab[BLOCK:], tab[:BLOCK]


def _row_tile(seq, want):
    tb = want
    while seq % tb:
        tb //= 2
    assert tb >= BLOCK, "sequence length must be a multiple of the 128-token block"
    return tb


def kernel(x, meta, norm_gain, a_w_in, a_q_gain, a_k_gain, a_sinks, a_w_out, b_w_in, b_conv_w,
           b_conv_b, b_gate_bias, b_h_gain, b_w_out, c_w_in, c_gamma, c_o_gain, c_w_out):
    bsz, seq, _ = x.shape
    depth = norm_gain.shape[0]
    cos, sin = _rope_tables(seq)
    hm = jnp.concatenate([jnp.zeros((FRONT_PAD, D_MODEL), x.dtype), meta.astype(x.dtype)], axis=0)
    p = jax.nn.softmax(c_gamma.astype(F32), axis=0)
    lower_bounds = jnp.cumsum(p, axis=0) - p
    head_cols = jnp.concatenate([jnp.arange(A_HD) + A_HD * hq for hq in _attn_head_order()])

    h = x
    for i in range(depth):
        kind, j = i % 3, i // 3
        gain = norm_gain[i].reshape(1, D_MODEL).astype(F32)
        if kind == 0:
            tab, tabm = _attn_tables(cos, sin, a_q_gain[j], a_k_gain[j])
            w = a_w_in[j]
            w_in = jnp.concatenate([w[:, head_cols], w[:, A_W:A_W + 2 * A_KVW],
                                    w[:, A_W + 2 * A_KVW + head_cols]], axis=1)
            h, hm = _attn_layer(h, hm, tab, tabm, gain, w_in.astype(BF16), a_sinks[j].astype(F32),
                                a_w_out[j][head_cols, :].astype(BF16), _row_tile(seq, 512))
        elif kind == 1:
            w = b_w_in[j]
            o_v, o_i, o_o, o_z = 2 * B_QK, 2 * B_QK + B_W, 2 * B_QK + B_W + 2 * B_HEADS, 2 * B_QK + 2 * B_W + 2 * B_HEADS
            wg = jnp.pad(w[:, o_i:o_o], ((0, 0), (0, LANES - 2 * B_HEADS)))
            gb = b_gate_bias[j].astype(F32)
            h, hm = _mlstm_layer(
                h, hm, gain, w[:, :o_v].astype(BF16), w[:, o_v:o_i].astype(BF16),
                w[:, o_o:o_z].astype(BF16), w[:, o_z:].astype(BF16), wg.astype(BF16),
                b_conv_w[j].astype(F32), b_conv_b[j].reshape(1, 2 * B_QK).astype(F32),
                jnp.broadcast_to(gb[:B_HEADS, None], (B_HEADS, BLOCK)),
                jnp.broadcast_to(gb[B_HEADS:, None], (B_HEADS, BLOCK)),
                b_h_gain[j].reshape(1, B_W).astype(F32), b_w_out[j].astype(BF16), _row_tile(seq, 512))
        else:
            w = c_w_in[j].astype(BF16)
            lb = lower_bounds[i]
            lb_rows = jnp.zeros((SUBLANES, C_K), F32)
            lb_rows = lb_rows.at[0].set(jnp.log(lb)).at[1].set(jnp.log1p(-lb)).at[2].set(1.0 - lb)
            h, hm = _hgrn_layer(
                h, hm, gain, w[:, :C_K], w[:, C_K:2 * C_K], w[:, 2 * C_K:3 * C_K], w[:, 3 * C_K:],
                lb_rows, c_o_gain[j].reshape(1, C_W).astype(F32),
                c_w_out[j].astype(BF16), _row_tile(seq, 512))
    return h
```

```python
import functools

import jax
import jax.numpy as jnp
from jax import lax
from jax.experimental import pallas as pl
from jax.experimental.pallas import tpu as pltpu

F32 = jnp.float32
BF16 = jnp.bfloat16

D_MODEL = 1024
BLOCK = 128
N_META = 16
FRONT_PAD = BLOCK - N_META
EPS = 1e-6
ROPE_THETA = 10000.0
LANES = 128
SUBLANES = 8
MXU_TILE = 256

A_HEADS, A_KV, A_HD = 16, 4, 64
A_GROUP = A_HEADS // A_KV
A_W = A_HEADS * A_HD
A_KVW = A_KV * A_HD
A_IN = 2 * A_W + 2 * A_KVW
A_PIECE = 2 * MXU_TILE
LOG2E = 1.4426950408889634

B_HEADS, B_DK, B_DV, B_CONV = 8, 128, 256, 4
B_QK = B_HEADS * B_DK
B_W = B_HEADS * B_DV
B_HIST = 8

C_HEADS, C_DK, C_DV, C_CHUNK = 8, 128, 128, 64
C_K = C_HEADS * C_DK
C_W = C_HEADS * C_DV
C_IN = 2 * C_K + 2 * C_W
C_PIECE = MXU_TILE
C_LEVELS = (1, 2, 4, 8, 16, 32)
C_SAFE_DECAY = 60.0

VMEM_LIMIT = 56 * 1024 * 1024


def _sigmoid(x):
    return 0.5 + 0.5 * jnp.tanh(0.5 * x)


def _silu(x):
    return x * _sigmoid(x)


def _log_sigmoid(x):
    return jnp.minimum(x, 0.0) - jnp.log(1.0 + jnp.exp(-jnp.abs(x)))


def _rmsnorm(x, g):
    ms = jnp.mean(x * x, axis=-1, keepdims=True)
    return x * lax.rsqrt(ms + EPS) * g


def _dot(a, b):
    return jnp.dot(a, b, preferred_element_type=F32)


def _dot_nt(a, b):
    return lax.dot_general(a, b, (((1,), (1,)), ((), ())), preferred_element_type=F32)


def _dot_tn(a, b):
    return lax.dot_general(a, b, (((0,), (0,)), ((), ())), preferred_element_type=F32)


def _valid_rows(rows, cols):
    return lax.broadcasted_iota(jnp.int32, (rows, cols), 0) >= FRONT_PAD


class _PieceBuffers:
    def __init__(self, refs):
        self.refs = refs
        self.piece = refs[0].shape[1]

    def cols(self, rows, lo, hi):
        k = lo // self.piece
        assert hi <= (k + 1) * self.piece, "a read must stay inside one piece"
        return self.refs[k][0:rows, lo - k * self.piece:hi - k * self.piece]

    def project(self, k, rows, u, w_ref):
        self.refs[k][0:rows, :] = _dot(u, w_ref[:, k * self.piece:(k + 1) * self.piece])


class _PieceRefill:
    def __init__(self, proj, next_piece):
        self.next_piece = next_piece
        self.n_pieces = len(proj.refs) if next_piece is not None else 0
        self.groups = proj.piece // LANES
        self.consumed, self.emitted = set(), set()

    def done(self, lo, hi):
        assert lo % LANES == 0 and hi % LANES == 0
        self.consumed.update(range(lo // LANES, hi // LANES))

    def emit(self, n=1):
        for k in range(self.n_pieces):
            ready = all(k * self.groups + g in self.consumed for g in range(self.groups))
            if n > 0 and ready and k not in self.emitted:
                self.emitted.add(k)
                self.next_piece(k)
                n -= 1

    def flush(self):
        self.emit(self.n_pieces)


def _attn_head_order():
    order = []
    for p in range(A_W // LANES):
        m, g = divmod(p, A_GROUP)
        order += [A_GROUP * (2 * m) + g, A_GROUP * (2 * m + 1) + g]
    return order


def _attn_rows(h, rows, meta, tab, sink_ref, wout_ref, proj, qbuf, klo, khi, vt, obuf, gz, next_piece):
    lane = lax.broadcasted_iota(jnp.int32, (rows, LANES), 1)
    low_head = lane < A_HD
    first_half = (lane % A_HD) < (A_HD // 2)

    def norm_rope(x, ctab, stab):
        sq = x * x
        s_lo = jnp.sum(jnp.where(low_head, sq, 0.0), axis=-1, keepdims=True)
        s_hi = jnp.sum(jnp.where(low_head, 0.0, sq), axis=-1, keepdims=True)
        r = jnp.where(low_head, lax.rsqrt(s_lo / A_HD + EPS), lax.rsqrt(s_hi / A_HD + EPS))
        rot = jnp.where(first_half, pltpu.roll(x, LANES - A_HD // 2, 1), pltpu.roll(x, A_HD // 2, 1))
        return (x * ctab + rot * stab) * r

    refill = _PieceRefill(proj, next_piece)
    z0 = A_W + 2 * A_KVW
    q_cos, q_sin = tab[:, 0:LANES], tab[:, LANES:2 * LANES]
    k_cos, k_sin = tab[:, 2 * LANES:3 * LANES], tab[:, 3 * LANES:4 * LANES]
    for lo in range(0, A_W, A_PIECE):
        for p in range(lo // LANES, (lo + A_PIECE) // LANES):
            x = proj.cols(rows, p * LANES, (p + 1) * LANES)
            qbuf[0:rows, p * LANES:(p + 1) * LANES] = norm_rope(x, q_cos, q_sin).astype(BF16)
        refill.done(lo, lo + A_PIECE)
        refill.emit()
        gz[0:rows, lo:lo + A_PIECE] = _silu(proj.cols(rows, z0 + lo, z0 + lo + A_PIECE))
        refill.done(z0 + lo, z0 + lo + A_PIECE)
    for m in range(A_KVW // LANES):
        k = norm_rope(proj.cols(rows, A_W + m * LANES, A_W + (m + 1) * LANES), k_cos, k_sin)
        klo[BLOCK:BLOCK + rows, m * LANES:(m + 1) * LANES] = jnp.where(low_head, k, 0.0).astype(BF16)
        khi[BLOCK:BLOCK + rows, m * LANES:(m + 1) * LANES] = jnp.where(low_head, 0.0, k).astype(BF16)
    refill.done(A_W, A_W + A_KVW)
    refill.emit()
    for m in range(A_KVW // LANES):
        v = proj.cols(rows, A_W + A_KVW + m * LANES, A_W + A_KVW + (m + 1) * LANES)
        for n in range(rows // BLOCK):
            vt[m * LANES:(m + 1) * LANES, (n + 1) * BLOCK:(n + 2) * BLOCK] = (
                v[n * BLOCK:(n + 1) * BLOCK].T.astype(BF16))
    refill.done(A_W + A_KVW, z0)
    refill.emit()

    c_idx = lax.broadcasted_iota(jnp.int32, (BLOCK, BLOCK), 0)
    t_idx = lax.broadcasted_iota(jnp.int32, (BLOCK, BLOCK), 1)
    own = c_idx <= t_idx
    first_valid = FRONT_PAD if meta else jnp.where(pl.program_id(1) == 0, FRONT_PAD, 0)

    def fold(tile, first):
        prev, cur = tile[0:BLOCK], tile[BLOCK:2 * BLOCK]
        if meta:
            return jnp.where(own & (c_idx >= first_valid), cur, -jnp.inf)
        if first:
            prev = jnp.where(c_idx >= first_valid, prev, -jnp.inf)
        return jnp.where(own, cur, prev)

    def scores(n):
        r0 = n * BLOCK
        out = []
        for m in range(A_KVW // LANES):
            keys = jnp.concatenate([klo[r0:r0 + 2 * BLOCK, m * LANES:(m + 1) * LANES],
                                    khi[r0:r0 + 2 * BLOCK, m * LANES:(m + 1) * LANES]], axis=0)
            qs = jnp.concatenate([qbuf[r0:r0 + BLOCK, (A_GROUP * m + g) * LANES:(A_GROUP * m + g + 1) * LANES]
                                  for g in range(A_GROUP)], axis=0)
            out.append(_dot_nt(keys, qs))
        return out

    n_blocks = rows // BLOCK
    pending = scores(0)
    for n in range(n_blocks):
        r0 = n * BLOCK
        s_now = pending
        if n + 1 < n_blocks:
            pending = scores(n + 1)
        refill.emit()
        for m in range(A_KVW // LANES):
            outs = []
            for half in range(2):
                j = 2 * m + half
                es, inv = [], []
                for g in range(A_GROUP):
                    tile = fold(s_now[m][half * 2 * BLOCK:(half + 1) * 2 * BLOCK, g * BLOCK:(g + 1) * BLOCK], n == 0)
                    sink = sink_ref[A_GROUP * j + g]
                    mx = jnp.maximum(jnp.max(tile, axis=0, keepdims=True), sink)
                    e = jnp.exp2(tile - mx)
                    inv.append(1.0 / (jnp.sum(e, axis=0, keepdims=True) + jnp.exp2(sink - mx)))
                    es.append(jnp.concatenate([jnp.where(own, 0.0, e), jnp.where(own, e, 0.0)],
                                              axis=0).astype(BF16))
                o_t = _dot(vt[j * A_HD:(j + 1) * A_HD, r0:r0 + 2 * BLOCK], jnp.concatenate(es, axis=1))
                outs.append(o_t * jnp.concatenate(inv, axis=1))
            for g in range(A_GROUP):
                pair = jnp.concatenate([outs[0][:, g * BLOCK:(g + 1) * BLOCK],
                                        outs[1][:, g * BLOCK:(g + 1) * BLOCK]], axis=0)
                p = A_GROUP * m + g
                obuf[r0:r0 + BLOCK, p * LANES:(p + 1) * LANES] = pair.T

    refill.flush()
    klo[0:BLOCK, :] = klo[rows:rows + BLOCK, :]
    khi[0:BLOCK, :] = khi[rows:rows + BLOCK, :]
    vt[:, 0:BLOCK] = vt[:, rows:rows + BLOCK]

    y = _dot((obuf[0:rows, :] * gz[0:rows, :]).astype(BF16), wout_ref[...])
    if meta:
        y = jnp.where(_valid_rows(rows, D_MODEL), y, 0.0)
    return h + y


def _attn_kernel(x_ref, xn_ref, hm_ref, tab_ref, tabm_ref, gain_ref, win_ref, sink_ref, wout_ref,
                 y_ref, hm_out_ref, qbuf, klo, khi, vt, obuf, gz, ub, *pieces, tb):
    proj = _PieceBuffers(pieces)
    n_pieces = len(pieces)
    params = (sink_ref, wout_ref, proj, qbuf, klo, khi, vt, obuf, gz)

    @pl.when(pl.program_id(1) == 0)
    def _():
        klo[0:BLOCK, :] = jnp.zeros((BLOCK, A_KVW), BF16)
        khi[0:BLOCK, :] = jnp.zeros((BLOCK, A_KVW), BF16)
        vt[:, 0:BLOCK] = jnp.zeros((A_KVW, BLOCK), BF16)
        hm = hm_ref[...]
        um = _rmsnorm(hm, gain_ref[...]).astype(BF16)
        for k in range(n_pieces):
            proj.project(k, BLOCK, um, win_ref)
        hm_out_ref[...] = _attn_rows(hm, BLOCK, True, tabm_ref[...], *params, None)
        u0 = _rmsnorm(x_ref[0], gain_ref[...]).astype(BF16)
        for k in range(n_pieces):
            proj.project(k, tb, u0, win_ref)

    ub[...] = _rmsnorm(xn_ref[0], gain_ref[...]).astype(BF16)

    def next_piece(k):
        proj.project(k, tb, ub[...], win_ref)

    y_ref[0] = _attn_rows(x_ref[0], tb, False, tab_ref[...], *params, next_piece)


def _const_spec(shape):
    zeros = (0,) * len(shape)
    return pl.BlockSpec(shape, lambda b, i: zeros, pipeline_mode=pl.Buffered(1))


def _meta_out_spec():
    return pl.BlockSpec((BLOCK, D_MODEL), lambda b, i: (0, 0))


def _attn_layer(x, hm, tab, tabm, gain, w_in, sinks, w_out, tb):
    bsz, seq, _ = x.shape
    n_tiles = seq // tb
    row_spec = pl.BlockSpec((1, tb, D_MODEL), lambda b, i: (b, i, 0))
    next_spec = pl.BlockSpec((1, tb, D_MODEL), lambda b, i: (b, jnp.minimum(i + 1, n_tiles - 1), 0))
    return pl.pallas_call(
        functools.partial(_attn_kernel, tb=tb),
        grid=(bsz, n_tiles),
        in_specs=[row_spec, next_spec, _const_spec((BLOCK, D_MODEL)),
                  pl.BlockSpec((tb, 4 * LANES), lambda b, i: (i, 0)), _const_spec((BLOCK, 4 * LANES)),
                  _const_spec((1, D_MODEL)), _const_spec((D_MODEL, A_IN)),
                  pl.BlockSpec(memory_space=pltpu.SMEM), _const_spec((A_W, D_MODEL))],
        out_specs=[row_spec, _meta_out_spec()],
        out_shape=[jax.ShapeDtypeStruct(x.shape, F32), jax.ShapeDtypeStruct((BLOCK, D_MODEL), F32)],
        scratch_shapes=[pltpu.VMEM((tb, A_W), BF16),
                        pltpu.VMEM((tb + BLOCK, A_KVW), BF16), pltpu.VMEM((tb + BLOCK, A_KVW), BF16),
                        pltpu.VMEM((A_KVW, tb + BLOCK), BF16), pltpu.VMEM((tb, A_W), F32),
                        pltpu.VMEM((tb, A_W), F32), pltpu.VMEM((tb, D_MODEL), BF16)]
                       + [pltpu.VMEM((tb, A_PIECE), F32)] * (A_IN // A_PIECE),
        compiler_params=pltpu.CompilerParams(dimension_semantics=("arbitrary", "arbitrary"),
                                             vmem_limit_bytes=VMEM_LIMIT),
        name="swa_layer",
    )(x, x, hm, tab, tabm, gain, w_in, sinks, w_out)


def _mlstm_rows(h, rows, meta, gain_ref, wqk_ref, wv_ref, wo_ref, wz_ref, wg_ref, cw_ref, cb_ref,
                bi_ref, bf_ref, hg_ref, wout_ref, ub, qkraw, qs, ks, vs, hb, gates, gated, cst, mst):
    ub[0:rows, :] = _rmsnorm(h, gain_ref[...]).astype(BF16)
    u = ub[0:rows, :]
    qkraw[B_HIST:B_HIST + rows, :] = _dot(u, wqk_ref[...])
    vs[0:rows, :] = _dot(u, wv_ref[...]).astype(BF16)
    gates[0:rows, :] = _dot(u, wg_ref[...])

    acc = cb_ref[...]
    for j in range(B_CONV):
        start = B_HIST - (B_CONV - 1) + j
        acc = acc + qkraw[start:start + rows, :] * cw_ref[j:j + 1, :]
    qk = _silu(acc)
    qs[0:rows, :] = qk[:, :B_QK].astype(BF16)
    ks[0:rows, :] = (qk[:, B_QK:] * (B_DK ** -0.5)).astype(BF16)
    qkraw[0:B_HIST, :] = qkraw[rows:rows + B_HIST, :]

    t_idx = lax.broadcasted_iota(jnp.int32, (BLOCK, BLOCK), 0)
    s_idx = lax.broadcasted_iota(jnp.int32, (BLOCK, BLOCK), 1)
    causal = s_idx <= t_idx
    lane8 = lax.broadcasted_iota(jnp.int32, (B_HEADS, BLOCK), 1)

    def chunk(c, carry):
        r0 = pl.multiple_of(c * BLOCK, BLOCK)
        g_t = gates[pl.ds(r0, BLOCK), :].T
        li = g_t[0:B_HEADS] + bi_ref[...]
        lf = _log_sigmoid(g_t[B_HEADS:2 * B_HEADS] + bf_ref[...])
        if meta:
            li = jnp.where(lane8 >= FRONT_PAD, li, -jnp.inf)
            lf = jnp.where(lane8 >= FRONT_PAD, lf, 0.0)
        a = lf
        shift = 1
        while shift < BLOCK:
            a = a + jnp.where(lane8 >= shift, pltpu.roll(a, shift, 1), 0.0)
            shift *= 2
        run = li - a
        shift = 1
        while shift < BLOCK:
            run = jnp.maximum(run, jnp.where(lane8 >= shift, pltpu.roll(run, shift, 1), -jnp.inf))
            shift *= 2
        m_prev = mst[:, 0:1]
        decay_max = jnp.maximum(m_prev, run)
        m_t = a + decay_max
        g_tot = a[:, BLOCK - 1:BLOCK]
        uu = g_tot - a + li
        m_new = jnp.maximum(g_tot + m_prev, jnp.max(uu, axis=1, keepdims=True))
        w_old = jnp.exp(g_tot + m_prev - m_new)
        w_s = jnp.exp(uu - m_new)
        b_row = li - a
        mst[...] = jnp.broadcast_to(m_new, (B_HEADS, LANES))

        def col_tile(row):
            return jnp.broadcast_to(row, (BLOCK, BLOCK)).T

        heads = range(B_HEADS)
        ones = jnp.ones((BLOCK, LANES), BF16)
        qh = [qs[pl.ds(r0, BLOCK), hd * B_DK:(hd + 1) * B_DK] for hd in heads]
        kh = [ks[pl.ds(r0, BLOCK), hd * B_DK:(hd + 1) * B_DK] for hd in heads]
        vh = [jnp.concatenate([vs[pl.ds(r0, BLOCK), hd * B_DV:(hd + 1) * B_DV], ones], axis=1) for hd in heads]
        c_prev = [cst[hd] for hd in heads]
        qk = [_dot_nt(qh[hd], kh[hd]) for hd in heads]
        inter_c = [_dot(qh[hd], c_prev[hd].astype(BF16)) for hd in heads]

        sc, kw, w_inter, floor = [], [], [], []
        for hd in heads:
            dm_t = col_tile(decay_max[hd:hd + 1, :])
            w = jnp.exp(jnp.where(causal, b_row[hd:hd + 1, :] - dm_t, -jnp.inf))
            sc.append((qk[hd] * w).astype(BF16))
            kw.append((kh[hd].astype(F32) * col_tile(w_s[hd:hd + 1, :])).astype(BF16))
            w_inter.append(jnp.exp(m_prev[hd:hd + 1, :] - dm_t))
            floor.append(jnp.exp(-col_tile(m_t[hd:hd + 1, :])))

        pv = [_dot(sc[hd], vh[hd]) for hd in heads]
        kv = [_dot_tn(kw[hd], vh[hd]) for hd in heads]
        for hd in heads:
            den = w_inter[hd] * inter_c[hd][:, B_DV:] + pv[hd][:, B_DV:]
            scale = 1.0 / jnp.maximum(jnp.abs(den), floor[hd])
            for half in range(B_DV // LANES):
                ls = slice(half * LANES, (half + 1) * LANES)
                hb[pl.ds(r0, BLOCK), hd * B_DV + half * LANES:hd * B_DV + (half + 1) * LANES] = (
                    (w_inter[hd] * inter_c[hd][:, ls] + pv[hd][:, ls]) * scale)
            cst[hd] = w_old[hd:hd + 1, :] * c_prev[hd] + kv[hd]
        return carry

    lax.fori_loop(0, rows // BLOCK, chunk, 0)

    for hd in range(B_HEADS):
        cs = slice(hd * B_DV, (hd + 1) * B_DV)
        hh = _sigmoid(_dot(u, wo_ref[:, cs])) * hb[0:rows, cs]
        gated[0:rows, cs] = (_rmsnorm(hh, hg_ref[:, cs]) * _silu(_dot(u, wz_ref[:, cs]))).astype(BF16)
    y = _dot(gated[0:rows, :], wout_ref[...])
    if meta:
        y = jnp.where(_valid_rows(rows, D_MODEL), y, 0.0)
    return h + y


def _mlstm_kernel(x_ref, hm_ref, gain_ref, wqk_ref, wv_ref, wo_ref, wz_ref, wg_ref, cw_ref, cb_ref,
                  bi_ref, bf_ref, hg_ref, wout_ref, y_ref, hm_out_ref,
                  ub, qkraw, qs, ks, vs, hb, gates, gated, cst, mst, *, tb):
    params = (gain_ref, wqk_ref, wv_ref, wo_ref, wz_ref, wg_ref, cw_ref, cb_ref, bi_ref, bf_ref,
              hg_ref, wout_ref, ub, qkraw, qs, ks, vs, hb, gates, gated, cst, mst)

    @pl.when(pl.program_id(1) == 0)
    def _():
        qkraw[0:B_HIST, :] = jnp.zeros((B_HIST, 2 * B_QK), F32)
        cst[...] = jnp.zeros(cst.shape, F32)
        mst[...] = jnp.zeros(mst.shape, F32)
        hm_out_ref[...] = _mlstm_rows(hm_ref[...], BLOCK, True, *params)

    y_ref[0] = _mlstm_rows(x_ref[0], tb, False, *params)


def _mlstm_layer(x, hm, gain, wqk, wv, wo, wz, wg, conv_w, conv_b, bias_i, bias_f, h_gain, w_out, tb):
    bsz, seq, _ = x.shape
    row_spec = pl.BlockSpec((1, tb, D_MODEL), lambda b, i: (b, i, 0))
    return pl.pallas_call(
        functools.partial(_mlstm_kernel, tb=tb),
        grid=(bsz, seq // tb),
        in_specs=[row_spec, _const_spec((BLOCK, D_MODEL)), _const_spec((1, D_MODEL)),
                  _const_spec((D_MODEL, 2 * B_QK)), _const_spec((D_MODEL, B_W)),
                  _const_spec((D_MODEL, B_W)), _const_spec((D_MODEL, B_W)),
                  _const_spec((D_MODEL, LANES)), _const_spec((B_CONV, 2 * B_QK)),
                  _const_spec((1, 2 * B_QK)), _const_spec((B_HEADS, BLOCK)),
                  _const_spec((B_HEADS, BLOCK)), _const_spec((1, B_W)), _const_spec((B_W, D_MODEL))],
        out_specs=[row_spec, _meta_out_spec()],
        out_shape=[jax.ShapeDtypeStruct(x.shape, F32), jax.ShapeDtypeStruct((BLOCK, D_MODEL), F32)],
        scratch_shapes=[pltpu.VMEM((tb, D_MODEL), BF16),
                        pltpu.VMEM((tb + B_HIST, 2 * B_QK), F32), pltpu.VMEM((tb, B_QK), BF16),
                        pltpu.VMEM((tb, B_QK), BF16), pltpu.VMEM((tb, B_W), BF16),
                        pltpu.VMEM((tb, B_W), F32),
                        pltpu.VMEM((tb, LANES), F32), pltpu.VMEM((tb, B_W), BF16),
                        pltpu.VMEM((B_HEADS, B_DK, B_DV + LANES), F32), pltpu.VMEM((B_HEADS, LANES), F32)],
        compiler_params=pltpu.CompilerParams(dimension_semantics=("arbitrary", "arbitrary"),
                                             vmem_limit_bytes=VMEM_LIMIT),
        name="mlstm_layer",
    )(x, hm, gain, wqk, wv, wo, wz, wg, conv_w, conv_b, bias_i, bias_f, h_gain, w_out)


def _hgrn_level_masks():
    t_idx = lax.broadcasted_iota(jnp.int32, (C_CHUNK, C_CHUNK), 0)
    s_idx = lax.broadcasted_iota(jnp.int32, (C_CHUNK, C_CHUNK), 1)
    masks = []
    for b in C_LEVELS:
        same = (t_idx // (2 * b)) == (s_idx // (2 * b))
        masks.append(same & ((t_idx % (2 * b)) >= b) & ((s_idx % (2 * b)) < b))
    return t_idx == s_idx, masks


def _hgrn_reference_rows(ab, r0, b):
    sub = lax.broadcasted_iota(jnp.int32, (8, C_K), 0)
    pieces = []
    if 2 * b >= 8:
        for p in range(C_CHUNK // (2 * b)):
            mid = r0 + 2 * b * p + b - 1
            pieces.append(jnp.broadcast_to(ab[pl.ds(mid, 1), :], (2 * b, C_K)))
    else:
        for v in range(C_CHUNK // 8):
            cand = [jnp.broadcast_to(ab[pl.ds(r0 + 8 * v + 2 * b * p + b - 1, 1), :], (8, C_K))
                    for p in range(8 // (2 * b))]
            sel = cand[-1]
            for p in reversed(range(len(cand) - 1)):
                sel = jnp.where(sub < 2 * b * (p + 1), cand[p], sel)
            pieces.append(sel)
    return jnp.concatenate(pieces, axis=0)


def _chunk_cumsum(x):
    rows, cols = x.shape
    x3 = x.reshape(rows // SUBLANES, SUBLANES, cols)
    sub = lax.broadcasted_iota(jnp.int32, x3.shape, 1)
    for s in (1, 2, 4):
        x3 = x3 + jnp.where(sub >= s, pltpu.roll(x3, s, 1), 0.0)
    x = x3.reshape(rows, cols)
    out = []
    for c in range(rows // C_CHUNK):
        run = None
        for g in range(C_CHUNK // SUBLANES):
            r = c * C_CHUNK + g * SUBLANES
            blk = x[r:r + SUBLANES]
            if run is not None:
                blk = blk + run
            out.append(blk)
            run = jnp.broadcast_to(blk[SUBLANES - 1:SUBLANES], (SUBLANES, cols))
    return jnp.concatenate(out, axis=0)


def _hgrn_rows(h, rows, meta, lb_ref, og_ref, wout_ref, proj, qb, kb, ab, vb, zb, gated, st,
               qin, kneg, kout, ob, next_piece):
    refill = _PieceRefill(proj, next_piece)
    PIECE = C_PIECE

    def unpack_gate(lo):
        zb[0:rows, lo:lo + PIECE] = _silu(proj.cols(rows, 3 * C_K + lo, 3 * C_K + lo + PIECE))
        refill.done(3 * C_K + lo, 3 * C_K + lo + PIECE)

    def unpack_value(lo):
        vb[0:rows, lo:lo + PIECE] = proj.cols(rows, 2 * C_K + lo, 2 * C_K + lo + PIECE).astype(BF16)
        refill.done(2 * C_K + lo, 2 * C_K + lo + PIECE)

    def unpack_query(lo):
        qb[0:rows, lo:lo + PIECE] = _silu(proj.cols(rows, lo, lo + PIECE))
        refill.done(lo, lo + PIECE)

    def unpack_forget(lo):
        cs = slice(lo, lo + LANES)
        fpre = proj.cols(rows, C_K + lo, C_K + lo + LANES)
        log_lb, log_1mlb, one_mlb = lb_ref[0:1, cs], lb_ref[1:2, cs], lb_ref[2:3, cs]
        soft = jnp.log(1.0 + jnp.exp(-jnp.abs(fpre)))
        log_sig = jnp.minimum(fpre, 0.0) - soft
        kk = one_mlb * jnp.exp(log_sig - fpre)
        grow = log_1mlb + log_sig
        lf = jnp.maximum(log_lb, grow) + jnp.log(1.0 + jnp.exp(-jnp.abs(log_lb - grow)))
        if meta:
            valid = _valid_rows(rows, LANES)
            lf = jnp.where(valid, lf, 0.0)
            kk = jnp.where(valid, kk, 0.0)
        ab[0:rows, cs] = _chunk_cumsum(lf)
        kb[0:rows, cs] = kk
        refill.done(C_K + lo, C_K + lo + LANES)

    light = ([functools.partial(unpack_gate, lo) for lo in range(0, C_W, PIECE)]
             + [functools.partial(unpack_query, lo) for lo in range(0, C_K, PIECE)]
             + [functools.partial(unpack_value, lo) for lo in range(0, C_W, PIECE)])
    heavy = [functools.partial(unpack_forget, lo) for lo in range(0, C_K, LANES)]
    for i in range(max(len(light), len(heavy))):
        if i < len(light):
            light[i]()
            refill.emit()
        if i < len(heavy):
            heavy[i]()
            refill.emit()
    refill.flush()

    n_chunks = rows // C_CHUNK
    a_end = ab[C_CHUNK - 1:C_CHUNK, :]
    for c in range(1, n_chunks):
        a_end = jnp.minimum(a_end, ab[(c + 1) * C_CHUNK - 1:(c + 1) * C_CHUNK, :])
    mild = jnp.min(a_end) >= -C_SAFE_DECAY

    t_idx = lax.broadcasted_iota(jnp.int32, (C_CHUNK, C_CHUNK), 0)
    s_idx = lax.broadcasted_iota(jnp.int32, (C_CHUNK, C_CHUNK), 1)
    causal = s_idx <= t_idx
    eye, masks = _hgrn_level_masks()

    def chunk(c, carry, factored):
        r0 = pl.multiple_of(c * C_CHUNK, C_CHUNK)
        v = vb[pl.ds(r0, C_CHUNK), :]
        a_last = ab[pl.ds(r0 + C_CHUNK - 1, 1), :]
        decay = jnp.exp(a_last)
        if factored:
            q_in = qin[pl.ds(r0, C_CHUNK), :]
            k_out = kout[pl.ds(r0, C_CHUNK), :]
            q_lv, k_lv, level_masks = [q_in], [kneg[pl.ds(r0, C_CHUNK), :]], [causal]
        else:
            a_c = ab[pl.ds(r0, C_CHUNK), :]
            q = qb[pl.ds(r0, C_CHUNK), :]
            k = kb[pl.ds(r0, C_CHUNK), :]
            q_in = (q * jnp.exp(a_c)).astype(BF16)
            k_out = (k * jnp.exp(a_last - a_c)).astype(BF16)
            q_lv, k_lv, level_masks = [q.astype(BF16)], [k.astype(BF16)], [eye] + masks
            for b in C_LEVELS:
                w = jnp.exp(-jnp.abs(a_c - _hgrn_reference_rows(ab, r0, b)))
                q_lv.append((q * w).astype(BF16))
                k_lv.append((k * w).astype(BF16))
        lanes = [slice(hd * C_DK, (hd + 1) * C_DK) for hd in range(C_HEADS)]
        atts = []
        for ln in lanes:
            att = None
            for ql, kl, mk in zip(q_lv, k_lv, level_masks):
                part = jnp.where(mk, _dot_nt(ql[:, ln], kl[:, ln]), 0.0)
                att = part if att is None else att + part
            atts.append(att.astype(BF16))
        states = [st[hd] for hd in range(C_HEADS)]
        inter = [_dot_nt(q_in[:, ln], s_t.astype(BF16)) for ln, s_t in zip(lanes, states)]
        for hd, ln in enumerate(lanes):
            ob[pl.ds(r0, C_CHUNK), ln] = inter[hd] + _dot(atts[hd], v[:, ln])
        for hd, ln in enumerate(lanes):
            st[hd] = states[hd] * decay[:, ln] + _dot_tn(v[:, ln], k_out[:, ln])
        return carry

    @pl.when(mild)
    def _():
        a = ab[0:rows, :]
        k = kb[0:rows, :]
        a_last = jnp.concatenate(
            [jnp.broadcast_to(ab[(c + 1) * C_CHUNK - 1:(c + 1) * C_CHUNK, :], (C_CHUNK, C_K))
             for c in range(n_chunks)], axis=0)
        qin[0:rows, :] = (qb[0:rows, :] * jnp.exp(a)).astype(BF16)
        kneg[0:rows, :] = (k * jnp.exp(-a)).astype(BF16)
        kout[0:rows, :] = (k * jnp.exp(a_last - a)).astype(BF16)
        lax.fori_loop(0, n_chunks, functools.partial(chunk, factored=True), 0)

    @pl.when(jnp.logical_not(mild))
    def _():
        lax.fori_loop(0, n_chunks, functools.partial(chunk, factored=False), 0)

    for hd in range(C_HEADS):
        ln = slice(hd * C_DK, (hd + 1) * C_DK)
        on = _rmsnorm(ob[0:rows, ln], og_ref[:, ln])
        gated[0:rows, ln] = (on * zb[0:rows, ln]).astype(BF16)
    y = _dot(gated[0:rows, :], wout_ref[...])
    if meta:
        y = jnp.where(_valid_rows(rows, D_MODEL), y, 0.0)
    return h + y


def _hgrn_kernel(x_ref, xn_ref, hm_ref, gain_ref, win_ref, lb_ref, og_ref, wout_ref,
                 y_ref, hm_out_ref, ub, qb, kb, ab, vb, zb, gated, st, qin, kneg, kout, ob, *pieces, tb):
    proj = _PieceBuffers(pieces)
    n_pieces = len(pieces)
    params = (lb_ref, og_ref, wout_ref, proj, qb, kb, ab, vb, zb, gated, st, qin, kneg, kout, ob)

    @pl.when(pl.program_id(1) == 0)
    def _():
        st[...] = jnp.zeros(st.shape, F32)
        hm = hm_ref[...]
        um = _rmsnorm(hm, gain_ref[...]).astype(BF16)
        for k in range(n_pieces):
            proj.project(k, BLOCK, um, win_ref)
        hm_out_ref[...] = _hgrn_rows(hm, BLOCK, True, *params, None)
        u0 = _rmsnorm(x_ref[0], gain_ref[...]).astype(BF16)
        for k in range(n_pieces):
            proj.project(k, tb, u0, win_ref)

    ub[...] = _rmsnorm(xn_ref[0], gain_ref[...]).astype(BF16)

    def next_piece(k):
        proj.project(k, tb, ub[...], win_ref)

    y_ref[0] = _hgrn_rows(x_ref[0], tb, False, *params, next_piece)


def _hgrn_layer(x, hm, gain, w_in, lb, o_gain, w_out, tb):
    bsz, seq, _ = x.shape
    n_tiles = seq // tb
    row_spec = pl.BlockSpec((1, tb, D_MODEL), lambda b, i: (b, i, 0))
    next_spec = pl.BlockSpec((1, tb, D_MODEL), lambda b, i: (b, jnp.minimum(i + 1, n_tiles - 1), 0))
    return pl.pallas_call(
        functools.partial(_hgrn_kernel, tb=tb),
        grid=(bsz, n_tiles),
        in_specs=[row_spec, next_spec, _const_spec((BLOCK, D_MODEL)), _const_spec((1, D_MODEL)),
                  _const_spec((D_MODEL, C_IN)), _const_spec((SUBLANES, C_K)), _const_spec((1, C_W)),
                  _const_spec((C_W, D_MODEL))],
        out_specs=[row_spec, _meta_out_spec()],
        out_shape=[jax.ShapeDtypeStruct(x.shape, F32), jax.ShapeDtypeStruct((BLOCK, D_MODEL), F32)],
        scratch_shapes=[pltpu.VMEM((tb, D_MODEL), BF16),
                        pltpu.VMEM((tb, C_K), F32), pltpu.VMEM((tb, C_K), F32),
                        pltpu.VMEM((tb, C_K), F32), pltpu.VMEM((tb, C_W), BF16),
                        pltpu.VMEM((tb, C_W), F32), pltpu.VMEM((tb, C_W), BF16),
                        pltpu.VMEM((C_HEADS, C_DV, C_DK), F32),
                        pltpu.VMEM((tb, C_K), BF16), pltpu.VMEM((tb, C_K), BF16),
                        pltpu.VMEM((tb, C_K), BF16), pltpu.VMEM((tb, C_W), F32)]
                       + [pltpu.VMEM((tb, C_PIECE), F32)] * (C_IN // C_PIECE),
        compiler_params=pltpu.CompilerParams(dimension_semantics=("arbitrary", "arbitrary"),
                                             vmem_limit_bytes=VMEM_LIMIT),
        name="hgrn2_layer",
    )(x, x, hm, gain, w_in, lb, o_gain, w_out)


def _rope_tables(seq):
    half = A_HD // 2
    inv = ROPE_THETA ** (-jnp.arange(half, dtype=F32) / half)
    pos = (jnp.arange(seq + BLOCK) - FRONT_PAD).astype(jnp.int32)
    ang = pos.astype(F32)[:, None] * inv[None, :]
    cos, sin = jnp.cos(ang), jnp.sin(ang)
    return jnp.concatenate([cos, cos], axis=-1), jnp.concatenate([-sin, sin], axis=-1)


def _attn_tables(cos, sin, q_gain, k_gain):
    half = A_HD // 2
    reps = LANES // A_HD
    cols = []
    for g, scale in ((q_gain.astype(F32), A_HD ** -0.5 * LOG2E), (k_gain.astype(F32), 1.0)):
        g_rot = jnp.concatenate([g[half:], g[:half]])
        cols += [jnp.tile(cos * (g * scale), (1, reps)), jnp.tile(sin * (g_rot * scale), (1, reps))]
    tab = jnp.concatenate(cols, axis=-1)
    return tab[BLOCK:], tab[:BLOCK]


def _row_tile(seq, want):
    tb = want
    while seq % tb:
        tb //= 2
    assert tb >= BLOCK, "sequence length must be a multiple of the 128-token block"
    return tb


def kernel(x, meta, norm_gain, a_w_in, a_q_gain, a_k_gain, a_sinks, a_w_out, b_w_in, b_conv_w,
           b_conv_b, b_gate_bias, b_h_gain, b_w_out, c_w_in, c_gamma, c_o_gain, c_w_out):
    bsz, seq, _ = x.shape
    depth = norm_gain.shape[0]
    cos, sin = _rope_tables(seq)
    hm = jnp.concatenate([jnp.zeros((FRONT_PAD, D_MODEL), x.dtype), meta.astype(x.dtype)], axis=0)
    p = jax.nn.softmax(c_gamma.astype(F32), axis=0)
    lower_bounds = jnp.cumsum(p, axis=0) - p
    head_cols = jnp.concatenate([jnp.arange(A_HD) + A_HD * hq for hq in _attn_head_order()])

    h = x
    for i in range(depth):
        kind, j = i % 3, i // 3
        gain = norm_gain[i].reshape(1, D_MODEL).astype(F32)
        if kind == 0:
            tab, tabm = _attn_tables(cos, sin, a_q_gain[j], a_k_gain[j])
            w = a_w_in[j]
            w_in = jnp.concatenate([w[:, head_cols], w[:, A_W:A_W + 2 * A_KVW],
                                    w[:, A_W + 2 * A_KVW + head_cols]], axis=1)
            h, hm = _attn_layer(h, hm, tab, tabm, gain, w_in.astype(BF16), a_sinks[j].astype(F32) * LOG2E,
                                a_w_out[j][head_cols, :].astype(BF16), _row_tile(seq, 512))
        elif kind == 1:
            w = b_w_in[j]
            o_v, o_i, o_o, o_z = 2 * B_QK, 2 * B_QK + B_W, 2 * B_QK + B_W + 2 * B_HEADS, 2 * B_QK + 2 * B_W + 2 * B_HEADS
            wg = jnp.pad(w[:, o_i:o_o], ((0, 0), (0, LANES - 2 * B_HEADS)))
            gb = b_gate_bias[j].astype(F32)
            h, hm = _mlstm_layer(
                h, hm, gain, w[:, :o_v].astype(BF16), w[:, o_v:o_i].astype(BF16),
                w[:, o_o:o_z].astype(BF16), w[:, o_z:].astype(BF16), wg.astype(BF16),
                b_conv_w[j].astype(F32), b_conv_b[j].reshape(1, 2 * B_QK).astype(F32),
                jnp.broadcast_to(gb[:B_HEADS, None], (B_HEADS, BLOCK)),
                jnp.broadcast_to(gb[B_HEADS:, None], (B_HEADS, BLOCK)),
                b_h_gain[j].reshape(1, B_W).astype(F32), b_w_out[j].astype(BF16), _row_tile(seq, 512))
        else:
            w = c_w_in[j].astype(BF16)
            lb = lower_bounds[i]
            lb_rows = jnp.zeros((SUBLANES, C_K), F32)
            lb_rows = lb_rows.at[0].set(jnp.log(lb)).at[1].set(jnp.log1p(-lb)).at[2].set(1.0 - lb)
            h, hm = _hgrn_layer(
                h, hm, gain, w, lb_rows, c_o_gain[j].reshape(1, C_W).astype(F32),
                c_w_out[j].astype(BF16), _row_tile(seq, 512))
    return h
```

```python
import functools

import jax
import jax.numpy as jnp
from jax import lax
from jax.experimental import pallas as pl
from jax.experimental.pallas import tpu as pltpu

F32 = jnp.float32
BF16 = jnp.bfloat16

D_MODEL = 1024
BLOCK = 128
N_META = 16
FRONT_PAD = BLOCK - N_META
EPS = 1e-6
ROPE_THETA = 10000.0
LANES = 128
SUBLANES = 8
MXU_TILE = 256

A_HEADS, A_KV, A_HD = 16, 4, 64
A_GROUP = A_HEADS // A_KV
A_W = A_HEADS * A_HD
A_KVW = A_KV * A_HD
A_IN = 2 * A_W + 2 * A_KVW
A_PIECE = 2 * MXU_TILE
LOG2E = 1.4426950408889634

B_HEADS, B_DK, B_DV, B_CONV = 8, 128, 256, 4
B_QK = B_HEADS * B_DK
B_W = B_HEADS * B_DV
B_HIST = 8
B_CONV_COLS = MXU_TILE
B_QK_PIECES = 4

C_HEADS, C_DK, C_DV, C_CHUNK = 8, 128, 128, 64
C_K = C_HEADS * C_DK
C_W = C_HEADS * C_DV
C_IN = 2 * C_K + 2 * C_W
C_PIECE = MXU_TILE
C_LEVELS = (1, 2, 4, 8, 16, 32)
C_SAFE_DECAY = 60.0

VMEM_LIMIT = 60 * 1024 * 1024


def _sigmoid(x):
    return 0.5 + 0.5 * jnp.tanh(0.5 * x)


def _silu(x):
    return x * _sigmoid(x)


def _log_sigmoid(x):
    return jnp.minimum(x, 0.0) - jnp.log(1.0 + jnp.exp(-jnp.abs(x)))


def _rmsnorm(x, g):
    ms = jnp.mean(x * x, axis=-1, keepdims=True)
    return x * lax.rsqrt(ms + EPS) * g


def _dot(a, b):
    return jnp.dot(a, b, preferred_element_type=F32)


def _dot_nt(a, b):
    return lax.dot_general(a, b, (((1,), (1,)), ((), ())), preferred_element_type=F32)


def _dot_tn(a, b):
    return lax.dot_general(a, b, (((0,), (0,)), ((), ())), preferred_element_type=F32)


def _valid_rows(rows, cols):
    return lax.broadcasted_iota(jnp.int32, (rows, cols), 0) >= FRONT_PAD


class _PieceBuffers:
    def __init__(self, refs):
        self.refs = refs
        self.piece = refs[0].shape[1]

    def cols(self, rows, lo, hi):
        k = lo // self.piece
        assert hi <= (k + 1) * self.piece, "a read must stay inside one piece"
        return self.refs[k][0:rows, lo - k * self.piece:hi - k * self.piece]

    def project(self, k, rows, u, w_ref):
        self.refs[k][0:rows, :] = _dot(u, w_ref[:, k * self.piece:(k + 1) * self.piece])


class _PieceRefill:
    def __init__(self, proj, next_piece):
        self.next_piece = next_piece
        self.n_pieces = len(proj.refs) if next_piece is not None else 0
        self.groups = proj.piece // LANES
        self.consumed, self.emitted = set(), set()

    def done(self, lo, hi):
        assert lo % LANES == 0 and hi % LANES == 0
        self.consumed.update(range(lo // LANES, hi // LANES))

    def emit(self, n=1):
        for k in range(self.n_pieces):
            ready = all(k * self.groups + g in self.consumed for g in range(self.groups))
            if n > 0 and ready and k not in self.emitted:
                self.emitted.add(k)
                self.next_piece(k)
                n -= 1

    def flush(self):
        self.emit(self.n_pieces)

    def fork(self):
        other = _PieceRefill.__new__(_PieceRefill)
        other.__dict__.update(self.__dict__)
        other.consumed, other.emitted = set(self.consumed), set(self.emitted)
        return other


def _attn_head_order():
    order = []
    for p in range(A_W // LANES):
        m, g = divmod(p, A_GROUP)
        order += [A_GROUP * (2 * m) + g, A_GROUP * (2 * m + 1) + g]
    return order


def _attn_rows(h, rows, meta, tab, sink_ref, wout_ref, proj, qbuf, klo, khi, vt, obuf, gz, next_piece):
    lane = lax.broadcasted_iota(jnp.int32, (rows, LANES), 1)
    low_head = lane < A_HD
    first_half = (lane % A_HD) < (A_HD // 2)

    def norm_rope(x, ctab, stab):
        sq = x * x
        s_lo = jnp.sum(jnp.where(low_head, sq, 0.0), axis=-1, keepdims=True)
        s_hi = jnp.sum(jnp.where(low_head, 0.0, sq), axis=-1, keepdims=True)
        r = jnp.where(low_head, lax.rsqrt(s_lo / A_HD + EPS), lax.rsqrt(s_hi / A_HD + EPS))
        rot = jnp.where(first_half, pltpu.roll(x, LANES - A_HD // 2, 1), pltpu.roll(x, A_HD // 2, 1))
        return (x * ctab + rot * stab) * r

    refill = _PieceRefill(proj, next_piece)
    z0 = A_W + 2 * A_KVW
    q_cos, q_sin = tab[:, 0:LANES], tab[:, LANES:2 * LANES]
    k_cos, k_sin = tab[:, 2 * LANES:3 * LANES], tab[:, 3 * LANES:4 * LANES]
    for lo in range(0, A_W, A_PIECE):
        for p in range(lo // LANES, (lo + A_PIECE) // LANES):
            x = proj.cols(rows, p * LANES, (p + 1) * LANES)
            qbuf[0:rows, p * LANES:(p + 1) * LANES] = norm_rope(x, q_cos, q_sin).astype(BF16)
        refill.done(lo, lo + A_PIECE)
        refill.emit()
        gz[0:rows, lo:lo + A_PIECE] = _silu(proj.cols(rows, z0 + lo, z0 + lo + A_PIECE))
        refill.done(z0 + lo, z0 + lo + A_PIECE)
    for m in range(A_KVW // LANES):
        k = norm_rope(proj.cols(rows, A_W + m * LANES, A_W + (m + 1) * LANES), k_cos, k_sin)
        klo[BLOCK:BLOCK + rows, m * LANES:(m + 1) * LANES] = jnp.where(low_head, k, 0.0).astype(BF16)
        khi[BLOCK:BLOCK + rows, m * LANES:(m + 1) * LANES] = jnp.where(low_head, 0.0, k).astype(BF16)
    refill.done(A_W, A_W + A_KVW)
    refill.emit()
    for m in range(A_KVW // LANES):
        v = proj.cols(rows, A_W + A_KVW + m * LANES, A_W + A_KVW + (m + 1) * LANES)
        for n in range(rows // BLOCK):
            vt[m * LANES:(m + 1) * LANES, (n + 1) * BLOCK:(n + 2) * BLOCK] = (
                v[n * BLOCK:(n + 1) * BLOCK].T.astype(BF16))
    refill.done(A_W + A_KVW, z0)
    refill.emit()

    c_idx = lax.broadcasted_iota(jnp.int32, (BLOCK, BLOCK), 0)
    t_idx = lax.broadcasted_iota(jnp.int32, (BLOCK, BLOCK), 1)
    own = c_idx <= t_idx
    first_valid = FRONT_PAD if meta else jnp.where(pl.program_id(1) == 0, FRONT_PAD, 0)

    def fold(tile, first):
        prev, cur = tile[0:BLOCK], tile[BLOCK:2 * BLOCK]
        if meta:
            return jnp.where(own & (c_idx >= first_valid), cur, -jnp.inf)
        if first:
            prev = jnp.where(c_idx >= first_valid, prev, -jnp.inf)
        return jnp.where(own, cur, prev)

    def scores(n):
        r0 = n * BLOCK
        out = []
        for m in range(A_KVW // LANES):
            keys = jnp.concatenate([klo[r0:r0 + 2 * BLOCK, m * LANES:(m + 1) * LANES],
                                    khi[r0:r0 + 2 * BLOCK, m * LANES:(m + 1) * LANES]], axis=0)
            qs = jnp.concatenate([qbuf[r0:r0 + BLOCK, (A_GROUP * m + g) * LANES:(A_GROUP * m + g + 1) * LANES]
                                  for g in range(A_GROUP)], axis=0)
            out.append(_dot_nt(keys, qs))
        return out

    n_blocks = rows // BLOCK
    pending = scores(0)
    for n in range(n_blocks):
        r0 = n * BLOCK
        s_now = pending
        if n + 1 < n_blocks:
            pending = scores(n + 1)
        refill.emit()
        for m in range(A_KVW // LANES):
            outs = []
            for half in range(2):
                j = 2 * m + half
                es, inv = [], []
                for g in range(A_GROUP):
                    tile = fold(s_now[m][half * 2 * BLOCK:(half + 1) * 2 * BLOCK, g * BLOCK:(g + 1) * BLOCK], n == 0)
                    sink = sink_ref[A_GROUP * j + g]
                    mx = jnp.maximum(jnp.max(tile, axis=0, keepdims=True), sink)
                    e = jnp.exp2(tile - mx)
                    inv.append(1.0 / (jnp.sum(e, axis=0, keepdims=True) + jnp.exp2(sink - mx)))
                    es.append(jnp.concatenate([jnp.where(own, 0.0, e), jnp.where(own, e, 0.0)],
                                              axis=0).astype(BF16))
                o_t = _dot(vt[j * A_HD:(j + 1) * A_HD, r0:r0 + 2 * BLOCK], jnp.concatenate(es, axis=1))
                outs.append(o_t * jnp.concatenate(inv, axis=1))
            for g in range(A_GROUP):
                pair = jnp.concatenate([outs[0][:, g * BLOCK:(g + 1) * BLOCK],
                                        outs[1][:, g * BLOCK:(g + 1) * BLOCK]], axis=0)
                p = A_GROUP * m + g
                obuf[r0:r0 + BLOCK, p * LANES:(p + 1) * LANES] = pair.T

    refill.flush()
    klo[0:BLOCK, :] = klo[rows:rows + BLOCK, :]
    khi[0:BLOCK, :] = khi[rows:rows + BLOCK, :]
    vt[:, 0:BLOCK] = vt[:, rows:rows + BLOCK]

    y = _dot((obuf[0:rows, :] * gz[0:rows, :]).astype(BF16), wout_ref[...])
    if meta:
        y = jnp.where(_valid_rows(rows, D_MODEL), y, 0.0)
    return h + y


def _attn_kernel(x_ref, xn_ref, hm_ref, tab_ref, tabm_ref, gain_ref, win_ref, sink_ref, wout_ref,
                 y_ref, hm_out_ref, qbuf, klo, khi, vt, obuf, gz, ub, *pieces, tb):
    proj = _PieceBuffers(pieces)
    n_pieces = len(pieces)
    params = (sink_ref, wout_ref, proj, qbuf, klo, khi, vt, obuf, gz)

    @pl.when(pl.program_id(1) == 0)
    def _():
        klo[0:BLOCK, :] = jnp.zeros((BLOCK, A_KVW), BF16)
        khi[0:BLOCK, :] = jnp.zeros((BLOCK, A_KVW), BF16)
        vt[:, 0:BLOCK] = jnp.zeros((A_KVW, BLOCK), BF16)
        hm = hm_ref[...]
        um = _rmsnorm(hm, gain_ref[...]).astype(BF16)
        for k in range(n_pieces):
            proj.project(k, BLOCK, um, win_ref)
        hm_out_ref[...] = _attn_rows(hm, BLOCK, True, tabm_ref[...], *params, None)
        u0 = _rmsnorm(x_ref[0], gain_ref[...]).astype(BF16)
        for k in range(n_pieces):
            proj.project(k, tb, u0, win_ref)

    ub[...] = _rmsnorm(xn_ref[0], gain_ref[...]).astype(BF16)

    def next_piece(k):
        proj.project(k, tb, ub[...], win_ref)

    y_ref[0] = _attn_rows(x_ref[0], tb, False, tab_ref[...], *params, next_piece)


def _const_spec(shape):
    zeros = (0,) * len(shape)
    return pl.BlockSpec(shape, lambda b, i: zeros, pipeline_mode=pl.Buffered(1))


def _meta_out_spec():
    return pl.BlockSpec((BLOCK, D_MODEL), lambda b, i: (0, 0))


def _attn_layer(x, hm, tab, tabm, gain, w_in, sinks, w_out, tb):
    bsz, seq, _ = x.shape
    n_tiles = seq // tb
    row_spec = pl.BlockSpec((1, tb, D_MODEL), lambda b, i: (b, i, 0))
    next_spec = pl.BlockSpec((1, tb, D_MODEL), lambda b, i: (b, jnp.minimum(i + 1, n_tiles - 1), 0))
    return pl.pallas_call(
        functools.partial(_attn_kernel, tb=tb),
        grid=(bsz, n_tiles),
        in_specs=[row_spec, next_spec, _const_spec((BLOCK, D_MODEL)),
                  pl.BlockSpec((tb, 4 * LANES), lambda b, i: (i, 0)), _const_spec((BLOCK, 4 * LANES)),
                  _const_spec((1, D_MODEL)), _const_spec((D_MODEL, A_IN)),
                  pl.BlockSpec(memory_space=pltpu.SMEM), _const_spec((A_W, D_MODEL))],
        out_specs=[row_spec, _meta_out_spec()],
        out_shape=[jax.ShapeDtypeStruct(x.shape, F32), jax.ShapeDtypeStruct((BLOCK, D_MODEL), F32)],
        scratch_shapes=[pltpu.VMEM((tb, A_W), BF16),
                        pltpu.VMEM((tb + BLOCK, A_KVW), BF16), pltpu.VMEM((tb + BLOCK, A_KVW), BF16),
                        pltpu.VMEM((A_KVW, tb + BLOCK), BF16), pltpu.VMEM((tb, A_W), F32),
                        pltpu.VMEM((tb, A_W), F32), pltpu.VMEM((tb, D_MODEL), BF16)]
                       + [pltpu.VMEM((tb, A_PIECE), F32)] * (A_IN // A_PIECE),
        compiler_params=pltpu.CompilerParams(dimension_semantics=("arbitrary", "arbitrary"),
                                             vmem_limit_bytes=VMEM_LIMIT),
        name="swa_layer",
    )(x, x, hm, tab, tabm, gain, w_in, sinks, w_out)


def _mlstm_rows(h, rows, meta, u, wv_ref, wo_ref, wz_ref, wg_ref, cw_ref, cb_ref,
                bi_ref, bf_ref, hg_ref, wout_ref, qkraw, qs, ks, vs, hb, gates, gated, cst, mst, next_qk):
    n_blk = 2 * B_QK // B_CONV_COLS
    qk_piece = 2 * B_QK // B_QK_PIECES
    v_piece = B_W // B_QK_PIECES
    after_block = [[] for _ in range(n_blk)]
    for k in range(B_QK_PIECES):
        after_block[k * n_blk // B_QK_PIECES].append(("v", k))
    after_block[n_blk - 1].append(("g", 0))

    for blk in range(n_blk):
        cs = slice(blk * B_CONV_COLS, (blk + 1) * B_CONV_COLS)
        acc = cb_ref[:, cs]
        for j in range(B_CONV):
            start = B_HIST - (B_CONV - 1) + j
            acc = acc + qkraw[start:start + rows, cs] * cw_ref[j:j + 1, cs]
        act = _silu(acc)
        if blk * B_CONV_COLS < B_QK:
            qs[0:rows, cs] = act.astype(BF16)
        else:
            ks[0:rows, blk * B_CONV_COLS - B_QK:(blk + 1) * B_CONV_COLS - B_QK] = (
                act * (B_DK ** -0.5)).astype(BF16)
        qkraw[0:B_HIST, cs] = qkraw[rows:rows + B_HIST, cs]
        for kind, k in after_block[blk]:
            if kind == "v":
                vs[0:rows, k * v_piece:(k + 1) * v_piece] = _dot(
                    u(), wv_ref[:, k * v_piece:(k + 1) * v_piece]).astype(BF16)
            else:
                gates[0:rows, :] = _dot(u(), wg_ref[...])

    t_idx = lax.broadcasted_iota(jnp.int32, (BLOCK, BLOCK), 0)
    s_idx = lax.broadcasted_iota(jnp.int32, (BLOCK, BLOCK), 1)
    causal = s_idx <= t_idx
    lane8 = lax.broadcasted_iota(jnp.int32, (B_HEADS, BLOCK), 1)

    n_chunks = rows // BLOCK

    def chunk(c):
        r0 = c * BLOCK
        g_t = gates[pl.ds(r0, BLOCK), :].T
        li = g_t[0:B_HEADS] + bi_ref[...]
        lf = _log_sigmoid(g_t[B_HEADS:2 * B_HEADS] + bf_ref[...])
        if meta:
            li = jnp.where(lane8 >= FRONT_PAD, li, -jnp.inf)
            lf = jnp.where(lane8 >= FRONT_PAD, lf, 0.0)
        a = lf
        shift = 1
        while shift < BLOCK:
            a = a + jnp.where(lane8 >= shift, pltpu.roll(a, shift, 1), 0.0)
            shift *= 2
        run = li - a
        shift = 1
        while shift < BLOCK:
            run = jnp.maximum(run, jnp.where(lane8 >= shift, pltpu.roll(run, shift, 1), -jnp.inf))
            shift *= 2
        m_prev = mst[:, 0:1]
        decay_max = jnp.maximum(m_prev, run)
        m_t = a + decay_max
        g_tot = a[:, BLOCK - 1:BLOCK]
        uu = g_tot - a + li
        m_new = jnp.maximum(g_tot + m_prev, jnp.max(uu, axis=1, keepdims=True))
        w_old = jnp.exp(g_tot + m_prev - m_new)
        w_s = jnp.exp(uu - m_new)
        b_row = li - a
        mst[...] = jnp.broadcast_to(m_new, (B_HEADS, LANES))

        def col_tile(row):
            return jnp.broadcast_to(row, (BLOCK, BLOCK)).T

        heads = range(B_HEADS)
        ones = jnp.ones((BLOCK, LANES), BF16)
        qh = [qs[pl.ds(r0, BLOCK), hd * B_DK:(hd + 1) * B_DK] for hd in heads]
        kh = [ks[pl.ds(r0, BLOCK), hd * B_DK:(hd + 1) * B_DK] for hd in heads]
        vh = [jnp.concatenate([vs[pl.ds(r0, BLOCK), hd * B_DV:(hd + 1) * B_DV], ones], axis=1) for hd in heads]
        c_prev = [cst[hd] for hd in heads]
        qk = [_dot_nt(qh[hd], kh[hd]) for hd in heads]
        inter_c = [_dot(qh[hd], c_prev[hd].astype(BF16)) for hd in heads]
        if next_qk is not None:
            for k in range(B_QK_PIECES):
                if k * n_chunks // B_QK_PIECES == c:
                    next_qk(k)

        sc, kw, w_inter, floor = [], [], [], []
        for hd in heads:
            dm_t = col_tile(decay_max[hd:hd + 1, :])
            w = jnp.exp(jnp.where(causal, b_row[hd:hd + 1, :] - dm_t, -jnp.inf))
            sc.append((qk[hd] * w).astype(BF16))
            kw.append((kh[hd].astype(F32) * col_tile(w_s[hd:hd + 1, :])).astype(BF16))
            w_inter.append(jnp.exp(m_prev[hd:hd + 1, :] - dm_t))
            floor.append(jnp.exp(-col_tile(m_t[hd:hd + 1, :])))

        pv = [_dot(sc[hd], vh[hd]) for hd in heads]
        kv = [_dot_tn(kw[hd], vh[hd]) for hd in heads]
        for hd in heads:
            den = w_inter[hd] * inter_c[hd][:, B_DV:] + pv[hd][:, B_DV:]
            scale = 1.0 / jnp.maximum(jnp.abs(den), floor[hd])
            for half in range(B_DV // LANES):
                ls = slice(half * LANES, (half + 1) * LANES)
                hb[pl.ds(r0, BLOCK), hd * B_DV + half * LANES:hd * B_DV + (half + 1) * LANES] = (
                    (w_inter[hd] * inter_c[hd][:, ls] + pv[hd][:, ls]) * scale)
            cst[hd] = w_old[hd:hd + 1, :] * c_prev[hd] + kv[hd]

    for c in range(n_chunks):
        chunk(c)

    for hd in range(B_HEADS):
        cs = slice(hd * B_DV, (hd + 1) * B_DV)
        hh = _sigmoid(_dot(u(), wo_ref[:, cs])) * hb[0:rows, cs]
        gated[0:rows, cs] = (_rmsnorm(hh, hg_ref[:, cs]) * _silu(_dot(u(), wz_ref[:, cs]))).astype(BF16)
    y = _dot(gated[0:rows, :], wout_ref[...])
    if meta:
        y = jnp.where(_valid_rows(rows, D_MODEL), y, 0.0)
    return h + y


def _mlstm_kernel(x_ref, xn_ref, hm_ref, gain_ref, wqk_ref, wv_ref, wo_ref, wz_ref, wg_ref, cw_ref, cb_ref,
                  bi_ref, bf_ref, hg_ref, wout_ref, y_ref, hm_out_ref,
                  ub, qkraw, qkg, vs, hb, gates, cst, mst, *, tb):
    qs, ks, gated = qkg.at[:, 0:B_QK], qkg.at[:, B_QK:2 * B_QK], qkg
    params = (wv_ref, wo_ref, wz_ref, wg_ref, cw_ref, cb_ref, bi_ref, bf_ref,
              hg_ref, wout_ref, qkraw, qs, ks, vs, hb, gates, gated, cst, mst)
    slot = pl.program_id(1) % 2

    @pl.when(pl.program_id(1) == 0)
    def _():
        qkraw[0:B_HIST, :] = jnp.zeros((B_HIST, 2 * B_QK), F32)
        cst[...] = jnp.zeros(cst.shape, F32)
        mst[...] = jnp.zeros(mst.shape, F32)
        hm = hm_ref[...]
        ub[1, 0:BLOCK, :] = _rmsnorm(hm, gain_ref[...]).astype(BF16)
        qkraw[B_HIST:B_HIST + BLOCK, :] = _dot(ub[1, 0:BLOCK, :], wqk_ref[...])
        hm_out_ref[...] = _mlstm_rows(hm, BLOCK, True, lambda: ub[1, 0:BLOCK, :], *params, None)
        ub[0] = _rmsnorm(x_ref[0], gain_ref[...]).astype(BF16)
        qkraw[B_HIST:B_HIST + tb, :] = _dot(ub[0], wqk_ref[...])

    ub[1 - slot] = _rmsnorm(xn_ref[0], gain_ref[...]).astype(BF16)
    qk_piece = 2 * B_QK // B_QK_PIECES

    def next_qk(k):
        cols = slice(k * qk_piece, (k + 1) * qk_piece)
        qkraw[B_HIST:B_HIST + tb, cols] = _dot(ub[1 - slot], wqk_ref[:, cols])

    y_ref[0] = _mlstm_rows(x_ref[0], tb, False, lambda: ub[slot], *params, next_qk)


def _mlstm_layer(x, hm, gain, wqk, wv, wo, wz, wg, conv_w, conv_b, bias_i, bias_f, h_gain, w_out, tb):
    bsz, seq, _ = x.shape
    n_tiles = seq // tb
    row_spec = pl.BlockSpec((1, tb, D_MODEL), lambda b, i: (b, i, 0))
    next_spec = pl.BlockSpec((1, tb, D_MODEL), lambda b, i: (b, jnp.minimum(i + 1, n_tiles - 1), 0))
    return pl.pallas_call(
        functools.partial(_mlstm_kernel, tb=tb),
        grid=(bsz, n_tiles),
        in_specs=[row_spec, next_spec, _const_spec((BLOCK, D_MODEL)), _const_spec((1, D_MODEL)),
                  _const_spec((D_MODEL, 2 * B_QK)), _const_spec((D_MODEL, B_W)),
                  _const_spec((D_MODEL, B_W)), _const_spec((D_MODEL, B_W)),
                  _const_spec((D_MODEL, LANES)), _const_spec((B_CONV, 2 * B_QK)),
                  _const_spec((1, 2 * B_QK)), _const_spec((B_HEADS, BLOCK)),
                  _const_spec((B_HEADS, BLOCK)), _const_spec((1, B_W)), _const_spec((B_W, D_MODEL))],
        out_specs=[row_spec, _meta_out_spec()],
        out_shape=[jax.ShapeDtypeStruct(x.shape, F32), jax.ShapeDtypeStruct((BLOCK, D_MODEL), F32)],
        scratch_shapes=[pltpu.VMEM((2, tb, D_MODEL), BF16),
                        pltpu.VMEM((tb + B_HIST, 2 * B_QK), F32), pltpu.VMEM((tb, B_W), BF16),
                        pltpu.VMEM((tb, B_W), BF16), pltpu.VMEM((tb, B_W), F32),
                        pltpu.VMEM((tb, LANES), F32),
                        pltpu.VMEM((B_HEADS, B_DK, B_DV + LANES), F32), pltpu.VMEM((B_HEADS, LANES), F32)],
        compiler_params=pltpu.CompilerParams(dimension_semantics=("arbitrary", "arbitrary"),
                                             vmem_limit_bytes=VMEM_LIMIT),
        name="mlstm_layer",
    )(x, x, hm, gain, wqk, wv, wo, wz, wg, conv_w, conv_b, bias_i, bias_f, h_gain, w_out)


def _hgrn_level_masks():
    t_idx = lax.broadcasted_iota(jnp.int32, (C_CHUNK, C_CHUNK), 0)
    s_idx = lax.broadcasted_iota(jnp.int32, (C_CHUNK, C_CHUNK), 1)
    masks = []
    for b in C_LEVELS:
        same = (t_idx // (2 * b)) == (s_idx // (2 * b))
        masks.append(same & ((t_idx % (2 * b)) >= b) & ((s_idx % (2 * b)) < b))
    return t_idx == s_idx, masks


def _hgrn_reference_rows(ab, r0, b):
    sub = lax.broadcasted_iota(jnp.int32, (8, C_K), 0)
    pieces = []
    if 2 * b >= 8:
        for p in range(C_CHUNK // (2 * b)):
            mid = r0 + 2 * b * p + b - 1
            pieces.append(jnp.broadcast_to(ab[pl.ds(mid, 1), :], (2 * b, C_K)))
    else:
        for v in range(C_CHUNK // 8):
            cand = [jnp.broadcast_to(ab[pl.ds(r0 + 8 * v + 2 * b * p + b - 1, 1), :], (8, C_K))
                    for p in range(8 // (2 * b))]
            sel = cand[-1]
            for p in reversed(range(len(cand) - 1)):
                sel = jnp.where(sub < 2 * b * (p + 1), cand[p], sel)
            pieces.append(sel)
    return jnp.concatenate(pieces, axis=0)


def _chunk_cumsum(x):
    rows, cols = x.shape
    x3 = x.reshape(rows // SUBLANES, SUBLANES, cols)
    sub = lax.broadcasted_iota(jnp.int32, x3.shape, 1)
    for s in (1, 2, 4):
        x3 = x3 + jnp.where(sub >= s, pltpu.roll(x3, s, 1), 0.0)
    x = x3.reshape(rows, cols)
    out = []
    for c in range(rows // C_CHUNK):
        run = None
        for g in range(C_CHUNK // SUBLANES):
            r = c * C_CHUNK + g * SUBLANES
            blk = x[r:r + SUBLANES]
            if run is not None:
                blk = blk + run
            out.append(blk)
            run = jnp.broadcast_to(blk[SUBLANES - 1:SUBLANES], (SUBLANES, cols))
    return jnp.concatenate(out, axis=0)


def _hgrn_rows(h, rows, meta, lb_ref, og_ref, wout_ref, proj, qb, kb, ab, vb, zb, gated, st,
               qin, kneg, kout, ob, next_piece):
    refill = _PieceRefill(proj, next_piece)
    PIECE = C_PIECE

    def unpack_gate(lo):
        zb[0:rows, lo:lo + PIECE] = _silu(proj.cols(rows, 3 * C_K + lo, 3 * C_K + lo + PIECE))
        refill.done(3 * C_K + lo, 3 * C_K + lo + PIECE)

    def unpack_value(lo):
        vb[0:rows, lo:lo + PIECE] = proj.cols(rows, 2 * C_K + lo, 2 * C_K + lo + PIECE).astype(BF16)
        refill.done(2 * C_K + lo, 2 * C_K + lo + PIECE)

    def unpack_query(lo):
        qb[0:rows, lo:lo + PIECE] = _silu(proj.cols(rows, lo, lo + PIECE))
        refill.done(lo, lo + PIECE)

    def unpack_forget(lo):
        cs = slice(lo, lo + LANES)
        fpre = proj.cols(rows, C_K + lo, C_K + lo + LANES)
        log_lb, log_1mlb, one_mlb = lb_ref[0:1, cs], lb_ref[1:2, cs], lb_ref[2:3, cs]
        soft = jnp.log(1.0 + jnp.exp(-jnp.abs(fpre)))
        log_sig = jnp.minimum(fpre, 0.0) - soft
        kk = one_mlb * jnp.exp(log_sig - fpre)
        grow = log_1mlb + log_sig
        lf = jnp.maximum(log_lb, grow) + jnp.log(1.0 + jnp.exp(-jnp.abs(log_lb - grow)))
        if meta:
            valid = _valid_rows(rows, LANES)
            lf = jnp.where(valid, lf, 0.0)
            kk = jnp.where(valid, kk, 0.0)
        ab[0:rows, cs] = _chunk_cumsum(lf)
        kb[0:rows, cs] = kk
        refill.done(C_K + lo, C_K + lo + LANES)

    light = ([functools.partial(unpack_gate, lo) for lo in range(0, C_W, PIECE)]
             + [functools.partial(unpack_query, lo) for lo in range(0, C_K, PIECE)]
             + [functools.partial(unpack_value, lo) for lo in range(0, C_W, PIECE)])
    heavy = [functools.partial(unpack_forget, lo) for lo in range(0, C_K, LANES)]
    for i in range(max(len(light), len(heavy))):
        if i < len(light):
            light[i]()
        if i < len(heavy):
            heavy[i]()
            refill.emit()

    n_chunks = rows // C_CHUNK
    a_end = ab[C_CHUNK - 1:C_CHUNK, :]
    for c in range(1, n_chunks):
        a_end = jnp.minimum(a_end, ab[(c + 1) * C_CHUNK - 1:(c + 1) * C_CHUNK, :])
    mild = jnp.min(a_end) >= -C_SAFE_DECAY

    t_idx = lax.broadcasted_iota(jnp.int32, (C_CHUNK, C_CHUNK), 0)
    s_idx = lax.broadcasted_iota(jnp.int32, (C_CHUNK, C_CHUNK), 1)
    causal = s_idx <= t_idx
    eye, masks = _hgrn_level_masks()

    def chunk(c, carry, factored, emit=None):
        r0 = c * C_CHUNK if isinstance(c, int) else pl.multiple_of(c * C_CHUNK, C_CHUNK)
        v = vb[pl.ds(r0, C_CHUNK), :]
        a_last = ab[pl.ds(r0 + C_CHUNK - 1, 1), :]
        decay = jnp.exp(a_last)
        if factored:
            q_in = qin[pl.ds(r0, C_CHUNK), :]
            k_out = kout[pl.ds(r0, C_CHUNK), :]
            q_lv, k_lv, level_masks = [q_in], [kneg[pl.ds(r0, C_CHUNK), :]], [causal]
        else:
            a_c = ab[pl.ds(r0, C_CHUNK), :]
            q = qb[pl.ds(r0, C_CHUNK), :]
            k = kb[pl.ds(r0, C_CHUNK), :]
            q_in = (q * jnp.exp(a_c)).astype(BF16)
            k_out = (k * jnp.exp(a_last - a_c)).astype(BF16)
            q_lv, k_lv, level_masks = [q.astype(BF16)], [k.astype(BF16)], [eye] + masks
            for b in C_LEVELS:
                w = jnp.exp(-jnp.abs(a_c - _hgrn_reference_rows(ab, r0, b)))
                q_lv.append((q * w).astype(BF16))
                k_lv.append((k * w).astype(BF16))
        lanes = [slice(hd * C_DK, (hd + 1) * C_DK) for hd in range(C_HEADS)]
        atts = []
        for ln in lanes:
            att = None
            for ql, kl, mk in zip(q_lv, k_lv, level_masks):
                part = jnp.where(mk, _dot_nt(ql[:, ln], kl[:, ln]), 0.0)
                att = part if att is None else att + part
            atts.append(att.astype(BF16))
        states = [st[hd] for hd in range(C_HEADS)]
        inter = [_dot_nt(q_in[:, ln], s_t.astype(BF16)) for ln, s_t in zip(lanes, states)]
        if emit is not None:
            emit()
        for hd, ln in enumerate(lanes):
            ob[pl.ds(r0, C_CHUNK), ln] = inter[hd] + _dot(atts[hd], v[:, ln])
        for hd, ln in enumerate(lanes):
            st[hd] = states[hd] * decay[:, ln] + _dot_tn(v[:, ln], k_out[:, ln])
        return carry

    @pl.when(mild)
    def _():
        arm = refill.fork()
        left = arm.n_pieces - len(arm.emitted)
        for lo in range(0, C_K, PIECE):
            cs = slice(lo, lo + PIECE)
            a = ab[0:rows, cs]
            k = kb[0:rows, cs]
            a_last = jnp.concatenate(
                [jnp.broadcast_to(ab[(c + 1) * C_CHUNK - 1:(c + 1) * C_CHUNK, cs], (C_CHUNK, PIECE))
                 for c in range(n_chunks)], axis=0)
            qin[0:rows, cs] = (qb[0:rows, cs] * jnp.exp(a)).astype(BF16)
            kneg[0:rows, cs] = (k * jnp.exp(-a)).astype(BF16)
            kout[0:rows, cs] = (k * jnp.exp(a_last - a)).astype(BF16)
            arm.emit()
        left = arm.n_pieces - len(arm.emitted)
        for c in range(n_chunks):
            share = -(-left * (c + 1) // n_chunks) - (-(-left * c // n_chunks))
            chunk(c, 0, True, functools.partial(arm.emit, share))
        arm.flush()

    @pl.when(jnp.logical_not(mild))
    def _():
        refill.fork().flush()
        lax.fori_loop(0, n_chunks, functools.partial(chunk, factored=False), 0)

    for hd in range(C_HEADS):
        ln = slice(hd * C_DK, (hd + 1) * C_DK)
        on = _rmsnorm(ob[0:rows, ln], og_ref[:, ln])
        gated[0:rows, ln] = (on * zb[0:rows, ln]).astype(BF16)
    y = _dot(gated[0:rows, :], wout_ref[...])
    if meta:
        y = jnp.where(_valid_rows(rows, D_MODEL), y, 0.0)
    return h + y


def _hgrn_kernel(x_ref, xn_ref, hm_ref, gain_ref, win_ref, lb_ref, og_ref, wout_ref,
                 y_ref, hm_out_ref, ub, qb, kb, ab, vb, zb, gated, st, qin, kneg, kout, ob, *pieces, tb):
    proj = _PieceBuffers(pieces)
    n_pieces = len(pieces)
    params = (lb_ref, og_ref, wout_ref, proj, qb, kb, ab, vb, zb, gated, st, qin, kneg, kout, ob)

    @pl.when(pl.program_id(1) == 0)
    def _():
        st[...] = jnp.zeros(st.shape, F32)
        hm = hm_ref[...]
        um = _rmsnorm(hm, gain_ref[...]).astype(BF16)
        for k in range(n_pieces):
            proj.project(k, BLOCK, um, win_ref)
        hm_out_ref[...] = _hgrn_rows(hm, BLOCK, True, *params, None)
        u0 = _rmsnorm(x_ref[0], gain_ref[...]).astype(BF16)
        for k in range(n_pieces):
            proj.project(k, tb, u0, win_ref)

    ub[...] = _rmsnorm(xn_ref[0], gain_ref[...]).astype(BF16)

    def next_piece(k):
        proj.project(k, tb, ub[...], win_ref)

    y_ref[0] = _hgrn_rows(x_ref[0], tb, False, *params, next_piece)


def _hgrn_layer(x, hm, gain, w_in, lb, o_gain, w_out, tb):
    bsz, seq, _ = x.shape
    n_tiles = seq // tb
    row_spec = pl.BlockSpec((1, tb, D_MODEL), lambda b, i: (b, i, 0))
    next_spec = pl.BlockSpec((1, tb, D_MODEL), lambda b, i: (b, jnp.minimum(i + 1, n_tiles - 1), 0))
    return pl.pallas_call(
        functools.partial(_hgrn_kernel, tb=tb),
        grid=(bsz, n_tiles),
        in_specs=[row_spec, next_spec, _const_spec((BLOCK, D_MODEL)), _const_spec((1, D_MODEL)),
                  _const_spec((D_MODEL, C_IN)), _const_spec((SUBLANES, C_K)), _const_spec((1, C_W)),
                  _const_spec((C_W, D_MODEL))],
        out_specs=[row_spec, _meta_out_spec()],
        out_shape=[jax.ShapeDtypeStruct(x.shape, F32), jax.ShapeDtypeStruct((BLOCK, D_MODEL), F32)],
        scratch_shapes=[pltpu.VMEM((tb, D_MODEL), BF16),
                        pltpu.VMEM((tb, C_K), F32), pltpu.VMEM((tb, C_K), F32),
                        pltpu.VMEM((tb, C_K), F32), pltpu.VMEM((tb, C_W), BF16),
                        pltpu.VMEM((tb, C_W), F32), pltpu.VMEM((tb, C_W), BF16),
                        pltpu.VMEM((C_HEADS, C_DV, C_DK), F32),
                        pltpu.VMEM((tb, C_K), BF16), pltpu.VMEM((tb, C_K), BF16),
                        pltpu.VMEM((tb, C_K), BF16), pltpu.VMEM((tb, C_W), F32)]
                       + [pltpu.VMEM((tb, C_PIECE), F32)] * (C_IN // C_PIECE),
        compiler_params=pltpu.CompilerParams(dimension_semantics=("arbitrary", "arbitrary"),
                                             vmem_limit_bytes=VMEM_LIMIT),
        name="hgrn2_layer",
    )(x, x, hm, gain, w_in, lb, o_gain, w_out)


def _rope_tables(seq):
    half = A_HD // 2
    inv = ROPE_THETA ** (-jnp.arange(half, dtype=F32) / half)
    pos = (jnp.arange(seq + BLOCK) - FRONT_PAD).astype(jnp.int32)
    ang = pos.astype(F32)[:, None] * inv[None, :]
    cos, sin = jnp.cos(ang), jnp.sin(ang)
    return jnp.concatenate([cos, cos], axis=-1), jnp.concatenate([-sin, sin], axis=-1)


def _attn_tables(cos, sin, q_gain, k_gain):
    half = A_HD // 2
    reps = LANES // A_HD
    cols = []
    for g, scale in ((q_gain.astype(F32), A_HD ** -0.5 * LOG2E), (k_gain.astype(F32), 1.0)):
        g_rot = jnp.concatenate([g[half:], g[:half]])
        cols += [jnp.tile(cos * (g * scale), (1, reps)), jnp.tile(sin * (g_rot * scale), (1, reps))]
    tab = jnp.concatenate(cols, axis=-1)
    return tab[BLOCK:], tab[:BLOCK]


def _row_tile(seq, want):
    tb = want
    while seq % tb:
        tb //= 2
    assert tb >= BLOCK, "sequence length must be a multiple of the 128-token block"
    return tb


def kernel(x, meta, norm_gain, a_w_in, a_q_gain, a_k_gain, a_sinks, a_w_out, b_w_in, b_conv_w,
           b_conv_b, b_gate_bias, b_h_gain, b_w_out, c_w_in, c_gamma, c_o_gain, c_w_out):
    bsz, seq, _ = x.shape
    depth = norm_gain.shape[0]
    cos, sin = _rope_tables(seq)
    hm = jnp.concatenate([jnp.zeros((FRONT_PAD, D_MODEL), x.dtype), meta.astype(x.dtype)], axis=0)
    p = jax.nn.softmax(c_gamma.astype(F32), axis=0)
    lower_bounds = jnp.cumsum(p, axis=0) - p
    head_cols = jnp.concatenate([jnp.arange(A_HD) + A_HD * hq for hq in _attn_head_order()])

    h = x
    for i in range(depth):
        kind, j = i % 3, i // 3
        gain = norm_gain[i].reshape(1, D_MODEL).astype(F32)
        if kind == 0:
            tab, tabm = _attn_tables(cos, sin, a_q_gain[j], a_k_gain[j])
            w = a_w_in[j]
            w_in = jnp.concatenate([w[:, head_cols], w[:, A_W:A_W + 2 * A_KVW],
                                    w[:, A_W + 2 * A_KVW + head_cols]], axis=1)
            h, hm = _attn_layer(h, hm, tab, tabm, gain, w_in.astype(BF16), a_sinks[j].astype(F32) * LOG2E,
                                a_w_out[j][head_cols, :].astype(BF16), _row_tile(seq, 512))
        elif kind == 1:
            w = b_w_in[j]
            o_v, o_i, o_o, o_z = 2 * B_QK, 2 * B_QK + B_W, 2 * B_QK + B_W + 2 * B_HEADS, 2 * B_QK + 2 * B_W + 2 * B_HEADS
            wg = jnp.pad(w[:, o_i:o_o], ((0, 0), (0, LANES - 2 * B_HEADS)))
            gb = b_gate_bias[j].astype(F32)
            h, hm = _mlstm_layer(
                h, hm, gain, w[:, :o_v].astype(BF16), w[:, o_v:o_i].astype(BF16),
                w[:, o_o:o_z].astype(BF16), w[:, o_z:].astype(BF16), wg.astype(BF16),
                b_conv_w[j].astype(F32), b_conv_b[j].reshape(1, 2 * B_QK).astype(F32),
                jnp.broadcast_to(gb[:B_HEADS, None], (B_HEADS, BLOCK)),
                jnp.broadcast_to(gb[B_HEADS:, None], (B_HEADS, BLOCK)),
                b_h_gain[j].reshape(1, B_W).astype(F32), b_w_out[j].astype(BF16), _row_tile(seq, 512))
        else:
            w = c_w_in[j].astype(BF16)
            lb = lower_bounds[i]
            lb_rows = jnp.zeros((SUBLANES, C_K), F32)
            lb_rows = lb_rows.at[0].set(jnp.log(lb)).at[1].set(jnp.log1p(-lb)).at[2].set(1.0 - lb)
            h, hm = _hgrn_layer(
                h, hm, gain, w, lb_rows, c_o_gain[j].reshape(1, C_W).astype(F32),
                c_w_out[j].astype(BF16), _row_tile(seq, 512))
    return h
```

```python
import functools

import jax
import jax.numpy as jnp
from jax import lax
from jax.experimental import pallas as pl
from jax.experimental.pallas import tpu as pltpu

F32 = jnp.float32
BF16 = jnp.bfloat16

D_MODEL = 1024
BLOCK = 128
N_META = 16
FRONT_PAD = BLOCK - N_META
EPS = 1e-6
ROPE_THETA = 10000.0
LANES = 128
SUBLANES = 8
MXU_TILE = 256

A_HEADS, A_KV, A_HD = 16, 4, 64
A_GROUP = A_HEADS // A_KV
A_W = A_HEADS * A_HD
A_KVW = A_KV * A_HD
A_IN = 2 * A_W + 2 * A_KVW
A_PIECE = 2 * MXU_TILE
LOG2E = 1.4426950408889634

B_HEADS, B_DK, B_DV, B_CONV = 8, 128, 256, 4
B_QK = B_HEADS * B_DK
B_W = B_HEADS * B_DV
B_HIST = 8
B_CONV_COLS = MXU_TILE
B_QK_PIECES = 4

C_HEADS, C_DK, C_DV, C_CHUNK = 8, 128, 128, 64
C_K = C_HEADS * C_DK
C_W = C_HEADS * C_DV
C_IN = 2 * C_K + 2 * C_W
C_PIECE = MXU_TILE
C_LEVELS = (1, 2, 4, 8, 16, 32)
C_SAFE_DECAY = 60.0

VMEM_LIMIT = 60 * 1024 * 1024


def _sigmoid(x):
    return 0.5 + 0.5 * jnp.tanh(0.5 * x)


def _silu(x):
    half = 0.5 * x
    return half + half * jnp.tanh(half)


def _log_sigmoid(x):
    return jnp.minimum(x, 0.0) - jnp.log(1.0 + jnp.exp(-jnp.abs(x)))


def _rmsnorm(x, g):
    ms = jnp.mean(x * x, axis=-1, keepdims=True)
    return x * lax.rsqrt(ms + EPS) * g


def _dot(a, b):
    return jnp.dot(a, b, preferred_element_type=F32)


def _dot_nt(a, b):
    return lax.dot_general(a, b, (((1,), (1,)), ((), ())), preferred_element_type=F32)


def _dot_tn(a, b):
    return lax.dot_general(a, b, (((0,), (0,)), ((), ())), preferred_element_type=F32)


def _valid_rows(rows, cols):
    return lax.broadcasted_iota(jnp.int32, (rows, cols), 0) >= FRONT_PAD


class _PieceBuffers:
    def __init__(self, refs):
        self.refs = refs
        self.piece = refs[0].shape[1]

    def cols(self, rows, lo, hi):
        k = lo // self.piece
        assert hi <= (k + 1) * self.piece, "a read must stay inside one piece"
        return self.refs[k][0:rows, lo - k * self.piece:hi - k * self.piece]

    def project(self, k, rows, u, w_ref):
        self.refs[k][0:rows, :] = _dot(u, w_ref[:, k * self.piece:(k + 1) * self.piece])


class _PieceRefill:
    def __init__(self, proj, next_piece):
        self.next_piece = next_piece
        self.n_pieces = len(proj.refs) if next_piece is not None else 0
        self.groups = proj.piece // LANES
        self.consumed, self.emitted = set(), set()

    def done(self, lo, hi):
        assert lo % LANES == 0 and hi % LANES == 0
        self.consumed.update(range(lo // LANES, hi // LANES))

    def emit(self, n=1):
        for k in range(self.n_pieces):
            ready = all(k * self.groups + g in self.consumed for g in range(self.groups))
            if n > 0 and ready and k not in self.emitted:
                self.emitted.add(k)
                self.next_piece(k)
                n -= 1

    def flush(self):
        self.emit(self.n_pieces)

    def fork(self):
        other = _PieceRefill.__new__(_PieceRefill)
        other.__dict__.update(self.__dict__)
        other.consumed, other.emitted = set(self.consumed), set(self.emitted)
        return other


def _attn_head_order():
    order = []
    for p in range(A_W // LANES):
        m, g = divmod(p, A_GROUP)
        order += [A_GROUP * (2 * m) + g, A_GROUP * (2 * m + 1) + g]
    return order


def _attn_rows(h, rows, meta, tab, gains_ref, sink_ref, wout_ref, proj, qbuf, klo, khi, vt, obuf, gz, next_piece):
    lane = lax.broadcasted_iota(jnp.int32, (rows, LANES), 1)
    low_head = lane < A_HD
    first_half = (lane % A_HD) < (A_HD // 2)

    def norm_rope(x, ctab, stab):
        sq = x * x
        s_lo = jnp.sum(jnp.where(low_head, sq, 0.0), axis=-1, keepdims=True)
        s_hi = jnp.sum(jnp.where(low_head, 0.0, sq), axis=-1, keepdims=True)
        r = jnp.where(low_head, lax.rsqrt(s_lo / A_HD + EPS), lax.rsqrt(s_hi / A_HD + EPS))
        rot = jnp.where(first_half, pltpu.roll(x, LANES - A_HD // 2, 1), pltpu.roll(x, A_HD // 2, 1))
        return (x * ctab + rot * stab) * r

    refill = _PieceRefill(proj, next_piece)
    z0 = A_W + 2 * A_KVW
    cos, sin = tab[:, 0:LANES], tab[:, LANES:2 * LANES]
    q_cos, q_sin = cos * gains_ref[0:1, :], sin * gains_ref[1:2, :]
    k_cos, k_sin = cos * gains_ref[2:3, :], sin * gains_ref[3:4, :]
    for lo in range(0, A_W, A_PIECE):
        for p in range(lo // LANES, (lo + A_PIECE) // LANES):
            x = proj.cols(rows, p * LANES, (p + 1) * LANES)
            qbuf[0:rows, p * LANES:(p + 1) * LANES] = norm_rope(x, q_cos, q_sin).astype(BF16)
        refill.done(lo, lo + A_PIECE)
        refill.emit()
        gz[0:rows, lo:lo + A_PIECE] = _silu(proj.cols(rows, z0 + lo, z0 + lo + A_PIECE))
        refill.done(z0 + lo, z0 + lo + A_PIECE)
    for m in range(A_KVW // LANES):
        k = norm_rope(proj.cols(rows, A_W + m * LANES, A_W + (m + 1) * LANES), k_cos, k_sin)
        klo[BLOCK:BLOCK + rows, m * LANES:(m + 1) * LANES] = jnp.where(low_head, k, 0.0).astype(BF16)
        khi[BLOCK:BLOCK + rows, m * LANES:(m + 1) * LANES] = jnp.where(low_head, 0.0, k).astype(BF16)
    refill.done(A_W, A_W + A_KVW)
    refill.emit()
    for m in range(A_KVW // LANES):
        v = proj.cols(rows, A_W + A_KVW + m * LANES, A_W + A_KVW + (m + 1) * LANES)
        for n in range(rows // BLOCK):
            vt[m * LANES:(m + 1) * LANES, (n + 1) * BLOCK:(n + 2) * BLOCK] = (
                v[n * BLOCK:(n + 1) * BLOCK].T.astype(BF16))
    refill.done(A_W + A_KVW, z0)
    refill.emit()

    c_idx = lax.broadcasted_iota(jnp.int32, (BLOCK, BLOCK), 0)
    t_idx = lax.broadcasted_iota(jnp.int32, (BLOCK, BLOCK), 1)
    own = c_idx <= t_idx
    first_valid = FRONT_PAD if meta else jnp.where(pl.program_id(1) == 0, FRONT_PAD, 0)

    def fold(tile, first):
        prev, cur = tile[0:BLOCK], tile[BLOCK:2 * BLOCK]
        if meta:
            return jnp.where(own & (c_idx >= first_valid), cur, -jnp.inf)
        if first:
            prev = jnp.where(c_idx >= first_valid, prev, -jnp.inf)
        return jnp.where(own, cur, prev)

    def scores(n):
        r0 = n * BLOCK
        out = []
        for m in range(A_KVW // LANES):
            keys = jnp.concatenate([klo[r0:r0 + 2 * BLOCK, m * LANES:(m + 1) * LANES],
                                    khi[r0:r0 + 2 * BLOCK, m * LANES:(m + 1) * LANES]], axis=0)
            qs = jnp.concatenate([qbuf[r0:r0 + BLOCK, (A_GROUP * m + g) * LANES:(A_GROUP * m + g + 1) * LANES]
                                  for g in range(A_GROUP)], axis=0)
            out.append(_dot_nt(keys, qs))
        return out

    n_blocks = rows // BLOCK
    pending = scores(0)
    for n in range(n_blocks):
        r0 = n * BLOCK
        s_now = pending
        if n + 1 < n_blocks:
            pending = scores(n + 1)
        refill.emit()
        for m in range(A_KVW // LANES):
            outs = []
            for half in range(2):
                j = 2 * m + half
                es, inv = [], []
                for g in range(A_GROUP):
                    tile = fold(s_now[m][half * 2 * BLOCK:(half + 1) * 2 * BLOCK, g * BLOCK:(g + 1) * BLOCK], n == 0)
                    sink = sink_ref[A_GROUP * j + g]
                    mx = jnp.maximum(jnp.max(tile, axis=0, keepdims=True), sink)
                    e = jnp.exp2(tile - mx)
                    inv.append(1.0 / (jnp.sum(e, axis=0, keepdims=True) + jnp.exp2(sink - mx)))
                    es.append(jnp.concatenate([jnp.where(own, 0.0, e), jnp.where(own, e, 0.0)],
                                              axis=0).astype(BF16))
                o_t = _dot(vt[j * A_HD:(j + 1) * A_HD, r0:r0 + 2 * BLOCK], jnp.concatenate(es, axis=1))
                outs.append(o_t * jnp.concatenate(inv, axis=1))
            for g in range(A_GROUP):
                pair = jnp.concatenate([outs[0][:, g * BLOCK:(g + 1) * BLOCK],
                                        outs[1][:, g * BLOCK:(g + 1) * BLOCK]], axis=0)
                p = A_GROUP * m + g
                obuf[r0:r0 + BLOCK, p * LANES:(p + 1) * LANES] = pair.T

    refill.flush()
    klo[0:BLOCK, :] = klo[rows:rows + BLOCK, :]
    khi[0:BLOCK, :] = khi[rows:rows + BLOCK, :]
    vt[:, 0:BLOCK] = vt[:, rows:rows + BLOCK]

    y = _dot((obuf[0:rows, :] * gz[0:rows, :]).astype(BF16), wout_ref[...])
    if meta:
        y = jnp.where(_valid_rows(rows, D_MODEL), y, 0.0)
    return h + y


def _attn_kernel(x_ref, xn_ref, hm_ref, tab_ref, tabm_ref, gains_ref, gain_ref, win_ref, sink_ref, wout_ref,
                 y_ref, hm_out_ref, qbuf, klo, khi, vt, obuf, gz, ub, *pieces, tb):
    proj = _PieceBuffers(pieces)
    n_pieces = len(pieces)
    params = (gains_ref, sink_ref, wout_ref, proj, qbuf, klo, khi, vt, obuf, gz)

    @pl.when(pl.program_id(1) == 0)
    def _():
        klo[0:BLOCK, :] = jnp.zeros((BLOCK, A_KVW), BF16)
        khi[0:BLOCK, :] = jnp.zeros((BLOCK, A_KVW), BF16)
        vt[:, 0:BLOCK] = jnp.zeros((A_KVW, BLOCK), BF16)
        hm = hm_ref[...]
        um = _rmsnorm(hm, gain_ref[...]).astype(BF16)
        for k in range(n_pieces):
            proj.project(k, BLOCK, um, win_ref)
        hm_out_ref[...] = _attn_rows(hm, BLOCK, True, tabm_ref[...], *params, None)
        u0 = _rmsnorm(x_ref[0], gain_ref[...]).astype(BF16)
        for k in range(n_pieces):
            proj.project(k, tb, u0, win_ref)

    ub[...] = _rmsnorm(xn_ref[0], gain_ref[...]).astype(BF16)

    def next_piece(k):
        proj.project(k, tb, ub[...], win_ref)

    y_ref[0] = _attn_rows(x_ref[0], tb, False, tab_ref[...], *params, next_piece)


def _const_spec(shape):
    zeros = (0,) * len(shape)
    return pl.BlockSpec(shape, lambda b, i: zeros, pipeline_mode=pl.Buffered(1))


def _meta_out_spec():
    return pl.BlockSpec((BLOCK, D_MODEL), lambda b, i: (0, 0))


def _attn_layer(x, hm, tab, tabm, gains, gain, w_in, sinks, w_out, tb):
    bsz, seq, _ = x.shape
    n_tiles = seq // tb
    row_spec = pl.BlockSpec((1, tb, D_MODEL), lambda b, i: (b, i, 0))
    next_spec = pl.BlockSpec((1, tb, D_MODEL), lambda b, i: (b, jnp.minimum(i + 1, n_tiles - 1), 0))
    return pl.pallas_call(
        functools.partial(_attn_kernel, tb=tb),
        grid=(bsz, n_tiles),
        in_specs=[row_spec, next_spec, _const_spec((BLOCK, D_MODEL)),
                  pl.BlockSpec((tb, 2 * LANES), lambda b, i: (i, 0)), _const_spec((BLOCK, 2 * LANES)),
                  _const_spec((SUBLANES, LANES)), _const_spec((1, D_MODEL)), _const_spec((D_MODEL, A_IN)),
                  pl.BlockSpec(memory_space=pltpu.SMEM), _const_spec((A_W, D_MODEL))],
        out_specs=[row_spec, _meta_out_spec()],
        out_shape=[jax.ShapeDtypeStruct(x.shape, F32), jax.ShapeDtypeStruct((BLOCK, D_MODEL), F32)],
        scratch_shapes=[pltpu.VMEM((tb, A_W), BF16),
                        pltpu.VMEM((tb + BLOCK, A_KVW), BF16), pltpu.VMEM((tb + BLOCK, A_KVW), BF16),
                        pltpu.VMEM((A_KVW, tb + BLOCK), BF16), pltpu.VMEM((tb, A_W), F32),
                        pltpu.VMEM((tb, A_W), F32), pltpu.VMEM((tb, D_MODEL), BF16)]
                       + [pltpu.VMEM((tb, A_PIECE), F32)] * (A_IN // A_PIECE),
        compiler_params=pltpu.CompilerParams(dimension_semantics=("arbitrary", "arbitrary"),
                                             vmem_limit_bytes=VMEM_LIMIT),
        name="swa_layer",
    )(x, x, hm, tab, tabm, gains, gain, w_in, sinks, w_out)


def _mlstm_rows(h, rows, meta, u, wv_ref, wo_ref, wz_ref, wg_ref, cw_ref, cb_ref,
                bi_ref, bf_ref, hg_ref, wout_ref, qkraw, qs, ks, vs, hb, gates, gated, cst, mst, next_qk):
    n_blk = 2 * B_QK // B_CONV_COLS
    qk_piece = 2 * B_QK // B_QK_PIECES
    v_piece = B_W // B_QK_PIECES
    after_block = [[] for _ in range(n_blk)]
    for k in range(B_QK_PIECES):
        after_block[k * n_blk // B_QK_PIECES].append(("v", k))
    after_block[n_blk - 1].append(("g", 0))

    for blk in range(n_blk):
        cs = slice(blk * B_CONV_COLS, (blk + 1) * B_CONV_COLS)
        acc = cb_ref[:, cs]
        for j in range(B_CONV):
            start = B_HIST - (B_CONV - 1) + j
            acc = acc + qkraw[start:start + rows, cs] * cw_ref[j:j + 1, cs]
        act = _silu(acc)
        if blk * B_CONV_COLS < B_QK:
            qs[0:rows, cs] = act.astype(BF16)
        else:
            ks[0:rows, blk * B_CONV_COLS - B_QK:(blk + 1) * B_CONV_COLS - B_QK] = (
                act * (B_DK ** -0.5)).astype(BF16)
        qkraw[0:B_HIST, cs] = qkraw[rows:rows + B_HIST, cs]
        for kind, k in after_block[blk]:
            if kind == "v":
                vs[0:rows, k * v_piece:(k + 1) * v_piece] = _dot(
                    u(), wv_ref[:, k * v_piece:(k + 1) * v_piece]).astype(BF16)
            else:
                gates[0:rows, :] = _dot(u(), wg_ref[...])

    t_idx = lax.broadcasted_iota(jnp.int32, (BLOCK, BLOCK), 0)
    s_idx = lax.broadcasted_iota(jnp.int32, (BLOCK, BLOCK), 1)
    causal = s_idx <= t_idx
    lane8 = lax.broadcasted_iota(jnp.int32, (B_HEADS, BLOCK), 1)

    n_chunks = rows // BLOCK

    def chunk(c):
        r0 = c * BLOCK
        g_t = gates[pl.ds(r0, BLOCK), :].T
        li = g_t[0:B_HEADS] + bi_ref[...]
        lf = _log_sigmoid(g_t[B_HEADS:2 * B_HEADS] + bf_ref[...])
        if meta:
            li = jnp.where(lane8 >= FRONT_PAD, li, -jnp.inf)
            lf = jnp.where(lane8 >= FRONT_PAD, lf, 0.0)
        a = lf
        shift = 1
        while shift < BLOCK:
            a = a + jnp.where(lane8 >= shift, pltpu.roll(a, shift, 1), 0.0)
            shift *= 2
        run = li - a
        shift = 1
        while shift < BLOCK:
            run = jnp.maximum(run, jnp.where(lane8 >= shift, pltpu.roll(run, shift, 1), -jnp.inf))
            shift *= 2
        m_prev = mst[:, 0:1]
        decay_max = jnp.maximum(m_prev, run)
        m_t = a + decay_max
        g_tot = a[:, BLOCK - 1:BLOCK]
        uu = g_tot - a + li
        m_new = jnp.maximum(g_tot + m_prev, jnp.max(uu, axis=1, keepdims=True))
        w_old = jnp.exp(g_tot + m_prev - m_new)
        w_s = jnp.exp(uu - m_new)
        b_row = li - a
        mst[...] = jnp.broadcast_to(m_new, (B_HEADS, LANES))

        def col_tile(row):
            return jnp.broadcast_to(row, (BLOCK, BLOCK)).T

        heads = range(B_HEADS)
        ones = jnp.ones((BLOCK, LANES), BF16)
        qh = [qs[pl.ds(r0, BLOCK), hd * B_DK:(hd + 1) * B_DK] for hd in heads]
        kh = [ks[pl.ds(r0, BLOCK), hd * B_DK:(hd + 1) * B_DK] for hd in heads]
        vh = [jnp.concatenate([vs[pl.ds(r0, BLOCK), hd * B_DV:(hd + 1) * B_DV], ones], axis=1) for hd in heads]
        c_prev = [cst[hd] for hd in heads]
        qk = [_dot_nt(qh[hd], kh[hd]) for hd in heads]
        inter_c = [_dot(qh[hd], c_prev[hd].astype(BF16)) for hd in heads]
        if next_qk is not None:
            for k in range(B_QK_PIECES):
                if k * n_chunks // B_QK_PIECES == c:
                    next_qk(k)

        sc, kw, w_inter, floor = [], [], [], []
        for hd in heads:
            dm_t = col_tile(decay_max[hd:hd + 1, :])
            w = jnp.exp(jnp.where(causal, b_row[hd:hd + 1, :] - dm_t, -jnp.inf))
            sc.append((qk[hd] * w).astype(BF16))
            kw.append((kh[hd].astype(F32) * col_tile(w_s[hd:hd + 1, :])).astype(BF16))
            w_inter.append(jnp.exp(m_prev[hd:hd + 1, :] - dm_t))
            floor.append(jnp.exp(-col_tile(m_t[hd:hd + 1, :])))

        pv = [_dot(sc[hd], vh[hd]) for hd in heads]
        kv = [_dot_tn(kw[hd], vh[hd]) for hd in heads]
        for hd in heads:
            den = w_inter[hd] * inter_c[hd][:, B_DV:] + pv[hd][:, B_DV:]
            scale = 1.0 / jnp.maximum(jnp.abs(den), floor[hd])
            for half in range(B_DV // LANES):
                ls = slice(half * LANES, (half + 1) * LANES)
                hb[pl.ds(r0, BLOCK), hd * B_DV + half * LANES:hd * B_DV + (half + 1) * LANES] = (
                    (w_inter[hd] * inter_c[hd][:, ls] + pv[hd][:, ls]) * scale)
            cst[hd] = w_old[hd:hd + 1, :] * c_prev[hd] + kv[hd]

    for c in range(n_chunks):
        chunk(c)

    for hd in range(B_HEADS):
        cs = slice(hd * B_DV, (hd + 1) * B_DV)
        hh = _sigmoid(_dot(u(), wo_ref[:, cs])) * hb[0:rows, cs]
        gated[0:rows, cs] = (_rmsnorm(hh, hg_ref[:, cs]) * _silu(_dot(u(), wz_ref[:, cs]))).astype(BF16)
    y = _dot(gated[0:rows, :], wout_ref[...])
    if meta:
        y = jnp.where(_valid_rows(rows, D_MODEL), y, 0.0)
    return h + y


def _mlstm_kernel(x_ref, xn_ref, hm_ref, gain_ref, wqk_ref, wv_ref, wo_ref, wz_ref, wg_ref, cw_ref, cb_ref,
                  bi_ref, bf_ref, hg_ref, wout_ref, y_ref, hm_out_ref,
                  ub, qkraw, qkg, vs, hb, gates, cst, mst, *, tb):
    qs, ks, gated = qkg.at[:, 0:B_QK], qkg.at[:, B_QK:2 * B_QK], qkg
    params = (wv_ref, wo_ref, wz_ref, wg_ref, cw_ref, cb_ref, bi_ref, bf_ref,
              hg_ref, wout_ref, qkraw, qs, ks, vs, hb, gates, gated, cst, mst)
    slot = pl.program_id(1) % 2

    @pl.when(pl.program_id(1) == 0)
    def _():
        qkraw[0:B_HIST, :] = jnp.zeros((B_HIST, 2 * B_QK), F32)
        cst[...] = jnp.zeros(cst.shape, F32)
        mst[...] = jnp.zeros(mst.shape, F32)
        hm = hm_ref[...]
        ub[1, 0:BLOCK, :] = _rmsnorm(hm, gain_ref[...]).astype(BF16)
        qkraw[B_HIST:B_HIST + BLOCK, :] = _dot(ub[1, 0:BLOCK, :], wqk_ref[...])
        hm_out_ref[...] = _mlstm_rows(hm, BLOCK, True, lambda: ub[1, 0:BLOCK, :], *params, None)
        ub[0] = _rmsnorm(x_ref[0], gain_ref[...]).astype(BF16)
        qkraw[B_HIST:B_HIST + tb, :] = _dot(ub[0], wqk_ref[...])

    ub[1 - slot] = _rmsnorm(xn_ref[0], gain_ref[...]).astype(BF16)
    qk_piece = 2 * B_QK // B_QK_PIECES

    def next_qk(k):
        cols = slice(k * qk_piece, (k + 1) * qk_piece)
        qkraw[B_HIST:B_HIST + tb, cols] = _dot(ub[1 - slot], wqk_ref[:, cols])

    y_ref[0] = _mlstm_rows(x_ref[0], tb, False, lambda: ub[slot], *params, next_qk)


def _mlstm_layer(x, hm, gain, wqk, wv, wo, wz, wg, conv_w, conv_b, bias_i, bias_f, h_gain, w_out, tb):
    bsz, seq, _ = x.shape
    n_tiles = seq // tb
    row_spec = pl.BlockSpec((1, tb, D_MODEL), lambda b, i: (b, i, 0))
    next_spec = pl.BlockSpec((1, tb, D_MODEL), lambda b, i: (b, jnp.minimum(i + 1, n_tiles - 1), 0))
    return pl.pallas_call(
        functools.partial(_mlstm_kernel, tb=tb),
        grid=(bsz, n_tiles),
        in_specs=[row_spec, next_spec, _const_spec((BLOCK, D_MODEL)), _const_spec((1, D_MODEL)),
                  _const_spec((D_MODEL, 2 * B_QK)), _const_spec((D_MODEL, B_W)),
                  _const_spec((D_MODEL, B_W)), _const_spec((D_MODEL, B_W)),
                  _const_spec((D_MODEL, LANES)), _const_spec((B_CONV, 2 * B_QK)),
                  _const_spec((1, 2 * B_QK)), _const_spec((B_HEADS, BLOCK)),
                  _const_spec((B_HEADS, BLOCK)), _const_spec((1, B_W)), _const_spec((B_W, D_MODEL))],
        out_specs=[row_spec, _meta_out_spec()],
        out_shape=[jax.ShapeDtypeStruct(x.shape, F32), jax.ShapeDtypeStruct((BLOCK, D_MODEL), F32)],
        scratch_shapes=[pltpu.VMEM((2, tb, D_MODEL), BF16),
                        pltpu.VMEM((tb + B_HIST, 2 * B_QK), F32), pltpu.VMEM((tb, B_W), BF16),
                        pltpu.VMEM((tb, B_W), BF16), pltpu.VMEM((tb, B_W), F32),
                        pltpu.VMEM((tb, LANES), F32),
                        pltpu.VMEM((B_HEADS, B_DK, B_DV + LANES), F32), pltpu.VMEM((B_HEADS, LANES), F32)],
        compiler_params=pltpu.CompilerParams(dimension_semantics=("arbitrary", "arbitrary"),
                                             vmem_limit_bytes=VMEM_LIMIT),
        name="mlstm_layer",
    )(x, x, hm, gain, wqk, wv, wo, wz, wg, conv_w, conv_b, bias_i, bias_f, h_gain, w_out)


def _hgrn_level_masks():
    t_idx = lax.broadcasted_iota(jnp.int32, (C_CHUNK, C_CHUNK), 0)
    s_idx = lax.broadcasted_iota(jnp.int32, (C_CHUNK, C_CHUNK), 1)
    masks = []
    for b in C_LEVELS:
        same = (t_idx // (2 * b)) == (s_idx // (2 * b))
        masks.append(same & ((t_idx % (2 * b)) >= b) & ((s_idx % (2 * b)) < b))
    return t_idx == s_idx, masks


def _hgrn_reference_rows(ab, r0, b):
    sub = lax.broadcasted_iota(jnp.int32, (8, C_K), 0)
    pieces = []
    if 2 * b >= 8:
        for p in range(C_CHUNK // (2 * b)):
            mid = r0 + 2 * b * p + b - 1
            pieces.append(jnp.broadcast_to(ab[pl.ds(mid, 1), :], (2 * b, C_K)))
    else:
        for v in range(C_CHUNK // 8):
            cand = [jnp.broadcast_to(ab[pl.ds(r0 + 8 * v + 2 * b * p + b - 1, 1), :], (8, C_K))
                    for p in range(8 // (2 * b))]
            sel = cand[-1]
            for p in reversed(range(len(cand) - 1)):
                sel = jnp.where(sub < 2 * b * (p + 1), cand[p], sel)
            pieces.append(sel)
    return jnp.concatenate(pieces, axis=0)


def _chunk_cumsum(x):
    rows, cols = x.shape
    x3 = x.reshape(rows // SUBLANES, SUBLANES, cols)
    sub = lax.broadcasted_iota(jnp.int32, x3.shape, 1)
    for s in (1, 2, 4):
        x3 = x3 + jnp.where(sub >= s, pltpu.roll(x3, s, 1), 0.0)
    x = x3.reshape(rows, cols)
    out = []
    for c in range(rows // C_CHUNK):
        run = None
        for g in range(C_CHUNK // SUBLANES):
            r = c * C_CHUNK + g * SUBLANES
            blk = x[r:r + SUBLANES]
            if run is not None:
                blk = blk + run
            out.append(blk)
            run = jnp.broadcast_to(blk[SUBLANES - 1:SUBLANES], (SUBLANES, cols))
    return jnp.concatenate(out, axis=0)


def _hgrn_rows(h, rows, meta, lb_ref, og_ref, wout_ref, proj, qb, kb, ab, vb, zb, gated, st,
               qin, kneg, kout, ob, next_piece):
    refill = _PieceRefill(proj, next_piece)
    PIECE = C_PIECE

    def unpack_gate(lo):
        zb[0:rows, lo:lo + PIECE] = _silu(proj.cols(rows, 3 * C_K + lo, 3 * C_K + lo + PIECE))
        refill.done(3 * C_K + lo, 3 * C_K + lo + PIECE)

    def unpack_value(lo):
        vb[0:rows, lo:lo + PIECE] = proj.cols(rows, 2 * C_K + lo, 2 * C_K + lo + PIECE).astype(BF16)
        refill.done(2 * C_K + lo, 2 * C_K + lo + PIECE)

    def unpack_query(lo):
        qb[0:rows, lo:lo + PIECE] = _silu(proj.cols(rows, lo, lo + PIECE))
        refill.done(lo, lo + PIECE)

    def unpack_forget(lo):
        cs = slice(lo, lo + LANES)
        fpre = proj.cols(rows, C_K + lo, C_K + lo + LANES)
        log_lb, log_1mlb, one_mlb = lb_ref[0:1, cs], lb_ref[1:2, cs], lb_ref[2:3, cs]
        soft = jnp.log(1.0 + jnp.exp(-jnp.abs(fpre)))
        log_sig = jnp.minimum(fpre, 0.0) - soft
        kk = one_mlb * jnp.exp(log_sig - fpre)
        grow = log_1mlb + log_sig
        lf = jnp.maximum(log_lb, grow) + jnp.log(1.0 + jnp.exp(-jnp.abs(log_lb - grow)))
        if meta:
            valid = _valid_rows(rows, LANES)
            lf = jnp.where(valid, lf, 0.0)
            kk = jnp.where(valid, kk, 0.0)
        ab[0:rows, cs] = _chunk_cumsum(lf)
        kb[0:rows, cs] = kk
        refill.done(C_K + lo, C_K + lo + LANES)

    light = ([functools.partial(unpack_gate, lo) for lo in range(0, C_W, PIECE)]
             + [functools.partial(unpack_query, lo) for lo in range(0, C_K, PIECE)]
             + [functools.partial(unpack_value, lo) for lo in range(0, C_W, PIECE)])
    heavy = [functools.partial(unpack_forget, lo) for lo in range(0, C_K, LANES)]
    for i in range(max(len(light), len(heavy))):
        if i < len(light):
            light[i]()
            if i % 2 == 0 and i < len(heavy):
                refill.emit()
        if i < len(heavy):
            heavy[i]()
            refill.emit()

    n_chunks = rows // C_CHUNK
    a_end = ab[C_CHUNK - 1:C_CHUNK, :]
    for c in range(1, n_chunks):
        a_end = jnp.minimum(a_end, ab[(c + 1) * C_CHUNK - 1:(c + 1) * C_CHUNK, :])
    mild = jnp.min(a_end) >= -C_SAFE_DECAY

    t_idx = lax.broadcasted_iota(jnp.int32, (C_CHUNK, C_CHUNK), 0)
    s_idx = lax.broadcasted_iota(jnp.int32, (C_CHUNK, C_CHUNK), 1)
    causal = s_idx <= t_idx
    eye, masks = _hgrn_level_masks()

    def chunk(c, carry, factored, emit=None):
        r0 = c * C_CHUNK if isinstance(c, int) else pl.multiple_of(c * C_CHUNK, C_CHUNK)
        v = vb[pl.ds(r0, C_CHUNK), :]
        a_last = ab[pl.ds(r0 + C_CHUNK - 1, 1), :]
        decay = jnp.exp(a_last)
        if factored:
            q_in = qin[pl.ds(r0, C_CHUNK), :]
            k_out = kout[pl.ds(r0, C_CHUNK), :]
            q_lv, k_lv, level_masks = [q_in], [kneg[pl.ds(r0, C_CHUNK), :]], [causal]
        else:
            a_c = ab[pl.ds(r0, C_CHUNK), :]
            q = qb[pl.ds(r0, C_CHUNK), :]
            k = kb[pl.ds(r0, C_CHUNK), :]
            q_in = (q * jnp.exp(a_c)).astype(BF16)
            k_out = (k * jnp.exp(a_last - a_c)).astype(BF16)
            q_lv, k_lv, level_masks = [q.astype(BF16)], [k.astype(BF16)], [eye] + masks
            for b in C_LEVELS:
                w = jnp.exp(-jnp.abs(a_c - _hgrn_reference_rows(ab, r0, b)))
                q_lv.append((q * w).astype(BF16))
                k_lv.append((k * w).astype(BF16))
        lanes = [slice(hd * C_DK, (hd + 1) * C_DK) for hd in range(C_HEADS)]
        atts = []
        for ln in lanes:
            att = None
            for ql, kl, mk in zip(q_lv, k_lv, level_masks):
                part = jnp.where(mk, _dot_nt(ql[:, ln], kl[:, ln]), 0.0)
                att = part if att is None else att + part
            atts.append(att.astype(BF16))
        states = [st[hd] for hd in range(C_HEADS)]
        inter = [_dot_nt(q_in[:, ln], s_t.astype(BF16)) for ln, s_t in zip(lanes, states)]
        if emit is not None:
            emit()
        for hd, ln in enumerate(lanes):
            ob[pl.ds(r0, C_CHUNK), ln] = inter[hd] + _dot(atts[hd], v[:, ln])
        for hd, ln in enumerate(lanes):
            st[hd] = states[hd] * decay[:, ln] + _dot_tn(v[:, ln], k_out[:, ln])
        return carry

    @pl.when(mild)
    def _():
        arm = refill.fork()
        left = arm.n_pieces - len(arm.emitted)
        for lo in range(0, C_K, PIECE):
            cs = slice(lo, lo + PIECE)
            a = ab[0:rows, cs]
            k = kb[0:rows, cs]
            a_last = jnp.concatenate(
                [jnp.broadcast_to(ab[(c + 1) * C_CHUNK - 1:(c + 1) * C_CHUNK, cs], (C_CHUNK, PIECE))
                 for c in range(n_chunks)], axis=0)
            qin[0:rows, cs] = (qb[0:rows, cs] * jnp.exp(a)).astype(BF16)
            kneg[0:rows, cs] = (k * jnp.exp(-a)).astype(BF16)
            kout[0:rows, cs] = (k * jnp.exp(a_last - a)).astype(BF16)
        left = arm.n_pieces - len(arm.emitted)
        for c in range(n_chunks):
            share = -(-left * (c + 1) // n_chunks) - (-(-left * c // n_chunks))
            chunk(c, 0, True, functools.partial(arm.emit, share))
        arm.flush()

    @pl.when(jnp.logical_not(mild))
    def _():
        refill.fork().flush()
        lax.fori_loop(0, n_chunks, functools.partial(chunk, factored=False), 0)

    y = None
    for lo in range(0, C_W, MXU_TILE):
        for ln in (slice(lo, lo + C_DV), slice(lo + C_DV, lo + 2 * C_DV)):
            on = _rmsnorm(ob[0:rows, ln], og_ref[:, ln])
            gated[0:rows, ln] = (on * zb[0:rows, ln]).astype(BF16)
        part = _dot(gated[0:rows, lo:lo + MXU_TILE], wout_ref[lo:lo + MXU_TILE, :])
        y = part if y is None else y + part
    if meta:
        y = jnp.where(_valid_rows(rows, D_MODEL), y, 0.0)
    return h + y


def _hgrn_kernel(x_ref, xn_ref, hm_ref, gain_ref, win_ref, lb_ref, og_ref, wout_ref,
                 y_ref, hm_out_ref, ub, qb, kb, ab, vb, zb, gated, st, qin, kneg, kout, ob, *pieces, tb):
    proj = _PieceBuffers(pieces)
    n_pieces = len(pieces)
    params = (lb_ref, og_ref, wout_ref, proj, qb, kb, ab, vb, zb, gated, st, qin, kneg, kout, ob)

    @pl.when(pl.program_id(1) == 0)
    def _():
        st[...] = jnp.zeros(st.shape, F32)
        hm = hm_ref[...]
        um = _rmsnorm(hm, gain_ref[...]).astype(BF16)
        for k in range(n_pieces):
            proj.project(k, BLOCK, um, win_ref)
        hm_out_ref[...] = _hgrn_rows(hm, BLOCK, True, *params, None)
        u0 = _rmsnorm(x_ref[0], gain_ref[...]).astype(BF16)
        for k in range(n_pieces):
            proj.project(k, tb, u0, win_ref)

    ub[...] = _rmsnorm(xn_ref[0], gain_ref[...]).astype(BF16)

    def next_piece(k):
        proj.project(k, tb, ub[...], win_ref)

    y_ref[0] = _hgrn_rows(x_ref[0], tb, False, *params, next_piece)


def _hgrn_layer(x, hm, gain, w_in, lb, o_gain, w_out, tb):
    bsz, seq, _ = x.shape
    n_tiles = seq // tb
    row_spec = pl.BlockSpec((1, tb, D_MODEL), lambda b, i: (b, i, 0))
    next_spec = pl.BlockSpec((1, tb, D_MODEL), lambda b, i: (b, jnp.minimum(i + 1, n_tiles - 1), 0))
    return pl.pallas_call(
        functools.partial(_hgrn_kernel, tb=tb),
        grid=(bsz, n_tiles),
        in_specs=[row_spec, next_spec, _const_spec((BLOCK, D_MODEL)), _const_spec((1, D_MODEL)),
                  _const_spec((D_MODEL, C_IN)), _const_spec((SUBLANES, C_K)), _const_spec((1, C_W)),
                  _const_spec((C_W, D_MODEL))],
        out_specs=[row_spec, _meta_out_spec()],
        out_shape=[jax.ShapeDtypeStruct(x.shape, F32), jax.ShapeDtypeStruct((BLOCK, D_MODEL), F32)],
        scratch_shapes=[pltpu.VMEM((tb, D_MODEL), BF16),
                        pltpu.VMEM((tb, C_K), F32), pltpu.VMEM((tb, C_K), F32),
                        pltpu.VMEM((tb, C_K), F32), pltpu.VMEM((tb, C_W), BF16),
                        pltpu.VMEM((tb, C_W), F32), pltpu.VMEM((tb, C_W), BF16),
                        pltpu.VMEM((C_HEADS, C_DV, C_DK), F32),
                        pltpu.VMEM((tb, C_K), BF16), pltpu.VMEM((tb, C_K), BF16),
                        pltpu.VMEM((tb, C_K), BF16), pltpu.VMEM((tb, C_W), F32)]
                       + [pltpu.VMEM((tb, C_PIECE), F32)] * (C_IN // C_PIECE),
        compiler_params=pltpu.CompilerParams(dimension_semantics=("arbitrary", "arbitrary"),
                                             vmem_limit_bytes=VMEM_LIMIT),
        name="hgrn2_layer",
    )(x, x, hm, gain, w_in, lb, o_gain, w_out)


def _rope_tables(seq):
    half = A_HD // 2
    inv = ROPE_THETA ** (-jnp.arange(half, dtype=F32) / half)
    pos = (jnp.arange(seq + BLOCK) - FRONT_PAD).astype(jnp.int32)
    ang = pos.astype(F32)[:, None] * inv[None, :]
    cos, sin = jnp.cos(ang), jnp.sin(ang)
    tab = jnp.concatenate([cos, cos, cos, cos, -sin, sin, -sin, sin], axis=-1)
    return tab[BLOCK:], tab[:BLOCK]


def _attn_gain_rows(q_gain, k_gain):
    half = A_HD // 2
    reps = LANES // A_HD
    rows = []
    for g, scale in ((q_gain.astype(F32), A_HD ** -0.5 * LOG2E), (k_gain.astype(F32), 1.0)):
        g_rot = jnp.concatenate([g[half:], g[:half]])
        rows += [jnp.tile(g * scale, reps), jnp.tile(g_rot * scale, reps)]
    return jnp.concatenate([jnp.stack(rows), jnp.zeros((SUBLANES - len(rows), LANES), F32)], axis=0)


def _row_tile(seq, want):
    tb = want
    while seq % tb:
        tb //= 2
    assert tb >= BLOCK, "sequence length must be a multiple of the 128-token block"
    return tb


def kernel(x, meta, norm_gain, a_w_in, a_q_gain, a_k_gain, a_sinks, a_w_out, b_w_in, b_conv_w,
           b_conv_b, b_gate_bias, b_h_gain, b_w_out, c_w_in, c_gamma, c_o_gain, c_w_out):
    bsz, seq, _ = x.shape
    depth = norm_gain.shape[0]
    tab, tabm = _rope_tables(seq)
    hm = jnp.concatenate([jnp.zeros((FRONT_PAD, D_MODEL), x.dtype), meta.astype(x.dtype)], axis=0)
    p = jax.nn.softmax(c_gamma.astype(F32), axis=0)
    lower_bounds = jnp.cumsum(p, axis=0) - p
    head_cols = jnp.concatenate([jnp.arange(A_HD) + A_HD * hq for hq in _attn_head_order()])

    h = x
    for i in range(depth):
        kind, j = i % 3, i // 3
        gain = norm_gain[i].reshape(1, D_MODEL).astype(F32)
        if kind == 0:
            w = a_w_in[j]
            w_in = jnp.concatenate([w[:, head_cols], w[:, A_W:A_W + 2 * A_KVW],
                                    w[:, A_W + 2 * A_KVW + head_cols]], axis=1)
            h, hm = _attn_layer(h, hm, tab, tabm, _attn_gain_rows(a_q_gain[j], a_k_gain[j]), gain,
                                w_in.astype(BF16), a_sinks[j].astype(F32) * LOG2E,
                                a_w_out[j][head_cols, :].astype(BF16), _row_tile(seq, 512))
        elif kind == 1:
            w = b_w_in[j]
            o_v, o_i, o_o, o_z = 2 * B_QK, 2 * B_QK + B_W, 2 * B_QK + B_W + 2 * B_HEADS, 2 * B_QK + 2 * B_W + 2 * B_HEADS
            wg = jnp.pad(w[:, o_i:o_o], ((0, 0), (0, LANES - 2 * B_HEADS)))
            gb = b_gate_bias[j].astype(F32)
            h, hm = _mlstm_layer(
                h, hm, gain, w[:, :o_v].astype(BF16), w[:, o_v:o_i].astype(BF16),
                w[:, o_o:o_z].astype(BF16), w[:, o_z:].astype(BF16), wg.astype(BF16),
                b_conv_w[j].astype(F32), b_conv_b[j].reshape(1, 2 * B_QK).astype(F32),
                jnp.broadcast_to(gb[:B_HEADS, None], (B_HEADS, BLOCK)),
                jnp.broadcast_to(gb[B_HEADS:, None], (B_HEADS, BLOCK)),
                b_h_gain[j].reshape(1, B_W).astype(F32), b_w_out[j].astype(BF16), _row_tile(seq, 512))
        else:
            w = c_w_in[j].astype(BF16)
            lb = lower_bounds[i]
            lb_rows = jnp.zeros((SUBLANES, C_K), F32)
            lb_rows = lb_rows.at[0].set(jnp.log(lb)).at[1].set(jnp.log1p(-lb)).at[2].set(1.0 - lb)
            h, hm = _hgrn_layer(
                h, hm, gain, w, lb_rows, c_o_gain[j].reshape(1, C_W).astype(F32),
                c_w_out[j].astype(BF16), _row_tile(seq, 512))
    return h
```

```python
import functools

import jax
import jax.numpy as jnp
from jax import lax
from jax.experimental import pallas as pl
from jax.experimental.pallas import tpu as pltpu

F32 = jnp.float32
BF16 = jnp.bfloat16

D_MODEL = 1024
BLOCK = 128
N_META = 16
FRONT_PAD = BLOCK - N_META
EPS = 1e-6
ROPE_THETA = 10000.0
LANES = 128
SUBLANES = 8
MXU_TILE = 256

A_HEADS, A_KV, A_HD = 16, 4, 64
A_GROUP = A_HEADS // A_KV
A_W = A_HEADS * A_HD
A_KVW = A_KV * A_HD
A_IN = 2 * A_W + 2 * A_KVW
A_PIECE = 2 * MXU_TILE
LOG2E = 1.4426950408889634

B_HEADS, B_DK, B_DV, B_CONV = 8, 128, 256, 4
B_QK = B_HEADS * B_DK
B_W = B_HEADS * B_DV
B_HIST = 8
B_CONV_COLS = MXU_TILE
B_QK_PIECES = 4

C_HEADS, C_DK, C_DV, C_CHUNK = 8, 128, 128, 64
C_K = C_HEADS * C_DK
C_W = C_HEADS * C_DV
C_IN = 2 * C_K + 2 * C_W
C_PIECE = MXU_TILE
C_LEVELS = (1, 2, 4, 8, 16, 32)
C_SAFE_DECAY = 60.0

VMEM_LIMIT = 60 * 1024 * 1024
B_VMEM_LIMIT = 62 * 1024 * 1024


def _sigmoid(x):
    return 0.5 + 0.5 * jnp.tanh(0.5 * x)


def _silu(x):
    half = 0.5 * x
    return half + half * jnp.tanh(half)


def _log_sigmoid(x):
    return jnp.minimum(x, 0.0) - jnp.log(1.0 + jnp.exp(-jnp.abs(x)))


def _rmsnorm(x, g):
    ms = jnp.mean(x * x, axis=-1, keepdims=True)
    return x * lax.rsqrt(ms + EPS) * g


def _dot(a, b):
    return jnp.dot(a, b, preferred_element_type=F32)


def _dot_nt(a, b):
    return lax.dot_general(a, b, (((1,), (1,)), ((), ())), preferred_element_type=F32)


def _dot_tn(a, b):
    return lax.dot_general(a, b, (((0,), (0,)), ((), ())), preferred_element_type=F32)


def _valid_rows(rows, cols):
    return lax.broadcasted_iota(jnp.int32, (rows, cols), 0) >= FRONT_PAD


class _PieceBuffers:
    def __init__(self, refs):
        self.refs = refs
        self.piece = refs[0].shape[1]

    def cols(self, rows, lo, hi):
        k = lo // self.piece
        assert hi <= (k + 1) * self.piece, "a read must stay inside one piece"
        return self.refs[k][0:rows, lo - k * self.piece:hi - k * self.piece]

    def project(self, k, rows, u, w_ref):
        self.refs[k][0:rows, :] = _dot(u, w_ref[:, k * self.piece:(k + 1) * self.piece])


class _PieceRefill:
    def __init__(self, proj, next_piece):
        self.next_piece = next_piece
        self.n_pieces = len(proj.refs) if next_piece is not None else 0
        self.groups = proj.piece // LANES
        self.consumed, self.emitted = set(), set()

    def done(self, lo, hi):
        assert lo % LANES == 0 and hi % LANES == 0
        self.consumed.update(range(lo // LANES, hi // LANES))

    def emit(self, n=1):
        for k in range(self.n_pieces):
            ready = all(k * self.groups + g in self.consumed for g in range(self.groups))
            if n > 0 and ready and k not in self.emitted:
                self.emitted.add(k)
                self.next_piece(k)
                n -= 1

    def flush(self):
        self.emit(self.n_pieces)

    def fork(self):
        other = _PieceRefill.__new__(_PieceRefill)
        other.__dict__.update(self.__dict__)
        other.consumed, other.emitted = set(self.consumed), set(self.emitted)
        return other


def _attn_head_order():
    order = []
    for p in range(A_W // LANES):
        m, g = divmod(p, A_GROUP)
        order += [A_GROUP * (2 * m) + g, A_GROUP * (2 * m + 1) + g]
    return order


def _attn_rows(h, rows, meta, tab, gains_ref, sink_ref, wout_ref, proj, qbuf, klo, khi, vt, obuf, gz, next_piece):
    lane = lax.broadcasted_iota(jnp.int32, (rows, LANES), 1)
    low_head = lane < A_HD
    first_half = (lane % A_HD) < (A_HD // 2)

    def norm_rope(x, ctab, stab):
        sq = x * x
        s_lo = jnp.sum(jnp.where(low_head, sq, 0.0), axis=-1, keepdims=True)
        s_hi = jnp.sum(jnp.where(low_head, 0.0, sq), axis=-1, keepdims=True)
        r = jnp.where(low_head, lax.rsqrt(s_lo / A_HD + EPS), lax.rsqrt(s_hi / A_HD + EPS))
        rot = jnp.where(first_half, pltpu.roll(x, LANES - A_HD // 2, 1), pltpu.roll(x, A_HD // 2, 1))
        return (x * ctab + rot * stab) * r

    refill = _PieceRefill(proj, next_piece)
    z0 = A_W + 2 * A_KVW
    cos, sin = tab[:, 0:LANES], tab[:, LANES:2 * LANES]
    q_cos, q_sin = cos * gains_ref[0:1, :], sin * gains_ref[1:2, :]
    k_cos, k_sin = cos * gains_ref[2:3, :], sin * gains_ref[3:4, :]
    for lo in range(0, A_W, A_PIECE):
        for p in range(lo // LANES, (lo + A_PIECE) // LANES):
            x = proj.cols(rows, p * LANES, (p + 1) * LANES)
            qbuf[0:rows, p * LANES:(p + 1) * LANES] = norm_rope(x, q_cos, q_sin).astype(BF16)
        refill.done(lo, lo + A_PIECE)
        refill.emit()
        gz[0:rows, lo:lo + A_PIECE] = _silu(proj.cols(rows, z0 + lo, z0 + lo + A_PIECE))
        refill.done(z0 + lo, z0 + lo + A_PIECE)
    for m in range(A_KVW // LANES):
        k = norm_rope(proj.cols(rows, A_W + m * LANES, A_W + (m + 1) * LANES), k_cos, k_sin)
        klo[BLOCK:BLOCK + rows, m * LANES:(m + 1) * LANES] = jnp.where(low_head, k, 0.0).astype(BF16)
        khi[BLOCK:BLOCK + rows, m * LANES:(m + 1) * LANES] = jnp.where(low_head, 0.0, k).astype(BF16)
    refill.done(A_W, A_W + A_KVW)
    refill.emit()
    for m in range(A_KVW // LANES):
        v = proj.cols(rows, A_W + A_KVW + m * LANES, A_W + A_KVW + (m + 1) * LANES)
        for n in range(rows // BLOCK):
            vt[m * LANES:(m + 1) * LANES, (n + 1) * BLOCK:(n + 2) * BLOCK] = (
                v[n * BLOCK:(n + 1) * BLOCK].T.astype(BF16))
    refill.done(A_W + A_KVW, z0)
    refill.emit()

    c_idx = lax.broadcasted_iota(jnp.int32, (BLOCK, BLOCK), 0)
    t_idx = lax.broadcasted_iota(jnp.int32, (BLOCK, BLOCK), 1)
    own = c_idx <= t_idx
    first_valid = FRONT_PAD if meta else jnp.where(pl.program_id(1) == 0, FRONT_PAD, 0)

    def fold(tile, first):
        prev, cur = tile[0:BLOCK], tile[BLOCK:2 * BLOCK]
        if meta:
            return jnp.where(own & (c_idx >= first_valid), cur, -jnp.inf)
        if first:
            prev = jnp.where(c_idx >= first_valid, prev, -jnp.inf)
        return jnp.where(own, cur, prev)

    def scores(n):
        r0 = n * BLOCK
        out = []
        for m in range(A_KVW // LANES):
            keys = jnp.concatenate([klo[r0:r0 + 2 * BLOCK, m * LANES:(m + 1) * LANES],
                                    khi[r0:r0 + 2 * BLOCK, m * LANES:(m + 1) * LANES]], axis=0)
            qs = jnp.concatenate([qbuf[r0:r0 + BLOCK, (A_GROUP * m + g) * LANES:(A_GROUP * m + g + 1) * LANES]
                                  for g in range(A_GROUP)], axis=0)
            out.append(_dot_nt(keys, qs))
        return out

    n_blocks = rows // BLOCK
    pending = scores(0)
    for n in range(n_blocks):
        r0 = n * BLOCK
        s_now = pending
        if n + 1 < n_blocks:
            pending = scores(n + 1)
        refill.emit()
        for m in range(A_KVW // LANES):
            outs = []
            for half in range(2):
                j = 2 * m + half
                es, inv = [], []
                for g in range(A_GROUP):
                    tile = fold(s_now[m][half * 2 * BLOCK:(half + 1) * 2 * BLOCK, g * BLOCK:(g + 1) * BLOCK], n == 0)
                    sink = sink_ref[A_GROUP * j + g]
                    mx = jnp.maximum(jnp.max(tile, axis=0, keepdims=True), sink)
                    e = jnp.exp2(tile - mx)
                    inv.append(1.0 / (jnp.sum(e, axis=0, keepdims=True) + jnp.exp2(sink - mx)))
                    es.append(jnp.concatenate([jnp.where(own, 0.0, e), jnp.where(own, e, 0.0)],
                                              axis=0).astype(BF16))
                o_t = _dot(vt[j * A_HD:(j + 1) * A_HD, r0:r0 + 2 * BLOCK], jnp.concatenate(es, axis=1))
                outs.append(o_t * jnp.concatenate(inv, axis=1))
            for g in range(A_GROUP):
                pair = jnp.concatenate([outs[0][:, g * BLOCK:(g + 1) * BLOCK],
                                        outs[1][:, g * BLOCK:(g + 1) * BLOCK]], axis=0)
                p = A_GROUP * m + g
                obuf[r0:r0 + BLOCK, p * LANES:(p + 1) * LANES] = pair.T

    refill.flush()
    klo[0:BLOCK, :] = klo[rows:rows + BLOCK, :]
    khi[0:BLOCK, :] = khi[rows:rows + BLOCK, :]
    vt[:, 0:BLOCK] = vt[:, rows:rows + BLOCK]

    y = _dot((obuf[0:rows, :] * gz[0:rows, :]).astype(BF16), wout_ref[...])
    if meta:
        y = jnp.where(_valid_rows(rows, D_MODEL), y, 0.0)
    return h + y


def _attn_kernel(x_ref, xn_ref, hm_ref, tab_ref, tabm_ref, gains_ref, gain_ref, win_ref, sink_ref, wout_ref,
                 y_ref, hm_out_ref, qbuf, klo, khi, vt, obuf, gz, ub, *pieces, tb):
    proj = _PieceBuffers(pieces)
    n_pieces = len(pieces)
    params = (gains_ref, sink_ref, wout_ref, proj, qbuf, klo, khi, vt, obuf, gz)

    @pl.when(pl.program_id(1) == 0)
    def _():
        klo[0:BLOCK, :] = jnp.zeros((BLOCK, A_KVW), BF16)
        khi[0:BLOCK, :] = jnp.zeros((BLOCK, A_KVW), BF16)
        vt[:, 0:BLOCK] = jnp.zeros((A_KVW, BLOCK), BF16)
        hm = hm_ref[...]
        um = _rmsnorm(hm, gain_ref[...]).astype(BF16)
        for k in range(n_pieces):
            proj.project(k, BLOCK, um, win_ref)
        hm_out_ref[...] = _attn_rows(hm, BLOCK, True, tabm_ref[...], *params, None)
        u0 = _rmsnorm(x_ref[0], gain_ref[...]).astype(BF16)
        for k in range(n_pieces):
            proj.project(k, tb, u0, win_ref)

    ub[...] = _rmsnorm(xn_ref[0], gain_ref[...]).astype(BF16)

    def next_piece(k):
        proj.project(k, tb, ub[...], win_ref)

    y_ref[0] = _attn_rows(x_ref[0], tb, False, tab_ref[...], *params, next_piece)


def _const_spec(shape):
    zeros = (0,) * len(shape)
    return pl.BlockSpec(shape, lambda b, i: zeros, pipeline_mode=pl.Buffered(1))


def _meta_out_spec():
    return pl.BlockSpec((BLOCK, D_MODEL), lambda b, i: (0, 0))


def _attn_layer(x, hm, tab, tabm, gains, gain, w_in, sinks, w_out, tb):
    bsz, seq, _ = x.shape
    n_tiles = seq // tb
    row_spec = pl.BlockSpec((1, tb, D_MODEL), lambda b, i: (b, i, 0))
    next_spec = pl.BlockSpec((1, tb, D_MODEL), lambda b, i: (b, jnp.minimum(i + 1, n_tiles - 1), 0))
    return pl.pallas_call(
        functools.partial(_attn_kernel, tb=tb),
        grid=(bsz, n_tiles),
        in_specs=[row_spec, next_spec, _const_spec((BLOCK, D_MODEL)),
                  pl.BlockSpec((tb, 2 * LANES), lambda b, i: (i, 0)), _const_spec((BLOCK, 2 * LANES)),
                  _const_spec((SUBLANES, LANES)), _const_spec((1, D_MODEL)), _const_spec((D_MODEL, A_IN)),
                  pl.BlockSpec(memory_space=pltpu.SMEM), _const_spec((A_W, D_MODEL))],
        out_specs=[row_spec, _meta_out_spec()],
        out_shape=[jax.ShapeDtypeStruct(x.shape, F32), jax.ShapeDtypeStruct((BLOCK, D_MODEL), F32)],
        scratch_shapes=[pltpu.VMEM((tb, A_W), BF16),
                        pltpu.VMEM((tb + BLOCK, A_KVW), BF16), pltpu.VMEM((tb + BLOCK, A_KVW), BF16),
                        pltpu.VMEM((A_KVW, tb + BLOCK), BF16), pltpu.VMEM((tb, A_W), F32),
                        pltpu.VMEM((tb, A_W), F32), pltpu.VMEM((tb, D_MODEL), BF16)]
                       + [pltpu.VMEM((tb, A_PIECE), F32)] * (A_IN // A_PIECE),
        compiler_params=pltpu.CompilerParams(dimension_semantics=("arbitrary", "arbitrary"),
                                             vmem_limit_bytes=VMEM_LIMIT),
        name="swa_layer",
    )(x, x, hm, tab, tabm, gains, gain, w_in, sinks, w_out)


def _mlstm_conv(rows, u, wv_ref, wg_ref, cw_ref, cb_ref, qkraw, qs, ks, vs, gates, between=None):
    n_blk = 2 * B_QK // B_CONV_COLS
    v_piece = B_W // B_QK_PIECES
    after_block = [[] for _ in range(n_blk)]
    for k in range(B_QK_PIECES):
        after_block[k * n_blk // B_QK_PIECES].append(("v", k))
    after_block[n_blk - 1].append(("g", 0))

    for blk in range(n_blk):
        cs = slice(blk * B_CONV_COLS, (blk + 1) * B_CONV_COLS)
        acc = cb_ref[:, cs]
        for j in range(B_CONV):
            start = B_HIST - (B_CONV - 1) + j
            acc = acc + qkraw[start:start + rows, cs] * cw_ref[j:j + 1, cs]
        act = _silu(acc)
        if blk * B_CONV_COLS < B_QK:
            qs[0:rows, cs] = act.astype(BF16)
        else:
            ks[0:rows, blk * B_CONV_COLS - B_QK:(blk + 1) * B_CONV_COLS - B_QK] = (
                act * (B_DK ** -0.5)).astype(BF16)
        qkraw[0:B_HIST, cs] = qkraw[rows:rows + B_HIST, cs]
        for kind, k in after_block[blk]:
            if kind == "v":
                vs[0:rows, k * v_piece:(k + 1) * v_piece] = _dot(
                    u(), wv_ref[:, k * v_piece:(k + 1) * v_piece]).astype(BF16)
            else:
                gates[0:rows, :] = _dot(u(), wg_ref[...])
        if between is not None:
            between(blk)


def _mlstm_chunks(rows, meta, bi_ref, bf_ref, qs, ks, vs, hb, gates, cst, mst, next_qk):
    t_idx = lax.broadcasted_iota(jnp.int32, (BLOCK, BLOCK), 0)
    s_idx = lax.broadcasted_iota(jnp.int32, (BLOCK, BLOCK), 1)
    causal = s_idx <= t_idx
    lane8 = lax.broadcasted_iota(jnp.int32, (B_HEADS, BLOCK), 1)

    n_chunks = rows // BLOCK

    def chunk(c):
        r0 = c * BLOCK
        g_t = gates[pl.ds(r0, BLOCK), :].T
        li = g_t[0:B_HEADS] + bi_ref[...]
        lf = _log_sigmoid(g_t[B_HEADS:2 * B_HEADS] + bf_ref[...])
        if meta:
            li = jnp.where(lane8 >= FRONT_PAD, li, -jnp.inf)
            lf = jnp.where(lane8 >= FRONT_PAD, lf, 0.0)
        a = lf
        shift = 1
        while shift < BLOCK:
            a = a + jnp.where(lane8 >= shift, pltpu.roll(a, shift, 1), 0.0)
            shift *= 2
        run = li - a
        shift = 1
        while shift < BLOCK:
            run = jnp.maximum(run, jnp.where(lane8 >= shift, pltpu.roll(run, shift, 1), -jnp.inf))
            shift *= 2
        m_prev = mst[:, 0:1]
        decay_max = jnp.maximum(m_prev, run)
        m_t = a + decay_max
        g_tot = a[:, BLOCK - 1:BLOCK]
        uu = g_tot - a + li
        m_new = jnp.maximum(g_tot + m_prev, jnp.max(uu, axis=1, keepdims=True))
        w_old = jnp.exp(g_tot + m_prev - m_new)
        w_s = jnp.exp(uu - m_new)
        b_row = li - a
        mst[...] = jnp.broadcast_to(m_new, (B_HEADS, LANES))

        def col_tile(row):
            return jnp.broadcast_to(row, (BLOCK, BLOCK)).T

        heads = range(B_HEADS)
        ones = jnp.ones((BLOCK, LANES), BF16)
        qh = [qs[pl.ds(r0, BLOCK), hd * B_DK:(hd + 1) * B_DK] for hd in heads]
        kh = [ks[pl.ds(r0, BLOCK), hd * B_DK:(hd + 1) * B_DK] for hd in heads]
        vh = [jnp.concatenate([vs[pl.ds(r0, BLOCK), hd * B_DV:(hd + 1) * B_DV], ones], axis=1) for hd in heads]
        c_prev = [cst[hd] for hd in heads]
        qk = [_dot_nt(qh[hd], kh[hd]) for hd in heads]
        inter_c = [_dot(qh[hd], c_prev[hd].astype(BF16)) for hd in heads]
        if next_qk is not None:
            for k in range(B_QK_PIECES):
                if k * n_chunks // B_QK_PIECES == c:
                    next_qk(k)

        sc, kw, w_inter, floor = [], [], [], []
        for hd in heads:
            dm_t = col_tile(decay_max[hd:hd + 1, :])
            w = jnp.exp(jnp.where(causal, b_row[hd:hd + 1, :] - dm_t, -jnp.inf))
            sc.append((qk[hd] * w).astype(BF16))
            kw.append((kh[hd].astype(F32) * col_tile(w_s[hd:hd + 1, :])).astype(BF16))
            w_inter.append(jnp.exp(m_prev[hd:hd + 1, :] - dm_t))
            floor.append(jnp.exp(-col_tile(m_t[hd:hd + 1, :])))

        pv = [_dot(sc[hd], vh[hd]) for hd in heads]
        kv = [_dot_tn(kw[hd], vh[hd]) for hd in heads]
        for hd in heads:
            den = w_inter[hd] * inter_c[hd][:, B_DV:] + pv[hd][:, B_DV:]
            scale = 1.0 / jnp.maximum(jnp.abs(den), floor[hd])
            for half in range(B_DV // LANES):
                ls = slice(half * LANES, (half + 1) * LANES)
                hb[pl.ds(r0, BLOCK), hd * B_DV + half * LANES:hd * B_DV + (half + 1) * LANES] = (
                    (w_inter[hd] * inter_c[hd][:, ls] + pv[hd][:, ls]) * scale)
            cst[hd] = w_old[hd:hd + 1, :] * c_prev[hd] + kv[hd]

    for c in range(n_chunks):
        chunk(c)


def _mlstm_post_head(hd, rows, u, wo_ref, wz_ref, hg_ref, hb, gated):
    cs = slice(hd * B_DV, (hd + 1) * B_DV)
    hh = _sigmoid(_dot(u(), wo_ref[:, cs])) * hb[0:rows, cs]
    gated[0:rows, cs] = (_rmsnorm(hh, hg_ref[:, cs]) * _silu(_dot(u(), wz_ref[:, cs]))).astype(BF16)


def _mlstm_kernel(x_ref, xn_ref, hm_ref, gain_ref, wqk_ref, wv_ref, wo_ref, wz_ref, wg_ref, cw_ref, cb_ref,
                  bi_ref, bf_ref, hg_ref, wout_ref, y_ref, hm_out_ref,
                  ub, qkraw, qkg, vs, hb, gates, gated, cst, mst, *, tb):
    qs, ks = qkg.at[:, 0:B_QK], qkg.at[:, B_QK:2 * B_QK]
    conv_refs = (wv_ref, wg_ref, cw_ref, cb_ref, qkraw, qs, ks, vs, gates)
    chunk_refs = (bi_ref, bf_ref, qs, ks, vs, hb, gates, cst, mst)
    post_refs = (wo_ref, wz_ref, hg_ref, hb, gated)
    slot = pl.program_id(1) % 2

    @pl.when(pl.program_id(1) == 0)
    def _():
        qkraw[0:B_HIST, :] = jnp.zeros((B_HIST, 2 * B_QK), F32)
        cst[...] = jnp.zeros(cst.shape, F32)
        mst[...] = jnp.zeros(mst.shape, F32)
        hm = hm_ref[...]
        ub[1, 0:BLOCK, :] = _rmsnorm(hm, gain_ref[...]).astype(BF16)
        um = lambda: ub[1, 0:BLOCK, :]
        qkraw[B_HIST:B_HIST + BLOCK, :] = _dot(um(), wqk_ref[...])
        _mlstm_conv(BLOCK, um, *conv_refs)
        _mlstm_chunks(BLOCK, True, *chunk_refs, None)
        for hd in range(B_HEADS):
            _mlstm_post_head(hd, BLOCK, um, *post_refs)
        ym = _dot(gated[0:BLOCK, :], wout_ref[...])
        hm_out_ref[...] = hm + jnp.where(_valid_rows(BLOCK, D_MODEL), ym, 0.0)
        ub[0] = _rmsnorm(x_ref[0], gain_ref[...]).astype(BF16)
        qkraw[B_HIST:B_HIST + tb, :] = _dot(ub[0], wqk_ref[...])
        _mlstm_conv(tb, lambda: ub[0], *conv_refs)

    ub[1 - slot] = _rmsnorm(xn_ref[0], gain_ref[...]).astype(BF16)
    qk_piece = 2 * B_QK // B_QK_PIECES

    def next_qk(k):
        cols = slice(k * qk_piece, (k + 1) * qk_piece)
        qkraw[B_HIST:B_HIST + tb, cols] = _dot(ub[1 - slot], wqk_ref[:, cols])

    _mlstm_chunks(tb, False, *chunk_refs, next_qk)
    assert 2 * B_QK // B_CONV_COLS == B_HEADS
    _mlstm_conv(tb, lambda: ub[1 - slot], *conv_refs,
                between=lambda blk: _mlstm_post_head(blk, tb, lambda: ub[slot], *post_refs))
    y_ref[0] = x_ref[0] + _dot(gated[...], wout_ref[...])


def _mlstm_layer(x, hm, gain, wqk, wv, wo, wz, wg, conv_w, conv_b, bias_i, bias_f, h_gain, w_out, tb):
    bsz, seq, _ = x.shape
    n_tiles = seq // tb
    row_spec = pl.BlockSpec((1, tb, D_MODEL), lambda b, i: (b, i, 0))
    next_spec = pl.BlockSpec((1, tb, D_MODEL), lambda b, i: (b, jnp.minimum(i + 1, n_tiles - 1), 0))
    return pl.pallas_call(
        functools.partial(_mlstm_kernel, tb=tb),
        grid=(bsz, n_tiles),
        in_specs=[row_spec, next_spec, _const_spec((BLOCK, D_MODEL)), _const_spec((1, D_MODEL)),
                  _const_spec((D_MODEL, 2 * B_QK)), _const_spec((D_MODEL, B_W)),
                  _const_spec((D_MODEL, B_W)), _const_spec((D_MODEL, B_W)),
                  _const_spec((D_MODEL, LANES)), _const_spec((B_CONV, 2 * B_QK)),
                  _const_spec((1, 2 * B_QK)), _const_spec((B_HEADS, BLOCK)),
                  _const_spec((B_HEADS, BLOCK)), _const_spec((1, B_W)), _const_spec((B_W, D_MODEL))],
        out_specs=[row_spec, _meta_out_spec()],
        out_shape=[jax.ShapeDtypeStruct(x.shape, F32), jax.ShapeDtypeStruct((BLOCK, D_MODEL), F32)],
        scratch_shapes=[pltpu.VMEM((2, tb, D_MODEL), BF16),
                        pltpu.VMEM((tb + B_HIST, 2 * B_QK), F32), pltpu.VMEM((tb, B_W), BF16),
                        pltpu.VMEM((tb, B_W), BF16), pltpu.VMEM((tb, B_W), F32),
                        pltpu.VMEM((tb, LANES), F32), pltpu.VMEM((tb, B_W), BF16),
                        pltpu.VMEM((B_HEADS, B_DK, B_DV + LANES), F32), pltpu.VMEM((B_HEADS, LANES), F32)],
        compiler_params=pltpu.CompilerParams(dimension_semantics=("arbitrary", "arbitrary"),
                                             vmem_limit_bytes=B_VMEM_LIMIT),
        name="mlstm_layer",
    )(x, x, hm, gain, wqk, wv, wo, wz, wg, conv_w, conv_b, bias_i, bias_f, h_gain, w_out)


def _hgrn_level_masks():
    t_idx = lax.broadcasted_iota(jnp.int32, (C_CHUNK, C_CHUNK), 0)
    s_idx = lax.broadcasted_iota(jnp.int32, (C_CHUNK, C_CHUNK), 1)
    masks = []
    for b in C_LEVELS:
        same = (t_idx // (2 * b)) == (s_idx // (2 * b))
        masks.append(same & ((t_idx % (2 * b)) >= b) & ((s_idx % (2 * b)) < b))
    return t_idx == s_idx, masks


def _hgrn_reference_rows(ab, r0, b):
    sub = lax.broadcasted_iota(jnp.int32, (8, C_K), 0)
    pieces = []
    if 2 * b >= 8:
        for p in range(C_CHUNK // (2 * b)):
            mid = r0 + 2 * b * p + b - 1
            pieces.append(jnp.broadcast_to(ab[pl.ds(mid, 1), :], (2 * b, C_K)))
    else:
        for v in range(C_CHUNK // 8):
            cand = [jnp.broadcast_to(ab[pl.ds(r0 + 8 * v + 2 * b * p + b - 1, 1), :], (8, C_K))
                    for p in range(8 // (2 * b))]
            sel = cand[-1]
            for p in reversed(range(len(cand) - 1)):
                sel = jnp.where(sub < 2 * b * (p + 1), cand[p], sel)
            pieces.append(sel)
    return jnp.concatenate(pieces, axis=0)


def _chunk_cumsum(x):
    rows, cols = x.shape
    x3 = x.reshape(rows // SUBLANES, SUBLANES, cols)
    sub = lax.broadcasted_iota(jnp.int32, x3.shape, 1)
    for s in (1, 2, 4):
        x3 = x3 + jnp.where(sub >= s, pltpu.roll(x3, s, 1), 0.0)
    x = x3.reshape(rows, cols)
    out = []
    for c in range(rows // C_CHUNK):
        run = None
        for g in range(C_CHUNK // SUBLANES):
            r = c * C_CHUNK + g * SUBLANES
            blk = x[r:r + SUBLANES]
            if run is not None:
                blk = blk + run
            out.append(blk)
            run = jnp.broadcast_to(blk[SUBLANES - 1:SUBLANES], (SUBLANES, cols))
    return jnp.concatenate(out, axis=0)


def _hgrn_rows(h, rows, meta, lb_ref, og_ref, wout_ref, proj, qb, kb, ab, vb, zb, gated, st,
               qin, kneg, kout, ob, next_piece):
    refill = _PieceRefill(proj, next_piece)
    PIECE = C_PIECE

    def unpack_gate(lo):
        zb[0:rows, lo:lo + PIECE] = _silu(proj.cols(rows, 3 * C_K + lo, 3 * C_K + lo + PIECE))
        refill.done(3 * C_K + lo, 3 * C_K + lo + PIECE)

    def unpack_value(lo):
        vb[0:rows, lo:lo + PIECE] = proj.cols(rows, 2 * C_K + lo, 2 * C_K + lo + PIECE).astype(BF16)
        refill.done(2 * C_K + lo, 2 * C_K + lo + PIECE)

    def unpack_query(lo):
        qb[0:rows, lo:lo + PIECE] = _silu(proj.cols(rows, lo, lo + PIECE))
        refill.done(lo, lo + PIECE)

    def unpack_forget(lo):
        cs = slice(lo, lo + LANES)
        fpre = proj.cols(rows, C_K + lo, C_K + lo + LANES)
        log_lb, log_1mlb, one_mlb = lb_ref[0:1, cs], lb_ref[1:2, cs], lb_ref[2:3, cs]
        soft = jnp.log(1.0 + jnp.exp(-jnp.abs(fpre)))
        log_sig = jnp.minimum(fpre, 0.0) - soft
        kk = one_mlb * jnp.exp(log_sig - fpre)
        grow = log_1mlb + log_sig
        lf = jnp.maximum(log_lb, grow) + jnp.log(1.0 + jnp.exp(-jnp.abs(log_lb - grow)))
        if meta:
            valid = _valid_rows(rows, LANES)
            lf = jnp.where(valid, lf, 0.0)
            kk = jnp.where(valid, kk, 0.0)
        ab[0:rows, cs] = _chunk_cumsum(lf)
        kb[0:rows, cs] = kk
        refill.done(C_K + lo, C_K + lo + LANES)

    light = ([functools.partial(unpack_gate, lo) for lo in range(0, C_W, PIECE)]
             + [functools.partial(unpack_query, lo) for lo in range(0, C_K, PIECE)]
             + [functools.partial(unpack_value, lo) for lo in range(0, C_W, PIECE)])
    heavy = [functools.partial(unpack_forget, lo) for lo in range(0, C_K, LANES)]
    for i in range(max(len(light), len(heavy))):
        if i < len(light):
            light[i]()
            if i % 2 == 0 and i < len(heavy):
                refill.emit()
        if i < len(heavy):
            heavy[i]()
            refill.emit()

    n_chunks = rows // C_CHUNK
    a_end = ab[C_CHUNK - 1:C_CHUNK, :]
    for c in range(1, n_chunks):
        a_end = jnp.minimum(a_end, ab[(c + 1) * C_CHUNK - 1:(c + 1) * C_CHUNK, :])
    mild = jnp.min(a_end) >= -C_SAFE_DECAY

    t_idx = lax.broadcasted_iota(jnp.int32, (C_CHUNK, C_CHUNK), 0)
    s_idx = lax.broadcasted_iota(jnp.int32, (C_CHUNK, C_CHUNK), 1)
    causal = s_idx <= t_idx
    eye, masks = _hgrn_level_masks()

    def chunk(c, carry, factored, emit=None):
        r0 = c * C_CHUNK if isinstance(c, int) else pl.multiple_of(c * C_CHUNK, C_CHUNK)
        v = vb[pl.ds(r0, C_CHUNK), :]
        a_last = ab[pl.ds(r0 + C_CHUNK - 1, 1), :]
        decay = jnp.exp(a_last)
        if factored:
            q_in = qin[pl.ds(r0, C_CHUNK), :]
            k_out = kout[pl.ds(r0, C_CHUNK), :]
            q_lv, k_lv, level_masks = [q_in], [kneg[pl.ds(r0, C_CHUNK), :]], [causal]
        else:
            a_c = ab[pl.ds(r0, C_CHUNK), :]
            q = qb[pl.ds(r0, C_CHUNK), :]
            k = kb[pl.ds(r0, C_CHUNK), :]
            q_in = (q * jnp.exp(a_c)).astype(BF16)
            k_out = (k * jnp.exp(a_last - a_c)).astype(BF16)
            q_lv, k_lv, level_masks = [q.astype(BF16)], [k.astype(BF16)], [eye] + masks
            for b in C_LEVELS:
                w = jnp.exp(-jnp.abs(a_c - _hgrn_reference_rows(ab, r0, b)))
                q_lv.append((q * w).astype(BF16))
                k_lv.append((k * w).astype(BF16))
        lanes = [slice(hd * C_DK, (hd + 1) * C_DK) for hd in range(C_HEADS)]
        atts = []
        for ln in lanes:
            att = None
            for ql, kl, mk in zip(q_lv, k_lv, level_masks):
                part = jnp.where(mk, _dot_nt(ql[:, ln], kl[:, ln]), 0.0)
                att = part if att is None else att + part
            atts.append(att.astype(BF16))
        states = [st[hd] for hd in range(C_HEADS)]
        inter = [_dot_nt(q_in[:, ln], s_t.astype(BF16)) for ln, s_t in zip(lanes, states)]
        if emit is not None:
            emit()
        for hd, ln in enumerate(lanes):
            ob[pl.ds(r0, C_CHUNK), ln] = inter[hd] + _dot(atts[hd], v[:, ln])
        for hd, ln in enumerate(lanes):
            st[hd] = states[hd] * decay[:, ln] + _dot_tn(v[:, ln], k_out[:, ln])
        return carry

    @pl.when(mild)
    def _():
        arm = refill.fork()
        left = arm.n_pieces - len(arm.emitted)
        for lo in range(0, C_K, PIECE):
            cs = slice(lo, lo + PIECE)
            a = ab[0:rows, cs]
            k = kb[0:rows, cs]
            a_last = jnp.concatenate(
                [jnp.broadcast_to(ab[(c + 1) * C_CHUNK - 1:(c + 1) * C_CHUNK, cs], (C_CHUNK, PIECE))
                 for c in range(n_chunks)], axis=0)
            qin[0:rows, cs] = (qb[0:rows, cs] * jnp.exp(a)).astype(BF16)
            kneg[0:rows, cs] = (k * jnp.exp(-a)).astype(BF16)
            kout[0:rows, cs] = (k * jnp.exp(a_last - a)).astype(BF16)
        left = arm.n_pieces - len(arm.emitted)
        for c in range(n_chunks):
            share = -(-left * (c + 1) // n_chunks) - (-(-left * c // n_chunks))
            chunk(c, 0, True, functools.partial(arm.emit, share))
        arm.flush()

    @pl.when(jnp.logical_not(mild))
    def _():
        refill.fork().flush()
        lax.fori_loop(0, n_chunks, functools.partial(chunk, factored=False), 0)

    y = None
    for lo in range(0, C_W, MXU_TILE):
        for ln in (slice(lo, lo + C_DV), slice(lo + C_DV, lo + 2 * C_DV)):
            on = _rmsnorm(ob[0:rows, ln], og_ref[:, ln])
            gated[0:rows, ln] = (on * zb[0:rows, ln]).astype(BF16)
        part = _dot(gated[0:rows, lo:lo + MXU_TILE], wout_ref[lo:lo + MXU_TILE, :])
        y = part if y is None else y + part
    if meta:
        y = jnp.where(_valid_rows(rows, D_MODEL), y, 0.0)
    return h + y


def _hgrn_kernel(x_ref, xn_ref, hm_ref, gain_ref, win_ref, lb_ref, og_ref, wout_ref,
                 y_ref, hm_out_ref, ub, qb, kb, ab, vb, zb, gated, st, qin, kneg, kout, ob, *pieces, tb):
    proj = _PieceBuffers(pieces)
    n_pieces = len(pieces)
    params = (lb_ref, og_ref, wout_ref, proj, qb, kb, ab, vb, zb, gated, st, qin, kneg, kout, ob)

    @pl.when(pl.program_id(1) == 0)
    def _():
        st[...] = jnp.zeros(st.shape, F32)
        hm = hm_ref[...]
        um = _rmsnorm(hm, gain_ref[...]).astype(BF16)
        for k in range(n_pieces):
            proj.project(k, BLOCK, um, win_ref)
        hm_out_ref[...] = _hgrn_rows(hm, BLOCK, True, *params, None)
        u0 = _rmsnorm(x_ref[0], gain_ref[...]).astype(BF16)
        for k in range(n_pieces):
            proj.project(k, tb, u0, win_ref)

    ub[...] = _rmsnorm(xn_ref[0], gain_ref[...]).astype(BF16)

    def next_piece(k):
        proj.project(k, tb, ub[...], win_ref)

    y_ref[0] = _hgrn_rows(x_ref[0], tb, False, *params, next_piece)


def _hgrn_layer(x, hm, gain, w_in, lb, o_gain, w_out, tb):
    bsz, seq, _ = x.shape
    n_tiles = seq // tb
    row_spec = pl.BlockSpec((1, tb, D_MODEL), lambda b, i: (b, i, 0))
    next_spec = pl.BlockSpec((1, tb, D_MODEL), lambda b, i: (b, jnp.minimum(i + 1, n_tiles - 1), 0))
    return pl.pallas_call(
        functools.partial(_hgrn_kernel, tb=tb),
        grid=(bsz, n_tiles),
        in_specs=[row_spec, next_spec, _const_spec((BLOCK, D_MODEL)), _const_spec((1, D_MODEL)),
                  _const_spec((D_MODEL, C_IN)), _const_spec((SUBLANES, C_K)), _const_spec((1, C_W)),
                  _const_spec((C_W, D_MODEL))],
        out_specs=[row_spec, _meta_out_spec()],
        out_shape=[jax.ShapeDtypeStruct(x.shape, F32), jax.ShapeDtypeStruct((BLOCK, D_MODEL), F32)],
        scratch_shapes=[pltpu.VMEM((tb, D_MODEL), BF16),
                        pltpu.VMEM((tb, C_K), F32), pltpu.VMEM((tb, C_K), F32),
                        pltpu.VMEM((tb, C_K), F32), pltpu.VMEM((tb, C_W), BF16),
                        pltpu.VMEM((tb, C_W), F32), pltpu.VMEM((tb, C_W), BF16),
                        pltpu.VMEM((C_HEADS, C_DV, C_DK), F32),
                        pltpu.VMEM((tb, C_K), BF16), pltpu.VMEM((tb, C_K), BF16),
                        pltpu.VMEM((tb, C_K), BF16), pltpu.VMEM((tb, C_W), F32)]
                       + [pltpu.VMEM((tb, C_PIECE), F32)] * (C_IN // C_PIECE),
        compiler_params=pltpu.CompilerParams(dimension_semantics=("arbitrary", "arbitrary"),
                                             vmem_limit_bytes=VMEM_LIMIT),
        name="hgrn2_layer",
    )(x, x, hm, gain, w_in, lb, o_gain, w_out)


def _rope_tables(first, n):
    half = A_HD // 2
    inv = ROPE_THETA ** (-jnp.arange(half, dtype=F32) / half)
    pos = (jnp.arange(first, first + n) - FRONT_PAD).astype(jnp.int32)
    ang = pos.astype(F32)[:, None] * inv[None, :]
    reps = LANES // half
    sign = jnp.tile(jnp.asarray([-1.0, 1.0], F32), reps // 2)
    cos = jnp.broadcast_to(jnp.cos(ang)[:, None, :], (n, reps, half))
    sin = jnp.sin(ang)[:, None, :] * sign[None, :, None]
    return jnp.concatenate([cos, sin], axis=1).reshape(n, 2 * LANES)


def _attn_gain_rows(q_gain, k_gain):
    half = A_HD // 2
    reps = LANES // A_HD
    rows = []
    for g, scale in ((q_gain.astype(F32), A_HD ** -0.5 * LOG2E), (k_gain.astype(F32), 1.0)):
        g_rot = jnp.concatenate([g[half:], g[:half]])
        rows += [jnp.tile(g * scale, reps), jnp.tile(g_rot * scale, reps)]
    return jnp.concatenate([jnp.stack(rows), jnp.zeros((SUBLANES - len(rows), LANES), F32)], axis=0)


def _row_tile(seq, want):
    tb = want
    while seq % tb:
        tb //= 2
    assert tb >= BLOCK, "sequence length must be a multiple of the 128-token block"
    return tb


def kernel(x, meta, norm_gain, a_w_in, a_q_gain, a_k_gain, a_sinks, a_w_out, b_w_in, b_conv_w,
           b_conv_b, b_gate_bias, b_h_gain, b_w_out, c_w_in, c_gamma, c_o_gain, c_w_out):
    bsz, seq, _ = x.shape
    depth = norm_gain.shape[0]
    tab, tabm = _rope_tables(BLOCK, seq), _rope_tables(0, BLOCK)
    hm = jnp.concatenate([jnp.zeros((FRONT_PAD, D_MODEL), x.dtype), meta.astype(x.dtype)], axis=0)
    p = jax.nn.softmax(c_gamma.astype(F32), axis=0)
    lower_bounds = jnp.cumsum(p, axis=0) - p

    def pair_heads(w, axis):
        shape = w.shape[:axis] + (A_KV // 2, 2, A_GROUP, A_HD) + w.shape[axis + 1:]
        perm = list(range(len(shape)))
        perm[axis + 1], perm[axis + 2] = axis + 2, axis + 1
        return w.reshape(shape).transpose(perm).reshape(w.shape)

    h = x
    for i in range(depth):
        kind, j = i % 3, i // 3
        gain = norm_gain[i].reshape(1, D_MODEL).astype(F32)
        if kind == 0:
            w = a_w_in[j].astype(BF16)
            w_in = jnp.concatenate([pair_heads(w[:, :A_W], 1), w[:, A_W:A_W + 2 * A_KVW],
                                    pair_heads(w[:, A_W + 2 * A_KVW:], 1)], axis=1)
            h, hm = _attn_layer(h, hm, tab, tabm, _attn_gain_rows(a_q_gain[j], a_k_gain[j]), gain,
                                w_in, a_sinks[j].astype(F32) * LOG2E,
                                pair_heads(a_w_out[j].astype(BF16), 0), _row_tile(seq, 512))
        elif kind == 1:
            w = b_w_in[j]
            o_v, o_i, o_o, o_z = 2 * B_QK, 2 * B_QK + B_W, 2 * B_QK + B_W + 2 * B_HEADS, 2 * B_QK + 2 * B_W + 2 * B_HEADS
            wg = jnp.pad(w[:, o_i:o_o], ((0, 0), (0, LANES - 2 * B_HEADS)))
            gb = b_gate_bias[j].astype(F32)
            h, hm = _mlstm_layer(
                h, hm, gain, w[:, :o_v].astype(BF16), w[:, o_v:o_i].astype(BF16),
                w[:, o_o:o_z].astype(BF16), w[:, o_z:].astype(BF16), wg.astype(BF16),
                b_conv_w[j].astype(F32), b_conv_b[j].reshape(1, 2 * B_QK).astype(F32),
                jnp.broadcast_to(gb[:B_HEADS, None], (B_HEADS, BLOCK)),
                jnp.broadcast_to(gb[B_HEADS:, None], (B_HEADS, BLOCK)),
                b_h_gain[j].reshape(1, B_W).astype(F32), b_w_out[j].astype(BF16), _row_tile(seq, 512))
        else:
            w = c_w_in[j].astype(BF16)
            lb = lower_bounds[i]
            lb_rows = jnp.zeros((SUBLANES, C_K), F32)
            lb_rows = lb_rows.at[0].set(jnp.log(lb)).at[1].set(jnp.log1p(-lb)).at[2].set(1.0 - lb)
            h, hm = _hgrn_layer(
                h, hm, gain, w, lb_rows, c_o_gain[j].reshape(1, C_W).astype(F32),
                c_w_out[j].astype(BF16), _row_tile(seq, 512))
    return h
```

```python
import functools

import jax
import jax.numpy as jnp
from jax import lax
from jax.experimental import pallas as pl
from jax.experimental.pallas import tpu as pltpu

F32 = jnp.float32
BF16 = jnp.bfloat16

D_MODEL = 1024
BLOCK = 128
N_META = 16
FRONT_PAD = BLOCK - N_META
EPS = 1e-6
ROPE_THETA = 10000.0
LANES = 128
SUBLANES = 8
MXU_TILE = 256
ROW_TILE = 512

A_HEADS, A_KV, A_HD = 16, 4, 64
A_GROUP = A_HEADS // A_KV
A_W = A_HEADS * A_HD
A_KVW = A_KV * A_HD
A_IN = 2 * A_W + 2 * A_KVW
A_PIECE = 2 * MXU_TILE
LOG2E = 1.4426950408889634

B_HEADS, B_DK, B_DV, B_CONV = 8, 128, 256, 4
B_QK = B_HEADS * B_DK
B_W = B_HEADS * B_DV
B_HIST = 8
B_CONV_COLS = MXU_TILE
B_QK_PIECES = 4

C_HEADS, C_DK, C_DV, C_CHUNK = 8, 128, 128, 64
C_K = C_HEADS * C_DK
C_W = C_HEADS * C_DV
C_IN = 2 * C_K + 2 * C_W
C_PIECE = MXU_TILE
C_LEVELS = (1, 2, 4, 8, 16, 32)
C_SAFE_DECAY = 60.0

VMEM_LIMIT = 60 * 1024 * 1024
B_VMEM_LIMIT = 62 * 1024 * 1024


def _sigmoid(x):
    return 0.5 + 0.5 * jnp.tanh(0.5 * x)


def _silu(x):
    half = 0.5 * x
    return half + half * jnp.tanh(half)


def _log_sigmoid(x):
    return jnp.minimum(x, 0.0) - jnp.log(1.0 + jnp.exp(-jnp.abs(x)))


def _rmsnorm(x, g):
    ms = jnp.mean(x * x, axis=-1, keepdims=True)
    return x * lax.rsqrt(ms + EPS) * g


def _dot(a, b):
    return jnp.dot(a, b, preferred_element_type=F32)


def _dot_nt(a, b):
    return lax.dot_general(a, b, (((1,), (1,)), ((), ())), preferred_element_type=F32)


def _dot_tn(a, b):
    return lax.dot_general(a, b, (((0,), (0,)), ((), ())), preferred_element_type=F32)


def _valid_rows(rows, cols):
    return lax.broadcasted_iota(jnp.int32, (rows, cols), 0) >= FRONT_PAD


class _PieceBuffers:
    def __init__(self, refs):
        self.refs = refs
        self.piece = refs[0].shape[1]

    def cols(self, rows, lo, hi):
        k = lo // self.piece
        assert hi <= (k + 1) * self.piece, "a read must stay inside one piece"
        return self.refs[k][0:rows, lo - k * self.piece:hi - k * self.piece]

    def project(self, k, rows, u, w_ref):
        self.refs[k][0:rows, :] = _dot(u, w_ref[:, k * self.piece:(k + 1) * self.piece])


class _PieceRefill:
    def __init__(self, proj, next_piece):
        self.next_piece = next_piece
        self.n_pieces = len(proj.refs) if next_piece is not None else 0
        self.groups = proj.piece // LANES
        self.consumed, self.emitted = set(), set()

    def done(self, lo, hi):
        assert lo % LANES == 0 and hi % LANES == 0
        self.consumed.update(range(lo // LANES, hi // LANES))

    def emit(self, n=1):
        for k in range(self.n_pieces):
            ready = all(k * self.groups + g in self.consumed for g in range(self.groups))
            if n > 0 and ready and k not in self.emitted:
                self.emitted.add(k)
                self.next_piece(k)
                n -= 1

    def flush(self):
        self.emit(self.n_pieces)

    def fork(self):
        other = _PieceRefill.__new__(_PieceRefill)
        other.__dict__.update(self.__dict__)
        other.consumed, other.emitted = set(self.consumed), set(self.emitted)
        return other


def _attn_rows(h, rows, meta, tab, gains_ref, sink_ref, wout_ref, proj, qbuf, klo, khi, vt, obuf, gz, next_piece):
    lane = lax.broadcasted_iota(jnp.int32, (rows, LANES), 1)
    low_head = lane < A_HD
    first_half = (lane % A_HD) < (A_HD // 2)

    def norm_rope(x, ctab, stab):
        sq = x * x
        s_lo = jnp.sum(jnp.where(low_head, sq, 0.0), axis=-1, keepdims=True)
        s_hi = jnp.sum(jnp.where(low_head, 0.0, sq), axis=-1, keepdims=True)
        r = jnp.where(low_head, lax.rsqrt(s_lo / A_HD + EPS), lax.rsqrt(s_hi / A_HD + EPS))
        rot = jnp.where(first_half, pltpu.roll(x, LANES - A_HD // 2, 1), pltpu.roll(x, A_HD // 2, 1))
        return (x * ctab + rot * stab) * r

    refill = _PieceRefill(proj, next_piece)
    z0 = A_W + 2 * A_KVW
    cos, sin = tab[:, 0:LANES], tab[:, LANES:2 * LANES]
    q_cos, q_sin = cos * gains_ref[0:1, :], sin * gains_ref[1:2, :]
    k_cos, k_sin = cos * gains_ref[2:3, :], sin * gains_ref[3:4, :]
    for lo in range(0, A_W, A_PIECE):
        for p in range(lo // LANES, (lo + A_PIECE) // LANES):
            x = proj.cols(rows, p * LANES, (p + 1) * LANES)
            qbuf[0:rows, p * LANES:(p + 1) * LANES] = norm_rope(x, q_cos, q_sin).astype(BF16)
        refill.done(lo, lo + A_PIECE)
        refill.emit()
        gz[0:rows, lo:lo + A_PIECE] = _silu(proj.cols(rows, z0 + lo, z0 + lo + A_PIECE))
        refill.done(z0 + lo, z0 + lo + A_PIECE)
    for m in range(A_KVW // LANES):
        k = norm_rope(proj.cols(rows, A_W + m * LANES, A_W + (m + 1) * LANES), k_cos, k_sin)
        klo[BLOCK:BLOCK + rows, m * LANES:(m + 1) * LANES] = jnp.where(low_head, k, 0.0).astype(BF16)
        khi[BLOCK:BLOCK + rows, m * LANES:(m + 1) * LANES] = jnp.where(low_head, 0.0, k).astype(BF16)
    refill.done(A_W, A_W + A_KVW)
    refill.emit()
    for m in range(A_KVW // LANES):
        v = proj.cols(rows, A_W + A_KVW + m * LANES, A_W + A_KVW + (m + 1) * LANES)
        for n in range(rows // BLOCK):
            vt[m * LANES:(m + 1) * LANES, (n + 1) * BLOCK:(n + 2) * BLOCK] = (
                v[n * BLOCK:(n + 1) * BLOCK].T.astype(BF16))
    refill.done(A_W + A_KVW, z0)
    refill.emit()

    c_idx = lax.broadcasted_iota(jnp.int32, (BLOCK, BLOCK), 0)
    t_idx = lax.broadcasted_iota(jnp.int32, (BLOCK, BLOCK), 1)
    own = c_idx <= t_idx
    first_valid = FRONT_PAD if meta else jnp.where(pl.program_id(1) == 0, FRONT_PAD, 0)

    def fold(tile, first):
        prev, cur = tile[0:BLOCK], tile[BLOCK:2 * BLOCK]
        if meta:
            return jnp.where(own & (c_idx >= first_valid), cur, -jnp.inf)
        if first:
            prev = jnp.where(c_idx >= first_valid, prev, -jnp.inf)
        return jnp.where(own, cur, prev)

    def scores(n):
        r0 = n * BLOCK
        out = []
        for m in range(A_KVW // LANES):
            keys = jnp.concatenate([klo[r0:r0 + 2 * BLOCK, m * LANES:(m + 1) * LANES],
                                    khi[r0:r0 + 2 * BLOCK, m * LANES:(m + 1) * LANES]], axis=0)
            qs = jnp.concatenate([qbuf[r0:r0 + BLOCK, (A_GROUP * m + g) * LANES:(A_GROUP * m + g + 1) * LANES]
                                  for g in range(A_GROUP)], axis=0)
            out.append(_dot_nt(keys, qs))
        return out

    n_blocks = rows // BLOCK
    pending = scores(0)
    for n in range(n_blocks):
        r0 = n * BLOCK
        s_now = pending
        if n + 1 < n_blocks:
            pending = scores(n + 1)
        refill.emit()
        for m in range(A_KVW // LANES):
            outs = []
            for half in range(2):
                j = 2 * m + half
                es, inv = [], []
                for g in range(A_GROUP):
                    tile = fold(s_now[m][half * 2 * BLOCK:(half + 1) * 2 * BLOCK, g * BLOCK:(g + 1) * BLOCK], n == 0)
                    sink = sink_ref[A_GROUP * j + g]
                    mx = jnp.maximum(jnp.max(tile, axis=0, keepdims=True), sink)
                    e = jnp.exp2(tile - mx)
                    inv.append(1.0 / (jnp.sum(e, axis=0, keepdims=True) + jnp.exp2(sink - mx)))
                    es.append(jnp.concatenate([jnp.where(own, 0.0, e), jnp.where(own, e, 0.0)],
                                              axis=0).astype(BF16))
                o_t = _dot(vt[j * A_HD:(j + 1) * A_HD, r0:r0 + 2 * BLOCK], jnp.concatenate(es, axis=1))
                outs.append(o_t * jnp.concatenate(inv, axis=1))
            for g in range(A_GROUP):
                pair = jnp.concatenate([outs[0][:, g * BLOCK:(g + 1) * BLOCK],
                                        outs[1][:, g * BLOCK:(g + 1) * BLOCK]], axis=0)
                p = A_GROUP * m + g
                obuf[r0:r0 + BLOCK, p * LANES:(p + 1) * LANES] = pair.T

    refill.flush()
    klo[0:BLOCK, :] = klo[rows:rows + BLOCK, :]
    khi[0:BLOCK, :] = khi[rows:rows + BLOCK, :]
    vt[:, 0:BLOCK] = vt[:, rows:rows + BLOCK]

    y = _dot((obuf[0:rows, :] * gz[0:rows, :]).astype(BF16), wout_ref[...])
    if meta:
        y = jnp.where(_valid_rows(rows, D_MODEL), y, 0.0)
    return h + y


def _attn_kernel(x_ref, xn_ref, hm_ref, tab_ref, tabm_ref, gains_ref, gain_ref, win_ref, sink_ref, wout_ref,
                 y_ref, hm_out_ref, qbuf, klo, khi, vt, obuf, gz, ub, *pieces, tb):
    proj = _PieceBuffers(pieces)
    n_pieces = len(pieces)
    params = (gains_ref, sink_ref, wout_ref, proj, qbuf, klo, khi, vt, obuf, gz)

    @pl.when(pl.program_id(1) == 0)
    def _():
        klo[0:BLOCK, :] = jnp.zeros((BLOCK, A_KVW), BF16)
        khi[0:BLOCK, :] = jnp.zeros((BLOCK, A_KVW), BF16)
        vt[:, 0:BLOCK] = jnp.zeros((A_KVW, BLOCK), BF16)
        hm = hm_ref[...]
        um = _rmsnorm(hm, gain_ref[...]).astype(BF16)
        for k in range(n_pieces):
            proj.project(k, BLOCK, um, win_ref)
        hm_out_ref[...] = _attn_rows(hm, BLOCK, True, tabm_ref[...], *params, None)
        u0 = _rmsnorm(x_ref[0], gain_ref[...]).astype(BF16)
        for k in range(n_pieces):
            proj.project(k, tb, u0, win_ref)

    ub[...] = _rmsnorm(xn_ref[0], gain_ref[...]).astype(BF16)

    def next_piece(k):
        proj.project(k, tb, ub[...], win_ref)

    y_ref[0] = _attn_rows(x_ref[0], tb, False, tab_ref[...], *params, next_piece)


def _const_spec(shape):
    zeros = (0,) * len(shape)
    return pl.BlockSpec(shape, lambda b, i: zeros, pipeline_mode=pl.Buffered(1))


def _meta_out_spec():
    return pl.BlockSpec((BLOCK, D_MODEL), lambda b, i: (0, 0))


def _attn_layer(x, hm, tab, tabm, gains, gain, w_in, sinks, w_out, tb):
    bsz, seq, _ = x.shape
    n_tiles = seq // tb
    row_spec = pl.BlockSpec((1, tb, D_MODEL), lambda b, i: (b, i, 0))
    next_spec = pl.BlockSpec((1, tb, D_MODEL), lambda b, i: (b, jnp.minimum(i + 1, n_tiles - 1), 0))
    return pl.pallas_call(
        functools.partial(_attn_kernel, tb=tb),
        grid=(bsz, n_tiles),
        in_specs=[row_spec, next_spec, _const_spec((BLOCK, D_MODEL)),
                  pl.BlockSpec((tb, 2 * LANES), lambda b, i: (i, 0)), _const_spec((BLOCK, 2 * LANES)),
                  _const_spec((SUBLANES, LANES)), _const_spec((1, D_MODEL)), _const_spec((D_MODEL, A_IN)),
                  pl.BlockSpec(memory_space=pltpu.SMEM), _const_spec((A_W, D_MODEL))],
        out_specs=[row_spec, _meta_out_spec()],
        out_shape=[jax.ShapeDtypeStruct(x.shape, F32), jax.ShapeDtypeStruct((BLOCK, D_MODEL), F32)],
        scratch_shapes=[pltpu.VMEM((tb, A_W), BF16),
                        pltpu.VMEM((tb + BLOCK, A_KVW), BF16), pltpu.VMEM((tb + BLOCK, A_KVW), BF16),
                        pltpu.VMEM((A_KVW, tb + BLOCK), BF16), pltpu.VMEM((tb, A_W), F32),
                        pltpu.VMEM((tb, A_W), F32), pltpu.VMEM((tb, D_MODEL), BF16)]
                       + [pltpu.VMEM((tb, A_PIECE), F32)] * (A_IN // A_PIECE),
        compiler_params=pltpu.CompilerParams(dimension_semantics=("arbitrary", "arbitrary"),
                                             vmem_limit_bytes=VMEM_LIMIT),
        name="swa_layer",
    )(x, x, hm, tab, tabm, gains, gain, w_in, sinks, w_out)


def _mlstm_conv(rows, u, wv_ref, wg_ref, cw_ref, cb_ref, qkraw, qs, ks, vs, gates, between=None):
    n_blk = 2 * B_QK // B_CONV_COLS
    v_piece = B_W // B_QK_PIECES
    after_block = [[] for _ in range(n_blk)]
    for k in range(B_QK_PIECES):
        after_block[k * n_blk // B_QK_PIECES].append(("v", k))
    after_block[n_blk - 1].append(("g", 0))

    for blk in range(n_blk):
        cs = slice(blk * B_CONV_COLS, (blk + 1) * B_CONV_COLS)
        acc = cb_ref[:, cs]
        for j in range(B_CONV):
            start = B_HIST - (B_CONV - 1) + j
            acc = acc + qkraw[start:start + rows, cs] * cw_ref[j:j + 1, cs]
        act = _silu(acc)
        if blk * B_CONV_COLS < B_QK:
            qs[0:rows, cs] = act.astype(BF16)
        else:
            ks[0:rows, blk * B_CONV_COLS - B_QK:(blk + 1) * B_CONV_COLS - B_QK] = (
                act * (B_DK ** -0.5)).astype(BF16)
        qkraw[0:B_HIST, cs] = qkraw[rows:rows + B_HIST, cs]
        for kind, k in after_block[blk]:
            if kind == "v":
                vs[0:rows, k * v_piece:(k + 1) * v_piece] = _dot(
                    u(), wv_ref[:, k * v_piece:(k + 1) * v_piece]).astype(BF16)
            else:
                gates[0:rows, :] = _dot(u(), wg_ref[...])
        if between is not None:
            between(blk)


def _mlstm_chunks(rows, meta, bi_ref, bf_ref, qs, ks, vs, hb, gates, cst, mst, next_qk):
    t_idx = lax.broadcasted_iota(jnp.int32, (BLOCK, BLOCK), 0)
    s_idx = lax.broadcasted_iota(jnp.int32, (BLOCK, BLOCK), 1)
    causal = s_idx <= t_idx
    lane8 = lax.broadcasted_iota(jnp.int32, (B_HEADS, BLOCK), 1)

    n_chunks = rows // BLOCK

    def chunk(c):
        r0 = c * BLOCK
        g_t = gates[pl.ds(r0, BLOCK), :].T
        li = g_t[0:B_HEADS] + bi_ref[...]
        lf = _log_sigmoid(g_t[B_HEADS:2 * B_HEADS] + bf_ref[...])
        if meta:
            li = jnp.where(lane8 >= FRONT_PAD, li, -jnp.inf)
            lf = jnp.where(lane8 >= FRONT_PAD, lf, 0.0)
        a = lf
        shift = 1
        while shift < BLOCK:
            a = a + jnp.where(lane8 >= shift, pltpu.roll(a, shift, 1), 0.0)
            shift *= 2
        run = li - a
        shift = 1
        while shift < BLOCK:
            run = jnp.maximum(run, jnp.where(lane8 >= shift, pltpu.roll(run, shift, 1), -jnp.inf))
            shift *= 2
        m_prev = mst[:, 0:1]
        decay_max = jnp.maximum(m_prev, run)
        m_t = a + decay_max
        g_tot = a[:, BLOCK - 1:BLOCK]
        uu = g_tot - a + li
        m_new = jnp.maximum(g_tot + m_prev, jnp.max(uu, axis=1, keepdims=True))
        w_old = jnp.exp(g_tot + m_prev - m_new)
        w_s = jnp.exp(uu - m_new)
        b_row = li - a
        mst[...] = jnp.broadcast_to(m_new, (B_HEADS, LANES))

        def col_tile(row):
            return jnp.broadcast_to(row, (BLOCK, BLOCK)).T

        heads = range(B_HEADS)
        ones = jnp.ones((BLOCK, LANES), BF16)
        qh = [qs[pl.ds(r0, BLOCK), hd * B_DK:(hd + 1) * B_DK] for hd in heads]
        kh = [ks[pl.ds(r0, BLOCK), hd * B_DK:(hd + 1) * B_DK] for hd in heads]
        vh = [jnp.concatenate([vs[pl.ds(r0, BLOCK), hd * B_DV:(hd + 1) * B_DV], ones], axis=1) for hd in heads]
        c_prev = [cst[hd] for hd in heads]
        qk = [_dot_nt(qh[hd], kh[hd]) for hd in heads]
        inter_c = [_dot(qh[hd], c_prev[hd].astype(BF16)) for hd in heads]
        if next_qk is not None:
            for k in range(B_QK_PIECES):
                if k * n_chunks // B_QK_PIECES == c:
                    next_qk(k)

        sc, kw, w_inter, floor = [], [], [], []
        for hd in heads:
            dm_t = col_tile(decay_max[hd:hd + 1, :])
            w = jnp.exp(jnp.where(causal, b_row[hd:hd + 1, :] - dm_t, -jnp.inf))
            sc.append((qk[hd] * w).astype(BF16))
            kw.append((kh[hd].astype(F32) * col_tile(w_s[hd:hd + 1, :])).astype(BF16))
            w_inter.append(jnp.exp(m_prev[hd:hd + 1, :] - dm_t))
            floor.append(jnp.exp(-col_tile(m_t[hd:hd + 1, :])))

        pv = [_dot(sc[hd], vh[hd]) for hd in heads]
        kv = [_dot_tn(kw[hd], vh[hd]) for hd in heads]
        for hd in heads:
            den = w_inter[hd] * inter_c[hd][:, B_DV:] + pv[hd][:, B_DV:]
            scale = 1.0 / jnp.maximum(jnp.abs(den), floor[hd])
            for half in range(B_DV // LANES):
                ls = slice(half * LANES, (half + 1) * LANES)
                hb[pl.ds(r0, BLOCK), hd * B_DV + half * LANES:hd * B_DV + (half + 1) * LANES] = (
                    (w_inter[hd] * inter_c[hd][:, ls] + pv[hd][:, ls]) * scale)
            cst[hd] = w_old[hd:hd + 1, :] * c_prev[hd] + kv[hd]

    for c in range(n_chunks):
        chunk(c)


def _mlstm_post_head(hd, rows, u, wo_ref, wz_ref, hg_ref, hb, gated):
    cs = slice(hd * B_DV, (hd + 1) * B_DV)
    hh = _sigmoid(_dot(u(), wo_ref[:, cs])) * hb[0:rows, cs]
    gated[0:rows, cs] = (_rmsnorm(hh, hg_ref[:, cs]) * _silu(_dot(u(), wz_ref[:, cs]))).astype(BF16)


def _mlstm_kernel(x_ref, xn_ref, hm_ref, gain_ref, wqk_ref, wv_ref, wo_ref, wz_ref, wg_ref, cw_ref, cb_ref,
                  bi_ref, bf_ref, hg_ref, wout_ref, y_ref, hm_out_ref,
                  ub, qkraw, qkg, vs, hb, gates, gated, cst, mst, *, tb):
    qs, ks = qkg.at[:, 0:B_QK], qkg.at[:, B_QK:2 * B_QK]
    conv_refs = (wv_ref, wg_ref, cw_ref, cb_ref, qkraw, qs, ks, vs, gates)
    chunk_refs = (bi_ref, bf_ref, qs, ks, vs, hb, gates, cst, mst)
    post_refs = (wo_ref, wz_ref, hg_ref, hb, gated)
    slot = pl.program_id(1) % 2

    @pl.when(pl.program_id(1) == 0)
    def _():
        qkraw[0:B_HIST, :] = jnp.zeros((B_HIST, 2 * B_QK), F32)
        cst[...] = jnp.zeros(cst.shape, F32)
        mst[...] = jnp.zeros(mst.shape, F32)
        hm = hm_ref[...]
        ub[1, 0:BLOCK, :] = _rmsnorm(hm, gain_ref[...]).astype(BF16)
        um = lambda: ub[1, 0:BLOCK, :]
        qkraw[B_HIST:B_HIST + BLOCK, :] = _dot(um(), wqk_ref[...])
        _mlstm_conv(BLOCK, um, *conv_refs)
        _mlstm_chunks(BLOCK, True, *chunk_refs, None)
        for hd in range(B_HEADS):
            _mlstm_post_head(hd, BLOCK, um, *post_refs)
        ym = _dot(gated[0:BLOCK, :], wout_ref[...])
        hm_out_ref[...] = hm + jnp.where(_valid_rows(BLOCK, D_MODEL), ym, 0.0)
        ub[0] = _rmsnorm(x_ref[0], gain_ref[...]).astype(BF16)
        qkraw[B_HIST:B_HIST + tb, :] = _dot(ub[0], wqk_ref[...])
        _mlstm_conv(tb, lambda: ub[0], *conv_refs)

    ub[1 - slot] = _rmsnorm(xn_ref[0], gain_ref[...]).astype(BF16)
    qk_piece = 2 * B_QK // B_QK_PIECES

    def next_qk(k):
        cols = slice(k * qk_piece, (k + 1) * qk_piece)
        qkraw[B_HIST:B_HIST + tb, cols] = _dot(ub[1 - slot], wqk_ref[:, cols])

    _mlstm_chunks(tb, False, *chunk_refs, next_qk)
    assert 2 * B_QK // B_CONV_COLS == B_HEADS
    _mlstm_conv(tb, lambda: ub[1 - slot], *conv_refs,
                between=lambda blk: _mlstm_post_head(blk, tb, lambda: ub[slot], *post_refs))
    y_ref[0] = x_ref[0] + _dot(gated[...], wout_ref[...])


def _mlstm_layer(x, hm, gain, wqk, wv, wo, wz, wg, conv_w, conv_b, bias_i, bias_f, h_gain, w_out, tb):
    bsz, seq, _ = x.shape
    n_tiles = seq // tb
    row_spec = pl.BlockSpec((1, tb, D_MODEL), lambda b, i: (b, i, 0))
    next_spec = pl.BlockSpec((1, tb, D_MODEL), lambda b, i: (b, jnp.minimum(i + 1, n_tiles - 1), 0))
    return pl.pallas_call(
        functools.partial(_mlstm_kernel, tb=tb),
        grid=(bsz, n_tiles),
        in_specs=[row_spec, next_spec, _const_spec((BLOCK, D_MODEL)), _const_spec((1, D_MODEL)),
                  _const_spec((D_MODEL, 2 * B_QK)), _const_spec((D_MODEL, B_W)),
                  _const_spec((D_MODEL, B_W)), _const_spec((D_MODEL, B_W)),
                  _const_spec((D_MODEL, LANES)), _const_spec((B_CONV, 2 * B_QK)),
                  _const_spec((1, 2 * B_QK)), _const_spec((B_HEADS, BLOCK)),
                  _const_spec((B_HEADS, BLOCK)), _const_spec((1, B_W)), _const_spec((B_W, D_MODEL))],
        out_specs=[row_spec, _meta_out_spec()],
        out_shape=[jax.ShapeDtypeStruct(x.shape, F32), jax.ShapeDtypeStruct((BLOCK, D_MODEL), F32)],
        scratch_shapes=[pltpu.VMEM((2, tb, D_MODEL), BF16),
                        pltpu.VMEM((tb + B_HIST, 2 * B_QK), F32), pltpu.VMEM((tb, 2 * B_QK), BF16),
                        pltpu.VMEM((tb, B_W), BF16), pltpu.VMEM((tb, B_W), F32),
                        pltpu.VMEM((tb, LANES), F32), pltpu.VMEM((tb, B_W), BF16),
                        pltpu.VMEM((B_HEADS, B_DK, B_DV + LANES), F32), pltpu.VMEM((B_HEADS, LANES), F32)],
        compiler_params=pltpu.CompilerParams(dimension_semantics=("arbitrary", "arbitrary"),
                                             vmem_limit_bytes=B_VMEM_LIMIT),
        name="mlstm_layer",
    )(x, x, hm, gain, wqk, wv, wo, wz, wg, conv_w, conv_b, bias_i, bias_f, h_gain, w_out)


def _hgrn_level_masks():
    t_idx = lax.broadcasted_iota(jnp.int32, (C_CHUNK, C_CHUNK), 0)
    s_idx = lax.broadcasted_iota(jnp.int32, (C_CHUNK, C_CHUNK), 1)
    masks = []
    for b in C_LEVELS:
        same = (t_idx // (2 * b)) == (s_idx // (2 * b))
        masks.append(same & ((t_idx % (2 * b)) >= b) & ((s_idx % (2 * b)) < b))
    return t_idx == s_idx, masks


def _hgrn_reference_rows(ab, r0, b):
    sub = lax.broadcasted_iota(jnp.int32, (8, C_K), 0)
    pieces = []
    if 2 * b >= 8:
        for p in range(C_CHUNK // (2 * b)):
            mid = r0 + 2 * b * p + b - 1
            pieces.append(jnp.broadcast_to(ab[pl.ds(mid, 1), :], (2 * b, C_K)))
    else:
        for v in range(C_CHUNK // 8):
            cand = [jnp.broadcast_to(ab[pl.ds(r0 + 8 * v + 2 * b * p + b - 1, 1), :], (8, C_K))
                    for p in range(8 // (2 * b))]
            sel = cand[-1]
            for p in reversed(range(len(cand) - 1)):
                sel = jnp.where(sub < 2 * b * (p + 1), cand[p], sel)
            pieces.append(sel)
    return jnp.concatenate(pieces, axis=0)


def _chunk_cumsum(x):
    rows, cols = x.shape
    x3 = x.reshape(rows // SUBLANES, SUBLANES, cols)
    sub = lax.broadcasted_iota(jnp.int32, x3.shape, 1)
    for s in (1, 2, 4):
        x3 = x3 + jnp.where(sub >= s, pltpu.roll(x3, s, 1), 0.0)
    x = x3.reshape(rows, cols)
    out = []
    for c in range(rows // C_CHUNK):
        run = None
        for g in range(C_CHUNK // SUBLANES):
            r = c * C_CHUNK + g * SUBLANES
            blk = x[r:r + SUBLANES]
            if run is not None:
                blk = blk + run
            out.append(blk)
            run = jnp.broadcast_to(blk[SUBLANES - 1:SUBLANES], (SUBLANES, cols))
    return jnp.concatenate(out, axis=0)


def _hgrn_rows(h, rows, meta, lb_ref, og_ref, wout_ref, proj, qb, kb, ab, vb, zb, gated, st,
               qin, kneg, kout, ob, next_piece):
    refill = _PieceRefill(proj, next_piece)

    def unpack_gate(lo):
        zb[0:rows, lo:lo + C_PIECE] = _silu(proj.cols(rows, 3 * C_K + lo, 3 * C_K + lo + C_PIECE))
        refill.done(3 * C_K + lo, 3 * C_K + lo + C_PIECE)

    def unpack_value(lo):
        vb[0:rows, lo:lo + C_PIECE] = proj.cols(rows, 2 * C_K + lo, 2 * C_K + lo + C_PIECE).astype(BF16)
        refill.done(2 * C_K + lo, 2 * C_K + lo + C_PIECE)

    def unpack_query(lo):
        qb[0:rows, lo:lo + C_PIECE] = _silu(proj.cols(rows, lo, lo + C_PIECE))
        refill.done(lo, lo + C_PIECE)

    def unpack_forget(lo):
        cs = slice(lo, lo + LANES)
        fpre = proj.cols(rows, C_K + lo, C_K + lo + LANES)
        log_lb, log_1mlb, one_mlb = lb_ref[0:1, cs], lb_ref[1:2, cs], lb_ref[2:3, cs]
        soft = jnp.log(1.0 + jnp.exp(-jnp.abs(fpre)))
        log_sig = jnp.minimum(fpre, 0.0) - soft
        kk = one_mlb * jnp.exp(log_sig - fpre)
        grow = log_1mlb + log_sig
        lf = jnp.maximum(log_lb, grow) + jnp.log(1.0 + jnp.exp(-jnp.abs(log_lb - grow)))
        if meta:
            valid = _valid_rows(rows, LANES)
            lf = jnp.where(valid, lf, 0.0)
            kk = jnp.where(valid, kk, 0.0)
        ab[0:rows, cs] = _chunk_cumsum(lf)
        kb[0:rows, cs] = kk
        refill.done(C_K + lo, C_K + lo + LANES)

    light = ([functools.partial(unpack_gate, lo) for lo in range(0, C_W, C_PIECE)]
             + [functools.partial(unpack_query, lo) for lo in range(0, C_K, C_PIECE)]
             + [functools.partial(unpack_value, lo) for lo in range(0, C_W, C_PIECE)])
    heavy = [functools.partial(unpack_forget, lo) for lo in range(0, C_K, LANES)]
    for i in range(max(len(light), len(heavy))):
        if i < len(light):
            light[i]()
            if i % 4 != 3 and i < len(heavy):
                refill.emit()
        if i < len(heavy):
            heavy[i]()
            refill.emit()

    n_chunks = rows // C_CHUNK
    a_end = ab[C_CHUNK - 1:C_CHUNK, :]
    for c in range(1, n_chunks):
        a_end = jnp.minimum(a_end, ab[(c + 1) * C_CHUNK - 1:(c + 1) * C_CHUNK, :])
    mild = jnp.min(a_end) >= -C_SAFE_DECAY

    t_idx = lax.broadcasted_iota(jnp.int32, (C_CHUNK, C_CHUNK), 0)
    s_idx = lax.broadcasted_iota(jnp.int32, (C_CHUNK, C_CHUNK), 1)
    causal = s_idx <= t_idx
    eye, masks = _hgrn_level_masks()

    def chunk(c, carry, factored, emit=None):
        r0 = c * C_CHUNK if isinstance(c, int) else pl.multiple_of(c * C_CHUNK, C_CHUNK)
        v = vb[pl.ds(r0, C_CHUNK), :]
        a_last = ab[pl.ds(r0 + C_CHUNK - 1, 1), :]
        decay = jnp.exp(a_last)
        if factored:
            q_in = qin[pl.ds(r0, C_CHUNK), :]
            k_out = kout[pl.ds(r0, C_CHUNK), :]
            q_lv, k_lv, level_masks = [q_in], [kneg[pl.ds(r0, C_CHUNK), :]], [causal]
        else:
            a_c = ab[pl.ds(r0, C_CHUNK), :]
            q = qb[pl.ds(r0, C_CHUNK), :]
            k = kb[pl.ds(r0, C_CHUNK), :]
            q_in = (q * jnp.exp(a_c)).astype(BF16)
            k_out = (k * jnp.exp(a_last - a_c)).astype(BF16)
            q_lv, k_lv, level_masks = [q.astype(BF16)], [k.astype(BF16)], [eye] + masks
            for b in C_LEVELS:
                w = jnp.exp(-jnp.abs(a_c - _hgrn_reference_rows(ab, r0, b)))
                q_lv.append((q * w).astype(BF16))
                k_lv.append((k * w).astype(BF16))
        lanes = [slice(hd * C_DK, (hd + 1) * C_DK) for hd in range(C_HEADS)]
        atts = []
        for ln in lanes:
            att = None
            for ql, kl, mk in zip(q_lv, k_lv, level_masks):
                part = jnp.where(mk, _dot_nt(ql[:, ln], kl[:, ln]), 0.0)
                att = part if att is None else att + part
            atts.append(att.astype(BF16))
        states = [st[hd] for hd in range(C_HEADS)]
        inter = [_dot_nt(q_in[:, ln], s_t.astype(BF16)) for ln, s_t in zip(lanes, states)]
        if emit is not None:
            emit()
        for hd, ln in enumerate(lanes):
            ob[pl.ds(r0, C_CHUNK), ln] = inter[hd] + _dot(atts[hd], v[:, ln])
        for hd, ln in enumerate(lanes):
            st[hd] = states[hd] * decay[:, ln] + _dot_tn(v[:, ln], k_out[:, ln])
        return carry

    @pl.when(mild)
    def _():
        arm = refill.fork()
        for lo in range(0, C_K, C_PIECE):
            cs = slice(lo, lo + C_PIECE)
            a = ab[0:rows, cs]
            k = kb[0:rows, cs]
            a_last = jnp.concatenate(
                [jnp.broadcast_to(ab[(c + 1) * C_CHUNK - 1:(c + 1) * C_CHUNK, cs], (C_CHUNK, C_PIECE))
                 for c in range(n_chunks)], axis=0)
            qin[0:rows, cs] = (qb[0:rows, cs] * jnp.exp(a)).astype(BF16)
            kneg[0:rows, cs] = (k * jnp.exp(-a)).astype(BF16)
            kout[0:rows, cs] = (k * jnp.exp(a_last - a)).astype(BF16)
        left = arm.n_pieces - len(arm.emitted)
        for c in range(n_chunks):
            share = -(-left * (c + 1) // n_chunks) - (-(-left * c // n_chunks))
            chunk(c, 0, True, functools.partial(arm.emit, share))
        arm.flush()

    @pl.when(jnp.logical_not(mild))
    def _():
        refill.fork().flush()
        lax.fori_loop(0, n_chunks, functools.partial(chunk, factored=False), 0)

    y = None
    for lo in range(0, C_W, MXU_TILE):
        for ln in (slice(lo, lo + C_DV), slice(lo + C_DV, lo + 2 * C_DV)):
            on = _rmsnorm(ob[0:rows, ln], og_ref[:, ln])
            gated[0:rows, ln] = (on * zb[0:rows, ln]).astype(BF16)
        part = _dot(gated[0:rows, lo:lo + MXU_TILE], wout_ref[lo:lo + MXU_TILE, :])
        y = part if y is None else y + part
    if meta:
        y = jnp.where(_valid_rows(rows, D_MODEL), y, 0.0)
    return h + y


def _hgrn_kernel(x_ref, xn_ref, hm_ref, gain_ref, win_ref, lb_ref, og_ref, wout_ref,
                 y_ref, hm_out_ref, ub, qb, kb, ab, vb, zb, gated, st, qin, kneg, kout, ob, *pieces, tb):
    proj = _PieceBuffers(pieces)
    n_pieces = len(pieces)
    params = (lb_ref, og_ref, wout_ref, proj, qb, kb, ab, vb, zb, gated, st, qin, kneg, kout, ob)

    @pl.when(pl.program_id(1) == 0)
    def _():
        st[...] = jnp.zeros(st.shape, F32)
        hm = hm_ref[...]
        um = _rmsnorm(hm, gain_ref[...]).astype(BF16)
        for k in range(n_pieces):
            proj.project(k, BLOCK, um, win_ref)
        hm_out_ref[...] = _hgrn_rows(hm, BLOCK, True, *params, None)
        u0 = _rmsnorm(x_ref[0], gain_ref[...]).astype(BF16)
        for k in range(n_pieces):
            proj.project(k, tb, u0, win_ref)

    ub[...] = _rmsnorm(xn_ref[0], gain_ref[...]).astype(BF16)

    def next_piece(k):
        proj.project(k, tb, ub[...], win_ref)

    y_ref[0] = _hgrn_rows(x_ref[0], tb, False, *params, next_piece)


def _hgrn_layer(x, hm, gain, w_in, lb, o_gain, w_out, tb):
    bsz, seq, _ = x.shape
    n_tiles = seq // tb
    row_spec = pl.BlockSpec((1, tb, D_MODEL), lambda b, i: (b, i, 0))
    next_spec = pl.BlockSpec((1, tb, D_MODEL), lambda b, i: (b, jnp.minimum(i + 1, n_tiles - 1), 0))
    return pl.pallas_call(
        functools.partial(_hgrn_kernel, tb=tb),
        grid=(bsz, n_tiles),
        in_specs=[row_spec, next_spec, _const_spec((BLOCK, D_MODEL)), _const_spec((1, D_MODEL)),
                  _const_spec((D_MODEL, C_IN)), _const_spec((SUBLANES, C_K)), _const_spec((1, C_W)),
                  _const_spec((C_W, D_MODEL))],
        out_specs=[row_spec, _meta_out_spec()],
        out_shape=[jax.ShapeDtypeStruct(x.shape, F32), jax.ShapeDtypeStruct((BLOCK, D_MODEL), F32)],
        scratch_shapes=[pltpu.VMEM((tb, D_MODEL), BF16),
                        pltpu.VMEM((tb, C_K), F32), pltpu.VMEM((tb, C_K), F32),
                        pltpu.VMEM((tb, C_K), F32), pltpu.VMEM((tb, C_W), BF16),
                        pltpu.VMEM((tb, C_W), F32), pltpu.VMEM((tb, C_W), BF16),
                        pltpu.VMEM((C_HEADS, C_DV, C_DK), F32),
                        pltpu.VMEM((tb, C_K), BF16), pltpu.VMEM((tb, C_K), BF16),
                        pltpu.VMEM((tb, C_K), BF16), pltpu.VMEM((tb, C_W), F32)]
                       + [pltpu.VMEM((tb, C_PIECE), F32)] * (C_IN // C_PIECE),
        compiler_params=pltpu.CompilerParams(dimension_semantics=("arbitrary", "arbitrary"),
                                             vmem_limit_bytes=VMEM_LIMIT),
        name="hgrn2_layer",
    )(x, x, hm, gain, w_in, lb, o_gain, w_out)


def _rope_tables(first, n):
    half = A_HD // 2
    inv = ROPE_THETA ** (-jnp.arange(half, dtype=F32) / half)
    pos = (jnp.arange(first, first + n) - FRONT_PAD).astype(jnp.int32)
    ang = pos.astype(F32)[:, None] * inv[None, :]
    reps = LANES // half
    sign = jnp.tile(jnp.asarray([-1.0, 1.0], F32), reps // 2)
    cos = jnp.broadcast_to(jnp.cos(ang)[:, None, :], (n, reps, half))
    sin = jnp.sin(ang)[:, None, :] * sign[None, :, None]
    return jnp.concatenate([cos, sin], axis=1).reshape(n, 2 * LANES)


def _attn_gain_rows(q_gain, k_gain):
    half = A_HD // 2
    reps = LANES // A_HD
    rows = []
    for g, scale in ((q_gain.astype(F32), A_HD ** -0.5 * LOG2E), (k_gain.astype(F32), 1.0)):
        g_rot = jnp.concatenate([g[half:], g[:half]])
        rows += [jnp.tile(g * scale, reps), jnp.tile(g_rot * scale, reps)]
    return jnp.concatenate([jnp.stack(rows), jnp.zeros((SUBLANES - len(rows), LANES), F32)], axis=0)


def _row_tile(seq):
    tb = ROW_TILE
    while seq % tb:
        tb //= 2
    assert tb >= BLOCK, "sequence length must be a multiple of the 128-token block"
    return tb


def kernel(x, meta, norm_gain, a_w_in, a_q_gain, a_k_gain, a_sinks, a_w_out, b_w_in, b_conv_w,
           b_conv_b, b_gate_bias, b_h_gain, b_w_out, c_w_in, c_gamma, c_o_gain, c_w_out):
    bsz, seq, _ = x.shape
    depth = norm_gain.shape[0]
    tab, tabm = _rope_tables(BLOCK, seq), _rope_tables(0, BLOCK)
    hm = jnp.concatenate([jnp.zeros((FRONT_PAD, D_MODEL), x.dtype), meta.astype(x.dtype)], axis=0)
    p = jax.nn.softmax(c_gamma.astype(F32), axis=0)
    lower_bounds = jnp.cumsum(p, axis=0) - p

    def pair_heads(w, axis):
        shape = w.shape[:axis] + (A_KV // 2, 2, A_GROUP, A_HD) + w.shape[axis + 1:]
        perm = list(range(len(shape)))
        perm[axis + 1], perm[axis + 2] = axis + 2, axis + 1
        return w.reshape(shape).transpose(perm).reshape(w.shape)

    h = x
    for i in range(depth):
        kind, j = i % 3, i // 3
        gain = norm_gain[i].reshape(1, D_MODEL).astype(F32)
        if kind == 0:
            w = a_w_in[j].astype(BF16)
            w_in = jnp.concatenate([pair_heads(w[:, :A_W], 1), w[:, A_W:A_W + 2 * A_KVW],
                                    pair_heads(w[:, A_W + 2 * A_KVW:], 1)], axis=1)
            h, hm = _attn_layer(h, hm, tab, tabm, _attn_gain_rows(a_q_gain[j], a_k_gain[j]), gain,
                                w_in, a_sinks[j].astype(F32) * LOG2E,
                                pair_heads(a_w_out[j].astype(BF16), 0), _row_tile(seq))
        elif kind == 1:
            w = b_w_in[j]
            o_v, o_i, o_o, o_z = 2 * B_QK, 2 * B_QK + B_W, 2 * B_QK + B_W + 2 * B_HEADS, 2 * B_QK + 2 * B_W + 2 * B_HEADS
            wg = jnp.pad(w[:, o_i:o_o], ((0, 0), (0, LANES - 2 * B_HEADS)))
            gb = b_gate_bias[j].astype(F32)
            h, hm = _mlstm_layer(
                h, hm, gain, w[:, :o_v].astype(BF16), w[:, o_v:o_i].astype(BF16),
                w[:, o_o:o_z].astype(BF16), w[:, o_z:].astype(BF16), wg.astype(BF16),
                b_conv_w[j].astype(F32), b_conv_b[j].reshape(1, 2 * B_QK).astype(F32),
                jnp.broadcast_to(gb[:B_HEADS, None], (B_HEADS, BLOCK)),
                jnp.broadcast_to(gb[B_HEADS:, None], (B_HEADS, BLOCK)),
                b_h_gain[j].reshape(1, B_W).astype(F32), b_w_out[j].astype(BF16), _row_tile(seq))
        else:
            w = c_w_in[j].astype(BF16)
            lb = lower_bounds[i]
            lb_rows = jnp.zeros((SUBLANES, C_K), F32)
            lb_rows = lb_rows.at[0].set(jnp.log(lb)).at[1].set(jnp.log1p(-lb)).at[2].set(1.0 - lb)
            h, hm = _hgrn_layer(
                h, hm, gain, w, lb_rows, c_o_gain[j].reshape(1, C_W).astype(F32),
                c_w_out[j].astype(BF16), _row_tile(seq))
    return h
```

```python
import functools

import jax
import jax.numpy as jnp
from jax import lax
from jax.experimental import pallas as pl
from jax.experimental.pallas import tpu as pltpu

F32 = jnp.float32
BF16 = jnp.bfloat16

D_MODEL = 1024
BLOCK = 128
N_META = 16
FRONT_PAD = BLOCK - N_META
EPS = 1e-6
ROPE_THETA = 10000.0
LANES = 128
SUBLANES = 8
MXU_TILE = 256
ROW_TILE = 512

A_HEADS, A_KV, A_HD = 16, 4, 64
A_GROUP = A_HEADS // A_KV
A_W = A_HEADS * A_HD
A_KVW = A_KV * A_HD
A_IN = 2 * A_W + 2 * A_KVW
A_PIECE = 2 * MXU_TILE
LOG2E = 1.4426950408889634

B_HEADS, B_DK, B_DV, B_CONV = 8, 128, 256, 4
B_QK = B_HEADS * B_DK
B_W = B_HEADS * B_DV
B_HIST = 8
B_CONV_COLS = MXU_TILE
B_QK_PIECES = 4

C_HEADS, C_DK, C_DV, C_CHUNK = 8, 128, 128, 64
C_K = C_HEADS * C_DK
C_W = C_HEADS * C_DV
C_IN = 2 * C_K + 2 * C_W
C_PIECE = MXU_TILE
C_LEVELS = (1, 2, 4, 8, 16, 32)
C_SAFE_DECAY = 60.0

VMEM_LIMIT = 60 * 1024 * 1024
B_VMEM_LIMIT = 62 * 1024 * 1024


def _sigmoid(x):
    return 0.5 + 0.5 * jnp.tanh(0.5 * x)


def _silu(x):
    half = 0.5 * x
    return half + half * jnp.tanh(half)


def _log_sigmoid(x):
    return jnp.minimum(x, 0.0) - jnp.log(1.0 + jnp.exp(-jnp.abs(x)))


def _rmsnorm(x, g):
    ms = jnp.mean(x * x, axis=-1, keepdims=True)
    return x * lax.rsqrt(ms + EPS) * g


def _dot(a, b):
    return jnp.dot(a, b, preferred_element_type=F32)


def _dot_nt(a, b):
    return lax.dot_general(a, b, (((1,), (1,)), ((), ())), preferred_element_type=F32)


def _dot_tn(a, b):
    return lax.dot_general(a, b, (((0,), (0,)), ((), ())), preferred_element_type=F32)


def _valid_rows(rows, cols):
    return lax.broadcasted_iota(jnp.int32, (rows, cols), 0) >= FRONT_PAD


class _PieceBuffers:
    def __init__(self, refs):
        self.refs = refs
        self.piece = refs[0].shape[1]

    def cols(self, rows, lo, hi):
        k = lo // self.piece
        assert hi <= (k + 1) * self.piece, "a read must stay inside one piece"
        return self.refs[k][0:rows, lo - k * self.piece:hi - k * self.piece]

    def project(self, k, rows, u, w_ref):
        self.refs[k][0:rows, :] = _dot(u, w_ref[:, k * self.piece:(k + 1) * self.piece])


class _PieceRefill:
    def __init__(self, proj, next_piece):
        self.next_piece = next_piece
        self.n_pieces = len(proj.refs) if next_piece is not None else 0
        self.groups = proj.piece // LANES
        self.consumed, self.emitted = set(), set()

    def done(self, lo, hi):
        assert lo % LANES == 0 and hi % LANES == 0
        self.consumed.update(range(lo // LANES, hi // LANES))

    def emit(self, n=1):
        for k in range(self.n_pieces):
            ready = all(k * self.groups + g in self.consumed for g in range(self.groups))
            if n > 0 and ready and k not in self.emitted:
                self.emitted.add(k)
                self.next_piece(k)
                n -= 1

    def flush(self):
        self.emit(self.n_pieces)

    def fork(self):
        other = _PieceRefill.__new__(_PieceRefill)
        other.__dict__.update(self.__dict__)
        other.consumed, other.emitted = set(self.consumed), set(self.emitted)
        return other


def _attn_rows(h, rows, meta, tab, gains_ref, sink_ref, wout_ref, proj, qbuf, klo, khi, vt, obuf, gz, next_piece):
    lane = lax.broadcasted_iota(jnp.int32, (rows, LANES), 1)
    low_head = lane < A_HD
    first_half = (lane % A_HD) < (A_HD // 2)

    def norm_rope(x, ctab, stab):
        sq = x * x
        s_lo = jnp.sum(jnp.where(low_head, sq, 0.0), axis=-1, keepdims=True)
        s_hi = jnp.sum(jnp.where(low_head, 0.0, sq), axis=-1, keepdims=True)
        r = jnp.where(low_head, lax.rsqrt(s_lo / A_HD + EPS), lax.rsqrt(s_hi / A_HD + EPS))
        rot = jnp.where(first_half, pltpu.roll(x, LANES - A_HD // 2, 1), pltpu.roll(x, A_HD // 2, 1))
        return (x * ctab + rot * stab) * r

    refill = _PieceRefill(proj, next_piece)
    z0 = A_W + 2 * A_KVW
    cos, sin = tab[:, 0:LANES], tab[:, LANES:2 * LANES]
    q_cos, q_sin = cos * gains_ref[0:1, :], sin * gains_ref[1:2, :]
    k_cos, k_sin = cos * gains_ref[2:3, :], sin * gains_ref[3:4, :]
    for lo in range(0, A_W, A_PIECE):
        for p in range(lo // LANES, (lo + A_PIECE) // LANES):
            x = proj.cols(rows, p * LANES, (p + 1) * LANES)
            qbuf[0:rows, p * LANES:(p + 1) * LANES] = norm_rope(x, q_cos, q_sin).astype(BF16)
        refill.done(lo, lo + A_PIECE)
        refill.emit()
        gz[0:rows, lo:lo + A_PIECE] = _silu(proj.cols(rows, z0 + lo, z0 + lo + A_PIECE))
        refill.done(z0 + lo, z0 + lo + A_PIECE)
    for m in range(A_KVW // LANES):
        k = norm_rope(proj.cols(rows, A_W + m * LANES, A_W + (m + 1) * LANES), k_cos, k_sin)
        klo[BLOCK:BLOCK + rows, m * LANES:(m + 1) * LANES] = jnp.where(low_head, k, 0.0).astype(BF16)
        khi[BLOCK:BLOCK + rows, m * LANES:(m + 1) * LANES] = jnp.where(low_head, 0.0, k).astype(BF16)
    refill.done(A_W, A_W + A_KVW)
    refill.emit()
    for m in range(A_KVW // LANES):
        v = proj.cols(rows, A_W + A_KVW + m * LANES, A_W + A_KVW + (m + 1) * LANES)
        for n in range(rows // BLOCK):
            vt[m * LANES:(m + 1) * LANES, (n + 1) * BLOCK:(n + 2) * BLOCK] = (
                v[n * BLOCK:(n + 1) * BLOCK].T.astype(BF16))
    refill.done(A_W + A_KVW, z0)
    refill.emit()

    c_idx = lax.broadcasted_iota(jnp.int32, (BLOCK, BLOCK), 0)
    t_idx = lax.broadcasted_iota(jnp.int32, (BLOCK, BLOCK), 1)
    own = c_idx <= t_idx
    first_valid = FRONT_PAD if meta else jnp.where(pl.program_id(1) == 0, FRONT_PAD, 0)

    def fold(tile, first):
        prev, cur = tile[0:BLOCK], tile[BLOCK:2 * BLOCK]
        if meta:
            return jnp.where(own & (c_idx >= first_valid), cur, -jnp.inf)
        if first:
            prev = jnp.where(c_idx >= first_valid, prev, -jnp.inf)
        return jnp.where(own, cur, prev)

    def scores(n):
        r0 = n * BLOCK
        out = []
        for m in range(A_KVW // LANES):
            keys = jnp.concatenate([klo[r0:r0 + 2 * BLOCK, m * LANES:(m + 1) * LANES],
                                    khi[r0:r0 + 2 * BLOCK, m * LANES:(m + 1) * LANES]], axis=0)
            qs = jnp.concatenate([qbuf[r0:r0 + BLOCK, (A_GROUP * m + g) * LANES:(A_GROUP * m + g + 1) * LANES]
                                  for g in range(A_GROUP)], axis=0)
            out.append(_dot_nt(keys, qs))
        return out

    n_blocks = rows // BLOCK
    pending = scores(0)
    for n in range(n_blocks):
        r0 = n * BLOCK
        s_now = pending
        if n + 1 < n_blocks:
            pending = scores(n + 1)
        refill.emit()
        for m in range(A_KVW // LANES):
            outs = []
            for half in range(2):
                j = 2 * m + half
                es, inv = [], []
                for g in range(A_GROUP):
                    tile = fold(s_now[m][half * 2 * BLOCK:(half + 1) * 2 * BLOCK, g * BLOCK:(g + 1) * BLOCK], n == 0)
                    sink = sink_ref[A_GROUP * j + g]
                    mx = jnp.maximum(jnp.max(tile, axis=0, keepdims=True), sink)
                    e = jnp.exp2(tile - mx)
                    inv.append(1.0 / (jnp.sum(e, axis=0, keepdims=True) + jnp.exp2(sink - mx)))
                    es.append(jnp.concatenate([jnp.where(own, 0.0, e), jnp.where(own, e, 0.0)],
                                              axis=0).astype(BF16))
                o_t = _dot(vt[j * A_HD:(j + 1) * A_HD, r0:r0 + 2 * BLOCK], jnp.concatenate(es, axis=1))
                outs.append(o_t * jnp.concatenate(inv, axis=1))
            for g in range(A_GROUP):
                pair = jnp.concatenate([outs[0][:, g * BLOCK:(g + 1) * BLOCK],
                                        outs[1][:, g * BLOCK:(g + 1) * BLOCK]], axis=0)
                p = A_GROUP * m + g
                obuf[r0:r0 + BLOCK, p * LANES:(p + 1) * LANES] = pair.T

    refill.flush()
    klo[0:BLOCK, :] = klo[rows:rows + BLOCK, :]
    khi[0:BLOCK, :] = khi[rows:rows + BLOCK, :]
    vt[:, 0:BLOCK] = vt[:, rows:rows + BLOCK]

    y = _dot((obuf[0:rows, :] * gz[0:rows, :]).astype(BF16), wout_ref[...])
    if meta:
        y = jnp.where(_valid_rows(rows, D_MODEL), y, 0.0)
    return h + y


def _attn_kernel(x_ref, xn_ref, hm_ref, tab_ref, tabm_ref, gains_ref, gain_ref, win_ref, sink_ref, wout_ref,
                 y_ref, hm_out_ref, qbuf, klo, khi, vt, obuf, gz, ub, *pieces, tb):
    proj = _PieceBuffers(pieces)
    n_pieces = len(pieces)
    params = (gains_ref, sink_ref, wout_ref, proj, qbuf, klo, khi, vt, obuf, gz)

    @pl.when(pl.program_id(1) == 0)
    def _():
        klo[0:BLOCK, :] = jnp.zeros((BLOCK, A_KVW), BF16)
        khi[0:BLOCK, :] = jnp.zeros((BLOCK, A_KVW), BF16)
        vt[:, 0:BLOCK] = jnp.zeros((A_KVW, BLOCK), BF16)
        hm = hm_ref[...]
        um = _rmsnorm(hm, gain_ref[...]).astype(BF16)
        for k in range(n_pieces):
            proj.project(k, BLOCK, um, win_ref)
        hm_out_ref[...] = _attn_rows(hm, BLOCK, True, tabm_ref[...], *params, None)
        u0 = _rmsnorm(x_ref[0], gain_ref[...]).astype(BF16)
        for k in range(n_pieces):
            proj.project(k, tb, u0, win_ref)

    ub[...] = _rmsnorm(xn_ref[0], gain_ref[...]).astype(BF16)

    def next_piece(k):
        proj.project(k, tb, ub[...], win_ref)

    y_ref[0] = _attn_rows(x_ref[0], tb, False, tab_ref[...], *params, next_piece)


def _const_spec(shape):
    zeros = (0,) * len(shape)
    return pl.BlockSpec(shape, lambda b, i: zeros, pipeline_mode=pl.Buffered(1))


def _meta_out_spec():
    return pl.BlockSpec((BLOCK, D_MODEL), lambda b, i: (0, 0))


def _attn_layer(x, hm, tab, tabm, gains, gain, w_in, sinks, w_out, tb):
    bsz, seq, _ = x.shape
    n_tiles = seq // tb
    row_spec = pl.BlockSpec((1, tb, D_MODEL), lambda b, i: (b, i, 0))
    next_spec = pl.BlockSpec((1, tb, D_MODEL), lambda b, i: (b, jnp.minimum(i + 1, n_tiles - 1), 0))
    return pl.pallas_call(
        functools.partial(_attn_kernel, tb=tb),
        grid=(bsz, n_tiles),
        in_specs=[row_spec, next_spec, _const_spec((BLOCK, D_MODEL)),
                  pl.BlockSpec((tb, 2 * LANES), lambda b, i: (i, 0)), _const_spec((BLOCK, 2 * LANES)),
                  _const_spec((SUBLANES, LANES)), _const_spec((1, D_MODEL)), _const_spec((D_MODEL, A_IN)),
                  pl.BlockSpec(memory_space=pltpu.SMEM), _const_spec((A_W, D_MODEL))],
        out_specs=[row_spec, _meta_out_spec()],
        out_shape=[jax.ShapeDtypeStruct(x.shape, F32), jax.ShapeDtypeStruct((BLOCK, D_MODEL), F32)],
        scratch_shapes=[pltpu.VMEM((tb, A_W), BF16),
                        pltpu.VMEM((tb + BLOCK, A_KVW), BF16), pltpu.VMEM((tb + BLOCK, A_KVW), BF16),
                        pltpu.VMEM((A_KVW, tb + BLOCK), BF16), pltpu.VMEM((tb, A_W), F32),
                        pltpu.VMEM((tb, A_W), F32), pltpu.VMEM((tb, D_MODEL), BF16)]
                       + [pltpu.VMEM((tb, A_PIECE), F32)] * (A_IN // A_PIECE),
        compiler_params=pltpu.CompilerParams(dimension_semantics=("arbitrary", "arbitrary"),
                                             vmem_limit_bytes=VMEM_LIMIT),
        name="swa_layer",
    )(x, x, hm, tab, tabm, gains, gain, w_in, sinks, w_out)


def _mlstm_conv(rows, u, wv_ref, wg_ref, cw_ref, cb_ref, qkraw, qs, ks, vs, gates, between=None):
    n_blk = 2 * B_QK // B_CONV_COLS
    v_piece = B_W // B_QK_PIECES
    after_block = [[] for _ in range(n_blk)]
    for k in range(B_QK_PIECES):
        after_block[k * n_blk // B_QK_PIECES].append(("v", k))
    after_block[n_blk - 1].append(("g", 0))

    for blk in range(n_blk):
        cs = slice(blk * B_CONV_COLS, (blk + 1) * B_CONV_COLS)
        acc = cb_ref[:, cs]
        for j in range(B_CONV):
            start = B_HIST - (B_CONV - 1) + j
            acc = acc + qkraw[start:start + rows, cs] * cw_ref[j:j + 1, cs]
        act = _silu(acc)
        if blk * B_CONV_COLS < B_QK:
            qs[0:rows, cs] = act.astype(BF16)
        else:
            ks[0:rows, blk * B_CONV_COLS - B_QK:(blk + 1) * B_CONV_COLS - B_QK] = (
                act * (B_DK ** -0.5)).astype(BF16)
        qkraw[0:B_HIST, cs] = qkraw[rows:rows + B_HIST, cs]
        for kind, k in after_block[blk]:
            if kind == "v":
                vs[0:rows, k * v_piece:(k + 1) * v_piece] = _dot(
                    u(), wv_ref[:, k * v_piece:(k + 1) * v_piece]).astype(BF16)
            else:
                gates[0:rows, :] = _dot(u(), wg_ref[...])
        if between is not None:
            between(blk)


def _mlstm_chunks(rows, meta, bi_ref, bf_ref, qs, ks, vs, hb, gates, cst, mst, next_qk):
    t_idx = lax.broadcasted_iota(jnp.int32, (BLOCK, BLOCK), 0)
    s_idx = lax.broadcasted_iota(jnp.int32, (BLOCK, BLOCK), 1)
    causal = s_idx <= t_idx
    lane8 = lax.broadcasted_iota(jnp.int32, (B_HEADS, BLOCK), 1)

    n_chunks = rows // BLOCK

    def chunk(c):
        r0 = c * BLOCK
        g_t = gates[pl.ds(r0, BLOCK), :].T
        li = g_t[0:B_HEADS] + bi_ref[...]
        lf = _log_sigmoid(g_t[B_HEADS:2 * B_HEADS] + bf_ref[...])
        if meta:
            li = jnp.where(lane8 >= FRONT_PAD, li, -jnp.inf)
            lf = jnp.where(lane8 >= FRONT_PAD, lf, 0.0)
        a = lf
        shift = 1
        while shift < BLOCK:
            a = a + jnp.where(lane8 >= shift, pltpu.roll(a, shift, 1), 0.0)
            shift *= 2
        run = li - a
        shift = 1
        while shift < BLOCK:
            run = jnp.maximum(run, jnp.where(lane8 >= shift, pltpu.roll(run, shift, 1), -jnp.inf))
            shift *= 2
        m_prev = mst[:, 0:1]
        decay_max = jnp.maximum(m_prev, run)
        m_t = a + decay_max
        g_tot = a[:, BLOCK - 1:BLOCK]
        uu = g_tot - a + li
        m_new = jnp.maximum(g_tot + m_prev, jnp.max(uu, axis=1, keepdims=True))
        w_old = jnp.exp(g_tot + m_prev - m_new)
        w_s = jnp.exp(uu - m_new)
        b_row = li - a
        mst[...] = jnp.broadcast_to(m_new, (B_HEADS, LANES))

        def col_tile(row):
            return jnp.broadcast_to(row, (BLOCK, BLOCK)).T

        heads = range(B_HEADS)
        ones = jnp.ones((BLOCK, LANES), BF16)
        qh = [qs[pl.ds(r0, BLOCK), hd * B_DK:(hd + 1) * B_DK] for hd in heads]
        kh = [ks[pl.ds(r0, BLOCK), hd * B_DK:(hd + 1) * B_DK] for hd in heads]
        vh = [jnp.concatenate([vs[pl.ds(r0, BLOCK), hd * B_DV:(hd + 1) * B_DV], ones], axis=1) for hd in heads]
        c_prev = [cst[hd] for hd in heads]
        qk = [_dot_nt(qh[hd], kh[hd]) for hd in heads]
        inter_c = [_dot(qh[hd], c_prev[hd].astype(BF16)) for hd in heads]
        if next_qk is not None:
            for k in range(B_QK_PIECES):
                if k * n_chunks // B_QK_PIECES == c:
                    next_qk(k)

        sc, kw, w_inter, floor = [], [], [], []
        for hd in heads:
            dm_t = col_tile(decay_max[hd:hd + 1, :])
            w = jnp.exp(jnp.where(causal, b_row[hd:hd + 1, :] - dm_t, -jnp.inf))
            sc.append((qk[hd] * w).astype(BF16))
            kw.append((kh[hd].astype(F32) * col_tile(w_s[hd:hd + 1, :])).astype(BF16))
            w_inter.append(jnp.exp(m_prev[hd:hd + 1, :] - dm_t))
            floor.append(jnp.exp(-col_tile(m_t[hd:hd + 1, :])))

        pv = [_dot(sc[hd], vh[hd]) for hd in heads]
        kv = [_dot_tn(kw[hd], vh[hd]) for hd in heads]
        for hd in heads:
            den = w_inter[hd] * inter_c[hd][:, B_DV:] + pv[hd][:, B_DV:]
            scale = 1.0 / jnp.maximum(jnp.abs(den), floor[hd])
            for half in range(B_DV // LANES):
                ls = slice(half * LANES, (half + 1) * LANES)
                hb[pl.ds(r0, BLOCK), hd * B_DV + half * LANES:hd * B_DV + (half + 1) * LANES] = (
                    (w_inter[hd] * inter_c[hd][:, ls] + pv[hd][:, ls]) * scale)
            cst[hd] = w_old[hd:hd + 1, :] * c_prev[hd] + kv[hd]

    for c in range(n_chunks):
        chunk(c)


def _mlstm_post_head(hd, rows, u, wo_ref, wz_ref, hg_ref, hb, gated):
    cs = slice(hd * B_DV, (hd + 1) * B_DV)
    hh = _sigmoid(_dot(u(), wo_ref[:, cs])) * hb[0:rows, cs]
    gated[0:rows, cs] = (_rmsnorm(hh, hg_ref[:, cs]) * _silu(_dot(u(), wz_ref[:, cs]))).astype(BF16)


def _mlstm_kernel(x_ref, xn_ref, hm_ref, gain_ref, wqk_ref, wv_ref, wo_ref, wz_ref, wg_ref, cw_ref, cb_ref,
                  bi_ref, bf_ref, hg_ref, wout_ref, y_ref, hm_out_ref,
                  ub, qkraw, qkg, vs, hb, gates, gated, cst, mst, *, tb):
    qs, ks = qkg.at[:, 0:B_QK], qkg.at[:, B_QK:2 * B_QK]
    conv_refs = (wv_ref, wg_ref, cw_ref, cb_ref, qkraw, qs, ks, vs, gates)
    chunk_refs = (bi_ref, bf_ref, qs, ks, vs, hb, gates, cst, mst)
    post_refs = (wo_ref, wz_ref, hg_ref, hb, gated)
    slot = pl.program_id(1) % 2

    @pl.when(pl.program_id(1) == 0)
    def _():
        qkraw[0:B_HIST, :] = jnp.zeros((B_HIST, 2 * B_QK), F32)
        cst[...] = jnp.zeros(cst.shape, F32)
        mst[...] = jnp.zeros(mst.shape, F32)
        hm = hm_ref[...]
        ub[1, 0:BLOCK, :] = _rmsnorm(hm, gain_ref[...]).astype(BF16)
        um = lambda: ub[1, 0:BLOCK, :]
        qkraw[B_HIST:B_HIST + BLOCK, :] = _dot(um(), wqk_ref[...])
        _mlstm_conv(BLOCK, um, *conv_refs)
        _mlstm_chunks(BLOCK, True, *chunk_refs, None)
        for hd in range(B_HEADS):
            _mlstm_post_head(hd, BLOCK, um, *post_refs)
        ym = _dot(gated[0:BLOCK, :], wout_ref[...])
        hm_out_ref[...] = hm + jnp.where(_valid_rows(BLOCK, D_MODEL), ym, 0.0)
        ub[0] = _rmsnorm(x_ref[0], gain_ref[...]).astype(BF16)
        qkraw[B_HIST:B_HIST + tb, :] = _dot(ub[0], wqk_ref[...])
        _mlstm_conv(tb, lambda: ub[0], *conv_refs)

    ub[1 - slot] = _rmsnorm(xn_ref[0], gain_ref[...]).astype(BF16)
    qk_piece = 2 * B_QK // B_QK_PIECES

    def next_qk(k):
        cols = slice(k * qk_piece, (k + 1) * qk_piece)
        qkraw[B_HIST:B_HIST + tb, cols] = _dot(ub[1 - slot], wqk_ref[:, cols])

    _mlstm_chunks(tb, False, *chunk_refs, next_qk)
    per_head = 2 * B_QK // B_CONV_COLS // B_HEADS

    def between(blk):
        if blk % per_head == per_head - 1:
            _mlstm_post_head(blk // per_head, tb, lambda: ub[slot], *post_refs)

    _mlstm_conv(tb, lambda: ub[1 - slot], *conv_refs, between=between)
    y_ref[0] = x_ref[0] + _dot(gated[...], wout_ref[...])


def _mlstm_layer(x, hm, gain, wqk, wv, wo, wz, wg, conv_w, conv_b, bias_i, bias_f, h_gain, w_out, tb):
    bsz, seq, _ = x.shape
    n_tiles = seq // tb
    row_spec = pl.BlockSpec((1, tb, D_MODEL), lambda b, i: (b, i, 0))
    next_spec = pl.BlockSpec((1, tb, D_MODEL), lambda b, i: (b, jnp.minimum(i + 1, n_tiles - 1), 0))
    return pl.pallas_call(
        functools.partial(_mlstm_kernel, tb=tb),
        grid=(bsz, n_tiles),
        in_specs=[row_spec, next_spec, _const_spec((BLOCK, D_MODEL)), _const_spec((1, D_MODEL)),
                  _const_spec((D_MODEL, 2 * B_QK)), _const_spec((D_MODEL, B_W)),
                  _const_spec((D_MODEL, B_W)), _const_spec((D_MODEL, B_W)),
                  _const_spec((D_MODEL, LANES)), _const_spec((B_CONV, 2 * B_QK)),
                  _const_spec((1, 2 * B_QK)), _const_spec((B_HEADS, BLOCK)),
                  _const_spec((B_HEADS, BLOCK)), _const_spec((1, B_W)), _const_spec((B_W, D_MODEL))],
        out_specs=[row_spec, _meta_out_spec()],
        out_shape=[jax.ShapeDtypeStruct(x.shape, F32), jax.ShapeDtypeStruct((BLOCK, D_MODEL), F32)],
        scratch_shapes=[pltpu.VMEM((2, tb, D_MODEL), BF16),
                        pltpu.VMEM((tb + B_HIST, 2 * B_QK), F32), pltpu.VMEM((tb, 2 * B_QK), BF16),
                        pltpu.VMEM((tb, B_W), BF16), pltpu.VMEM((tb, B_W), F32),
                        pltpu.VMEM((tb, LANES), F32), pltpu.VMEM((tb, B_W), BF16),
                        pltpu.VMEM((B_HEADS, B_DK, B_DV + LANES), F32), pltpu.VMEM((B_HEADS, LANES), F32)],
        compiler_params=pltpu.CompilerParams(dimension_semantics=("arbitrary", "arbitrary"),
                                             vmem_limit_bytes=B_VMEM_LIMIT),
        name="mlstm_layer",
    )(x, x, hm, gain, wqk, wv, wo, wz, wg, conv_w, conv_b, bias_i, bias_f, h_gain, w_out)


def _hgrn_level_masks():
    t_idx = lax.broadcasted_iota(jnp.int32, (C_CHUNK, C_CHUNK), 0)
    s_idx = lax.broadcasted_iota(jnp.int32, (C_CHUNK, C_CHUNK), 1)
    masks = []
    for b in C_LEVELS:
        same = (t_idx // (2 * b)) == (s_idx // (2 * b))
        masks.append(same & ((t_idx % (2 * b)) >= b) & ((s_idx % (2 * b)) < b))
    return t_idx == s_idx, masks


def _hgrn_reference_rows(ab, r0, b):
    sub = lax.broadcasted_iota(jnp.int32, (8, C_K), 0)
    pieces = []
    if 2 * b >= 8:
        for p in range(C_CHUNK // (2 * b)):
            mid = r0 + 2 * b * p + b - 1
            pieces.append(jnp.broadcast_to(ab[pl.ds(mid, 1), :], (2 * b, C_K)))
    else:
        for v in range(C_CHUNK // 8):
            cand = [jnp.broadcast_to(ab[pl.ds(r0 + 8 * v + 2 * b * p + b - 1, 1), :], (8, C_K))
                    for p in range(8 // (2 * b))]
            sel = cand[-1]
            for p in reversed(range(len(cand) - 1)):
                sel = jnp.where(sub < 2 * b * (p + 1), cand[p], sel)
            pieces.append(sel)
    return jnp.concatenate(pieces, axis=0)


def _chunk_cumsum(x):
    rows, cols = x.shape
    x3 = x.reshape(rows // SUBLANES, SUBLANES, cols)
    sub = lax.broadcasted_iota(jnp.int32, x3.shape, 1)
    for s in (1, 2, 4):
        x3 = x3 + jnp.where(sub >= s, pltpu.roll(x3, s, 1), 0.0)
    x = x3.reshape(rows, cols)
    out = []
    for c in range(rows // C_CHUNK):
        run = None
        for g in range(C_CHUNK // SUBLANES):
            r = c * C_CHUNK + g * SUBLANES
            blk = x[r:r + SUBLANES]
            if run is not None:
                blk = blk + run
            out.append(blk)
            run = jnp.broadcast_to(blk[SUBLANES - 1:SUBLANES], (SUBLANES, cols))
    return jnp.concatenate(out, axis=0)


def _hgrn_rows(h, rows, meta, lb_ref, og_ref, wout_ref, proj, qb, kb, ab, vb, zb, gated, st,
               qin, kneg, kout, ob, next_piece):
    refill = _PieceRefill(proj, next_piece)

    def unpack_gate(lo):
        zb[0:rows, lo:lo + C_PIECE] = _silu(proj.cols(rows, 3 * C_K + lo, 3 * C_K + lo + C_PIECE))
        refill.done(3 * C_K + lo, 3 * C_K + lo + C_PIECE)

    def unpack_value(lo):
        vb[0:rows, lo:lo + C_PIECE] = proj.cols(rows, 2 * C_K + lo, 2 * C_K + lo + C_PIECE).astype(BF16)
        refill.done(2 * C_K + lo, 2 * C_K + lo + C_PIECE)

    def unpack_query(lo):
        qb[0:rows, lo:lo + C_PIECE] = _silu(proj.cols(rows, lo, lo + C_PIECE))
        refill.done(lo, lo + C_PIECE)

    def unpack_forget(lo):
        cs = slice(lo, lo + LANES)
        fpre = proj.cols(rows, C_K + lo, C_K + lo + LANES)
        log_lb, log_1mlb, one_mlb = lb_ref[0:1, cs], lb_ref[1:2, cs], lb_ref[2:3, cs]
        soft = jnp.log(1.0 + jnp.exp(-jnp.abs(fpre)))
        log_sig = jnp.minimum(fpre, 0.0) - soft
        kk = one_mlb * jnp.exp(log_sig - fpre)
        grow = log_1mlb + log_sig
        lf = jnp.maximum(log_lb, grow) + jnp.log(1.0 + jnp.exp(-jnp.abs(log_lb - grow)))
        if meta:
            valid = _valid_rows(rows, LANES)
            lf = jnp.where(valid, lf, 0.0)
            kk = jnp.where(valid, kk, 0.0)
        ab[0:rows, cs] = _chunk_cumsum(lf)
        kb[0:rows, cs] = kk
        refill.done(C_K + lo, C_K + lo + LANES)

    light = ([functools.partial(unpack_gate, lo) for lo in range(0, C_W, C_PIECE)]
             + [functools.partial(unpack_query, lo) for lo in range(0, C_K, C_PIECE)]
             + [functools.partial(unpack_value, lo) for lo in range(0, C_W, C_PIECE)])
    heavy = [functools.partial(unpack_forget, lo) for lo in range(0, C_K, LANES)]
    for i in range(max(len(light), len(heavy))):
        if i < len(light):
            light[i]()
            if i % 4 != 3 and i < len(heavy):
                refill.emit()
        if i < len(heavy):
            heavy[i]()
            refill.emit()

    n_chunks = rows // C_CHUNK
    a_end = ab[C_CHUNK - 1:C_CHUNK, :]
    for c in range(1, n_chunks):
        a_end = jnp.minimum(a_end, ab[(c + 1) * C_CHUNK - 1:(c + 1) * C_CHUNK, :])
    mild = jnp.min(a_end) >= -C_SAFE_DECAY

    t_idx = lax.broadcasted_iota(jnp.int32, (C_CHUNK, C_CHUNK), 0)
    s_idx = lax.broadcasted_iota(jnp.int32, (C_CHUNK, C_CHUNK), 1)
    causal = s_idx <= t_idx
    eye, masks = _hgrn_level_masks()

    def chunk(c, carry, factored, emit=None):
        r0 = c * C_CHUNK if isinstance(c, int) else pl.multiple_of(c * C_CHUNK, C_CHUNK)
        v = vb[pl.ds(r0, C_CHUNK), :]
        a_last = ab[pl.ds(r0 + C_CHUNK - 1, 1), :]
        decay = jnp.exp(a_last)
        if factored:
            q_in = qin[pl.ds(r0, C_CHUNK), :]
            k_out = kout[pl.ds(r0, C_CHUNK), :]
            q_lv, k_lv, level_masks = [q_in], [kneg[pl.ds(r0, C_CHUNK), :]], [causal]
        else:
            a_c = ab[pl.ds(r0, C_CHUNK), :]
            q = qb[pl.ds(r0, C_CHUNK), :]
            k = kb[pl.ds(r0, C_CHUNK), :]
            q_in = (q * jnp.exp(a_c)).astype(BF16)
            k_out = (k * jnp.exp(a_last - a_c)).astype(BF16)
            q_lv, k_lv, level_masks = [q.astype(BF16)], [k.astype(BF16)], [eye] + masks
            for b in C_LEVELS:
                w = jnp.exp(-jnp.abs(a_c - _hgrn_reference_rows(ab, r0, b)))
                q_lv.append((q * w).astype(BF16))
                k_lv.append((k * w).astype(BF16))
        lanes = [slice(hd * C_DK, (hd + 1) * C_DK) for hd in range(C_HEADS)]
        atts = []
        for ln in lanes:
            att = None
            for ql, kl, mk in zip(q_lv, k_lv, level_masks):
                part = jnp.where(mk, _dot_nt(ql[:, ln], kl[:, ln]), 0.0)
                att = part if att is None else att + part
            atts.append(att.astype(BF16))
        states = [st[hd] for hd in range(C_HEADS)]
        inter = [_dot_nt(q_in[:, ln], s_t.astype(BF16)) for ln, s_t in zip(lanes, states)]
        if emit is not None:
            emit()
        for hd, ln in enumerate(lanes):
            ob[pl.ds(r0, C_CHUNK), ln] = inter[hd] + _dot(atts[hd], v[:, ln])
        for hd, ln in enumerate(lanes):
            st[hd] = states[hd] * decay[:, ln] + _dot_tn(v[:, ln], k_out[:, ln])
        return carry

    @pl.when(mild)
    def _():
        arm = refill.fork()
        for lo in range(0, C_K, C_PIECE):
            cs = slice(lo, lo + C_PIECE)
            a = ab[0:rows, cs]
            k = kb[0:rows, cs]
            a_last = jnp.concatenate(
                [jnp.broadcast_to(ab[(c + 1) * C_CHUNK - 1:(c + 1) * C_CHUNK, cs], (C_CHUNK, C_PIECE))
                 for c in range(n_chunks)], axis=0)
            qin[0:rows, cs] = (qb[0:rows, cs] * jnp.exp(a)).astype(BF16)
            kneg[0:rows, cs] = (k * jnp.exp(-a)).astype(BF16)
            kout[0:rows, cs] = (k * jnp.exp(a_last - a)).astype(BF16)
        left = arm.n_pieces - len(arm.emitted)
        for c in range(n_chunks):
            share = -(-left * (c + 1) // n_chunks) - (-(-left * c // n_chunks))
            chunk(c, 0, True, functools.partial(arm.emit, share))
        arm.flush()

    @pl.when(jnp.logical_not(mild))
    def _():
        refill.fork().flush()
        lax.fori_loop(0, n_chunks, functools.partial(chunk, factored=False), 0)

    y = None
    for lo in range(0, C_W, MXU_TILE):
        for ln in (slice(lo, lo + C_DV), slice(lo + C_DV, lo + 2 * C_DV)):
            on = _rmsnorm(ob[0:rows, ln], og_ref[:, ln])
            gated[0:rows, ln] = (on * zb[0:rows, ln]).astype(BF16)
        part = _dot(gated[0:rows, lo:lo + MXU_TILE], wout_ref[lo:lo + MXU_TILE, :])
        y = part if y is None else y + part
    if meta:
        y = jnp.where(_valid_rows(rows, D_MODEL), y, 0.0)
    return h + y


def _hgrn_kernel(x_ref, xn_ref, hm_ref, gain_ref, win_ref, lb_ref, og_ref, wout_ref,
                 y_ref, hm_out_ref, ub, qb, kb, ab, vb, zb, gated, st, qin, kneg, kout, ob, *pieces, tb):
    proj = _PieceBuffers(pieces)
    n_pieces = len(pieces)
    params = (lb_ref, og_ref, wout_ref, proj, qb, kb, ab, vb, zb, gated, st, qin, kneg, kout, ob)

    @pl.when(pl.program_id(1) == 0)
    def _():
        st[...] = jnp.zeros(st.shape, F32)
        hm = hm_ref[...]
        um = _rmsnorm(hm, gain_ref[...]).astype(BF16)
        for k in range(n_pieces):
            proj.project(k, BLOCK, um, win_ref)
        hm_out_ref[...] = _hgrn_rows(hm, BLOCK, True, *params, None)
        u0 = _rmsnorm(x_ref[0], gain_ref[...]).astype(BF16)
        for k in range(n_pieces):
            proj.project(k, tb, u0, win_ref)

    ub[...] = _rmsnorm(xn_ref[0], gain_ref[...]).astype(BF16)

    def next_piece(k):
        proj.project(k, tb, ub[...], win_ref)

    y_ref[0] = _hgrn_rows(x_ref[0], tb, False, *params, next_piece)


def _hgrn_layer(x, hm, gain, w_in, lb, o_gain, w_out, tb):
    bsz, seq, _ = x.shape
    n_tiles = seq // tb
    row_spec = pl.BlockSpec((1, tb, D_MODEL), lambda b, i: (b, i, 0))
    next_spec = pl.BlockSpec((1, tb, D_MODEL), lambda b, i: (b, jnp.minimum(i + 1, n_tiles - 1), 0))
    return pl.pallas_call(
        functools.partial(_hgrn_kernel, tb=tb),
        grid=(bsz, n_tiles),
        in_specs=[row_spec, next_spec, _const_spec((BLOCK, D_MODEL)), _const_spec((1, D_MODEL)),
                  _const_spec((D_MODEL, C_IN)), _const_spec((SUBLANES, C_K)), _const_spec((1, C_W)),
                  _const_spec((C_W, D_MODEL))],
        out_specs=[row_spec, _meta_out_spec()],
        out_shape=[jax.ShapeDtypeStruct(x.shape, F32), jax.ShapeDtypeStruct((BLOCK, D_MODEL), F32)],
        scratch_shapes=[pltpu.VMEM((tb, D_MODEL), BF16),
                        pltpu.VMEM((tb, C_K), F32), pltpu.VMEM((tb, C_K), F32),
                        pltpu.VMEM((tb, C_K), F32), pltpu.VMEM((tb, C_W), BF16),
                        pltpu.VMEM((tb, C_W), F32), pltpu.VMEM((tb, C_W), BF16),
                        pltpu.VMEM((C_HEADS, C_DV, C_DK), F32),
                        pltpu.VMEM((tb, C_K), BF16), pltpu.VMEM((tb, C_K), BF16),
                        pltpu.VMEM((tb, C_K), BF16), pltpu.VMEM((tb, C_W), F32)]
                       + [pltpu.VMEM((tb, C_PIECE), F32)] * (C_IN // C_PIECE),
        compiler_params=pltpu.CompilerParams(dimension_semantics=("arbitrary", "arbitrary"),
                                             vmem_limit_bytes=VMEM_LIMIT),
        name="hgrn2_layer",
    )(x, x, hm, gain, w_in, lb, o_gain, w_out)


def _rope_tables(first, n):
    half = A_HD // 2
    inv = ROPE_THETA ** (-jnp.arange(half, dtype=F32) / half)
    pos = (jnp.arange(first, first + n) - FRONT_PAD).astype(jnp.int32)
    ang = pos.astype(F32)[:, None] * inv[None, :]
    reps = LANES // half
    sign = jnp.tile(jnp.asarray([-1.0, 1.0], F32), reps // 2)
    cos = jnp.broadcast_to(jnp.cos(ang)[:, None, :], (n, reps, half))
    sin = jnp.sin(ang)[:, None, :] * sign[None, :, None]
    return jnp.concatenate([cos, sin], axis=1).reshape(n, 2 * LANES)


def _attn_gain_rows(q_gain, k_gain):
    half = A_HD // 2
    reps = LANES // A_HD
    rows = []
    for g, scale in ((q_gain.astype(F32), A_HD ** -0.5 * LOG2E), (k_gain.astype(F32), 1.0)):
        g_rot = jnp.concatenate([g[half:], g[:half]])
        rows += [jnp.tile(g * scale, reps), jnp.tile(g_rot * scale, reps)]
    return jnp.concatenate([jnp.stack(rows), jnp.zeros((SUBLANES - len(rows), LANES), F32)], axis=0)


def _row_tile(seq):
    tb = ROW_TILE
    while seq % tb:
        tb //= 2
    assert tb >= BLOCK, "sequence length must be a multiple of the 128-token block"
    return tb


def kernel(x, meta, norm_gain, a_w_in, a_q_gain, a_k_gain, a_sinks, a_w_out, b_w_in, b_conv_w,
           b_conv_b, b_gate_bias, b_h_gain, b_w_out, c_w_in, c_gamma, c_o_gain, c_w_out):
    bsz, seq, _ = x.shape
    depth = norm_gain.shape[0]
    tab, tabm = _rope_tables(BLOCK, seq), _rope_tables(0, BLOCK)
    hm = jnp.concatenate([jnp.zeros((FRONT_PAD, D_MODEL), x.dtype), meta.astype(x.dtype)], axis=0)
    p = jax.nn.softmax(c_gamma.astype(F32), axis=0)
    lower_bounds = jnp.cumsum(p, axis=0) - p

    def pair_heads(w, axis):
        shape = w.shape[:axis] + (A_KV // 2, 2, A_GROUP, A_HD) + w.shape[axis + 1:]
        perm = list(range(len(shape)))
        perm[axis + 1], perm[axis + 2] = axis + 2, axis + 1
        return w.reshape(shape).transpose(perm).reshape(w.shape)

    h = x
    for i in range(depth):
        kind, j = i % 3, i // 3
        gain = norm_gain[i].reshape(1, D_MODEL).astype(F32)
        if kind == 0:
            w = a_w_in[j].astype(BF16)
            w_in = jnp.concatenate([pair_heads(w[:, :A_W], 1), w[:, A_W:A_W + 2 * A_KVW],
                                    pair_heads(w[:, A_W + 2 * A_KVW:], 1)], axis=1)
            h, hm = _attn_layer(h, hm, tab, tabm, _attn_gain_rows(a_q_gain[j], a_k_gain[j]), gain,
                                w_in, a_sinks[j].astype(F32) * LOG2E,
                                pair_heads(a_w_out[j].astype(BF16), 0), _row_tile(seq))
        elif kind == 1:
            w = b_w_in[j].astype(BF16)
            o_v, o_i, o_o, o_z = 2 * B_QK, 2 * B_QK + B_W, 2 * B_QK + B_W + 2 * B_HEADS, 2 * B_QK + 2 * B_W + 2 * B_HEADS
            wg = jnp.pad(w[:, o_i:o_o], ((0, 0), (0, LANES - 2 * B_HEADS)))
            gb = b_gate_bias[j].astype(F32)
            h, hm = _mlstm_layer(
                h, hm, gain, w[:, :o_v], w[:, o_v:o_i], w[:, o_o:o_z], w[:, o_z:], wg,
                b_conv_w[j].astype(F32), b_conv_b[j].reshape(1, 2 * B_QK).astype(F32),
                jnp.broadcast_to(gb[:B_HEADS, None], (B_HEADS, BLOCK)),
                jnp.broadcast_to(gb[B_HEADS:, None], (B_HEADS, BLOCK)),
                b_h_gain[j].reshape(1, B_W).astype(F32), b_w_out[j].astype(BF16), _row_tile(seq))
        else:
            w = c_w_in[j].astype(BF16)
            lb = lower_bounds[i]
            lb_rows = jnp.zeros((SUBLANES, C_K), F32)
            lb_rows = lb_rows.at[0].set(jnp.log(lb)).at[1].set(jnp.log1p(-lb)).at[2].set(1.0 - lb)
            h, hm = _hgrn_layer(
                h, hm, gain, w, lb_rows, c_o_gain[j].reshape(1, C_W).astype(F32),
                c_w_out[j].astype(BF16), _row_tile(seq))
    return h
```

```python
import functools

import jax
import jax.numpy as jnp
from jax import lax
from jax.experimental import pallas as pl
from jax.experimental.pallas import tpu as pltpu

F32 = jnp.float32
BF16 = jnp.bfloat16

D_MODEL = 1024
BLOCK = 128
N_META = 16
FRONT_PAD = BLOCK - N_META
EPS = 1e-6
ROPE_THETA = 10000.0
LANES = 128
SUBLANES = 8
MXU_TILE = 256
ROW_TILE = 512

A_HEADS, A_KV, A_HD = 16, 4, 64
A_GROUP = A_HEADS // A_KV
A_W = A_HEADS * A_HD
A_KVW = A_KV * A_HD
A_IN = 2 * A_W + 2 * A_KVW
A_PIECE = 2 * MXU_TILE
LOG2E = 1.4426950408889634

B_HEADS, B_DK, B_DV, B_CONV = 8, 128, 256, 4
B_QK = B_HEADS * B_DK
B_W = B_HEADS * B_DV
B_HIST = 8
B_CONV_COLS = MXU_TILE
B_QK_PIECES = 4

C_HEADS, C_DK, C_DV, C_CHUNK = 8, 128, 128, 64
C_K = C_HEADS * C_DK
C_W = C_HEADS * C_DV
C_IN = 2 * C_K + 2 * C_W
C_PIECE = MXU_TILE
C_LEVELS = (1, 2, 4, 8, 16, 32)
C_SAFE_DECAY = 60.0

VMEM_LIMIT = 60 * 1024 * 1024
B_VMEM_LIMIT = 62 * 1024 * 1024


def _sigmoid(x):
    return 0.5 + 0.5 * jnp.tanh(0.5 * x)


def _silu(x):
    half = 0.5 * x
    return half + half * jnp.tanh(half)


def _log_sigmoid(x):
    return jnp.minimum(x, 0.0) - jnp.log(1.0 + jnp.exp(-jnp.abs(x)))


def _rmsnorm(x, g):
    ms = jnp.mean(x * x, axis=-1, keepdims=True)
    return x * lax.rsqrt(ms + EPS) * g


def _dot(a, b):
    return jnp.dot(a, b, preferred_element_type=F32)


def _dot_nt(a, b):
    return lax.dot_general(a, b, (((1,), (1,)), ((), ())), preferred_element_type=F32)


def _dot_tn(a, b):
    return lax.dot_general(a, b, (((0,), (0,)), ((), ())), preferred_element_type=F32)


def _valid_rows(rows, cols):
    return lax.broadcasted_iota(jnp.int32, (rows, cols), 0) >= FRONT_PAD


class _PieceBuffers:
    def __init__(self, refs):
        self.refs = refs
        self.piece = refs[0].shape[1]

    def cols(self, rows, lo, hi):
        k = lo // self.piece
        assert hi <= (k + 1) * self.piece, "a read must stay inside one piece"
        return self.refs[k][0:rows, lo - k * self.piece:hi - k * self.piece]

    def project(self, k, rows, u, w_ref):
        self.refs[k][0:rows, :] = _dot(u, w_ref[:, k * self.piece:(k + 1) * self.piece])


class _PieceRefill:
    def __init__(self, proj, next_piece):
        self.next_piece = next_piece
        self.n_pieces = len(proj.refs) if next_piece is not None else 0
        self.groups = proj.piece // LANES
        self.consumed, self.emitted = set(), set()

    def done(self, lo, hi):
        assert lo % LANES == 0 and hi % LANES == 0
        self.consumed.update(range(lo // LANES, hi // LANES))

    def emit(self, n=1):
        for k in range(self.n_pieces):
            ready = all(k * self.groups + g in self.consumed for g in range(self.groups))
            if n > 0 and ready and k not in self.emitted:
                self.emitted.add(k)
                self.next_piece(k)
                n -= 1

    def flush(self):
        self.emit(self.n_pieces)

    def fork(self):
        other = _PieceRefill.__new__(_PieceRefill)
        other.__dict__.update(self.__dict__)
        other.consumed, other.emitted = set(self.consumed), set(self.emitted)
        return other


def _attn_rows(h, rows, meta, tab, gains_ref, sink_ref, wout_ref, proj, qbuf, klo, khi, vt, obuf, gz, next_piece):
    lane = lax.broadcasted_iota(jnp.int32, (rows, LANES), 1)
    low_head = lane < A_HD
    first_half = (lane % A_HD) < (A_HD // 2)

    def norm_rope(x, ctab, stab):
        sq = x * x
        s_lo = jnp.sum(jnp.where(low_head, sq, 0.0), axis=-1, keepdims=True)
        s_hi = jnp.sum(jnp.where(low_head, 0.0, sq), axis=-1, keepdims=True)
        r = jnp.where(low_head, lax.rsqrt(s_lo / A_HD + EPS), lax.rsqrt(s_hi / A_HD + EPS))
        rot = jnp.where(first_half, pltpu.roll(x, LANES - A_HD // 2, 1), pltpu.roll(x, A_HD // 2, 1))
        return (x * ctab + rot * stab) * r

    refill = _PieceRefill(proj, next_piece)
    z0 = A_W + 2 * A_KVW
    cos, sin = tab[:, 0:LANES], tab[:, LANES:2 * LANES]
    q_cos, q_sin = cos * gains_ref[0:1, :], sin * gains_ref[1:2, :]
    k_cos, k_sin = cos * gains_ref[2:3, :], sin * gains_ref[3:4, :]
    for lo in range(0, A_W, A_PIECE):
        for p in range(lo // LANES, (lo + A_PIECE) // LANES):
            x = proj.cols(rows, p * LANES, (p + 1) * LANES)
            qbuf[0:rows, p * LANES:(p + 1) * LANES] = norm_rope(x, q_cos, q_sin).astype(BF16)
        refill.done(lo, lo + A_PIECE)
        refill.emit()
        gz[0:rows, lo:lo + A_PIECE] = _silu(proj.cols(rows, z0 + lo, z0 + lo + A_PIECE))
        refill.done(z0 + lo, z0 + lo + A_PIECE)
    for m in range(A_KVW // LANES):
        k = norm_rope(proj.cols(rows, A_W + m * LANES, A_W + (m + 1) * LANES), k_cos, k_sin)
        klo[BLOCK:BLOCK + rows, m * LANES:(m + 1) * LANES] = jnp.where(low_head, k, 0.0).astype(BF16)
        khi[BLOCK:BLOCK + rows, m * LANES:(m + 1) * LANES] = jnp.where(low_head, 0.0, k).astype(BF16)
    refill.done(A_W, A_W + A_KVW)
    refill.emit()
    for m in range(A_KVW // LANES):
        v = proj.cols(rows, A_W + A_KVW + m * LANES, A_W + A_KVW + (m + 1) * LANES)
        for n in range(rows // BLOCK):
            vt[m * LANES:(m + 1) * LANES, (n + 1) * BLOCK:(n + 2) * BLOCK] = (
                v[n * BLOCK:(n + 1) * BLOCK].T.astype(BF16))
    refill.done(A_W + A_KVW, z0)
    refill.emit()

    c_idx = lax.broadcasted_iota(jnp.int32, (BLOCK, BLOCK), 0)
    t_idx = lax.broadcasted_iota(jnp.int32, (BLOCK, BLOCK), 1)
    own = c_idx <= t_idx
    first_valid = FRONT_PAD if meta else jnp.where(pl.program_id(1) == 0, FRONT_PAD, 0)

    def fold(tile, first):
        prev, cur = tile[0:BLOCK], tile[BLOCK:2 * BLOCK]
        if meta:
            return jnp.where(own & (c_idx >= first_valid), cur, -jnp.inf)
        if first:
            prev = jnp.where(c_idx >= first_valid, prev, -jnp.inf)
        return jnp.where(own, cur, prev)

    def scores(n):
        r0 = n * BLOCK
        out = []
        for m in range(A_KVW // LANES):
            keys = jnp.concatenate([klo[r0:r0 + 2 * BLOCK, m * LANES:(m + 1) * LANES],
                                    khi[r0:r0 + 2 * BLOCK, m * LANES:(m + 1) * LANES]], axis=0)
            qs = jnp.concatenate([qbuf[r0:r0 + BLOCK, (A_GROUP * m + g) * LANES:(A_GROUP * m + g + 1) * LANES]
                                  for g in range(A_GROUP)], axis=0)
            out.append(_dot_nt(keys, qs))
        return out

    n_blocks = rows // BLOCK
    pending = scores(0)
    for n in range(n_blocks):
        r0 = n * BLOCK
        s_now = pending
        if n + 1 < n_blocks:
            pending = scores(n + 1)
        refill.emit()
        for m in range(A_KVW // LANES):
            outs = []
            for half in range(2):
                j = 2 * m + half
                es, inv = [], []
                for g in range(A_GROUP):
                    tile = fold(s_now[m][half * 2 * BLOCK:(half + 1) * 2 * BLOCK, g * BLOCK:(g + 1) * BLOCK], n == 0)
                    sink = sink_ref[A_GROUP * j + g]
                    mx = jnp.maximum(jnp.max(tile, axis=0, keepdims=True), sink)
                    e = jnp.exp2(tile - mx)
                    inv.append(1.0 / (jnp.sum(e, axis=0, keepdims=True) + jnp.exp2(sink - mx)))
                    es.append(jnp.concatenate([jnp.where(own, 0.0, e), jnp.where(own, e, 0.0)],
                                              axis=0).astype(BF16))
                o_t = _dot(vt[j * A_HD:(j + 1) * A_HD, r0:r0 + 2 * BLOCK], jnp.concatenate(es, axis=1))
                outs.append(o_t * jnp.concatenate(inv, axis=1))
            for g in range(A_GROUP):
                pair = jnp.concatenate([outs[0][:, g * BLOCK:(g + 1) * BLOCK],
                                        outs[1][:, g * BLOCK:(g + 1) * BLOCK]], axis=0)
                p = A_GROUP * m + g
                obuf[r0:r0 + BLOCK, p * LANES:(p + 1) * LANES] = pair.T

    refill.flush()
    klo[0:BLOCK, :] = klo[rows:rows + BLOCK, :]
    khi[0:BLOCK, :] = khi[rows:rows + BLOCK, :]
    vt[:, 0:BLOCK] = vt[:, rows:rows + BLOCK]

    y = _dot((obuf[0:rows, :] * gz[0:rows, :]).astype(BF16), wout_ref[...])
    if meta:
        y = jnp.where(_valid_rows(rows, D_MODEL), y, 0.0)
    return h + y


def _attn_kernel(x_ref, xn_ref, hm_ref, tab_ref, tabm_ref, gains_ref, gain_ref, win_ref, sink_ref, wout_ref,
                 y_ref, hm_out_ref, qbuf, klo, khi, vt, obuf, gz, ub, *pieces, tb):
    proj = _PieceBuffers(pieces)
    n_pieces = len(pieces)
    params = (gains_ref, sink_ref, wout_ref, proj, qbuf, klo, khi, vt, obuf, gz)

    @pl.when(pl.program_id(1) == 0)
    def _():
        klo[0:BLOCK, :] = jnp.zeros((BLOCK, A_KVW), BF16)
        khi[0:BLOCK, :] = jnp.zeros((BLOCK, A_KVW), BF16)
        vt[:, 0:BLOCK] = jnp.zeros((A_KVW, BLOCK), BF16)
        hm = hm_ref[...]
        um = _rmsnorm(hm, gain_ref[...]).astype(BF16)
        for k in range(n_pieces):
            proj.project(k, BLOCK, um, win_ref)
        hm_out_ref[...] = _attn_rows(hm, BLOCK, True, tabm_ref[...], *params, None)
        u0 = _rmsnorm(x_ref[0], gain_ref[...]).astype(BF16)
        for k in range(n_pieces):
            proj.project(k, tb, u0, win_ref)

    ub[...] = _rmsnorm(xn_ref[0], gain_ref[...]).astype(BF16)

    def next_piece(k):
        proj.project(k, tb, ub[...], win_ref)

    y_ref[0] = _attn_rows(x_ref[0], tb, False, tab_ref[...], *params, next_piece)


def _const_spec(shape):
    zeros = (0,) * len(shape)
    return pl.BlockSpec(shape, lambda b, i: zeros, pipeline_mode=pl.Buffered(1))


def _meta_out_spec():
    return pl.BlockSpec((BLOCK, D_MODEL), lambda b, i: (0, 0))


def _attn_layer(x, hm, tab, tabm, gains, gain, w_in, sinks, w_out, tb):
    bsz, seq, _ = x.shape
    n_tiles = seq // tb
    row_spec = pl.BlockSpec((1, tb, D_MODEL), lambda b, i: (b, i, 0))
    next_spec = pl.BlockSpec((1, tb, D_MODEL), lambda b, i: (b, jnp.minimum(i + 1, n_tiles - 1), 0))
    return pl.pallas_call(
        functools.partial(_attn_kernel, tb=tb),
        grid=(bsz, n_tiles),
        in_specs=[row_spec, next_spec, _const_spec((BLOCK, D_MODEL)),
                  pl.BlockSpec((tb, 2 * LANES), lambda b, i: (i, 0)), _const_spec((BLOCK, 2 * LANES)),
                  _const_spec((SUBLANES, LANES)), _const_spec((1, D_MODEL)), _const_spec((D_MODEL, A_IN)),
                  pl.BlockSpec(memory_space=pltpu.SMEM), _const_spec((A_W, D_MODEL))],
        out_specs=[row_spec, _meta_out_spec()],
        out_shape=[jax.ShapeDtypeStruct(x.shape, F32), jax.ShapeDtypeStruct((BLOCK, D_MODEL), F32)],
        scratch_shapes=[pltpu.VMEM((tb, A_W), BF16),
                        pltpu.VMEM((tb + BLOCK, A_KVW), BF16), pltpu.VMEM((tb + BLOCK, A_KVW), BF16),
                        pltpu.VMEM((A_KVW, tb + BLOCK), BF16), pltpu.VMEM((tb, A_W), F32),
                        pltpu.VMEM((tb, A_W), F32), pltpu.VMEM((tb, D_MODEL), BF16)]
                       + [pltpu.VMEM((tb, A_PIECE), F32)] * (A_IN // A_PIECE),
        compiler_params=pltpu.CompilerParams(dimension_semantics=("arbitrary", "arbitrary"),
                                             vmem_limit_bytes=VMEM_LIMIT),
        name="swa_layer",
    )(x, x, hm, tab, tabm, gains, gain, w_in, sinks, w_out)


def _mlstm_conv(rows, u, wv_ref, wg_ref, cw_ref, cb_ref, qkraw, qs, ks, vs, gates, between=None):
    n_blk = 2 * B_QK // B_CONV_COLS
    v_piece = B_W // B_QK_PIECES
    after_block = [[] for _ in range(n_blk)]
    for k in range(B_QK_PIECES):
        after_block[k * n_blk // B_QK_PIECES].append(("v", k))
    after_block[n_blk - 1].append(("g", 0))

    for blk in range(n_blk):
        cs = slice(blk * B_CONV_COLS, (blk + 1) * B_CONV_COLS)
        acc = cb_ref[:, cs]
        for j in range(B_CONV):
            start = B_HIST - (B_CONV - 1) + j
            acc = acc + qkraw[start:start + rows, cs] * cw_ref[j:j + 1, cs]
        act = _silu(acc)
        if blk * B_CONV_COLS < B_QK:
            qs[0:rows, cs] = act.astype(BF16)
        else:
            ks[0:rows, blk * B_CONV_COLS - B_QK:(blk + 1) * B_CONV_COLS - B_QK] = (
                act * (B_DK ** -0.5)).astype(BF16)
        qkraw[0:B_HIST, cs] = qkraw[rows:rows + B_HIST, cs]
        for kind, k in after_block[blk]:
            if kind == "v":
                vs[0:rows, k * v_piece:(k + 1) * v_piece] = _dot(
                    u(), wv_ref[:, k * v_piece:(k + 1) * v_piece]).astype(BF16)
            else:
                gates[0:rows, :] = _dot(u(), wg_ref[...])
        if between is not None:
            between(blk)


def _mlstm_chunks(rows, meta, bi_ref, bf_ref, qs, ks, vs, hb, gates, cst, mst, next_qk):
    t_idx = lax.broadcasted_iota(jnp.int32, (BLOCK, BLOCK), 0)
    s_idx = lax.broadcasted_iota(jnp.int32, (BLOCK, BLOCK), 1)
    causal = s_idx <= t_idx
    lane8 = lax.broadcasted_iota(jnp.int32, (B_HEADS, BLOCK), 1)

    n_chunks = rows // BLOCK

    def chunk(c):
        r0 = c * BLOCK
        g_t = gates[pl.ds(r0, BLOCK), :].T
        li = g_t[0:B_HEADS] + bi_ref[...]
        lf = _log_sigmoid(g_t[B_HEADS:2 * B_HEADS] + bf_ref[...])
        if meta:
            li = jnp.where(lane8 >= FRONT_PAD, li, -jnp.inf)
            lf = jnp.where(lane8 >= FRONT_PAD, lf, 0.0)
        a = lf
        shift = 1
        while shift < BLOCK:
            a = a + jnp.where(lane8 >= shift, pltpu.roll(a, shift, 1), 0.0)
            shift *= 2
        run = li - a
        shift = 1
        while shift < BLOCK:
            run = jnp.maximum(run, jnp.where(lane8 >= shift, pltpu.roll(run, shift, 1), -jnp.inf))
            shift *= 2
        m_prev = mst[:, 0:1]
        decay_max = jnp.maximum(m_prev, run)
        m_t = a + decay_max
        g_tot = a[:, BLOCK - 1:BLOCK]
        uu = g_tot - a + li
        m_new = jnp.maximum(g_tot + m_prev, jnp.max(uu, axis=1, keepdims=True))
        w_old = jnp.exp(g_tot + m_prev - m_new)
        w_s = jnp.exp(uu - m_new)
        b_row = li - a
        mst[...] = jnp.broadcast_to(m_new, (B_HEADS, LANES))

        def col_tile(row):
            return jnp.broadcast_to(row, (BLOCK, BLOCK)).T

        heads = range(B_HEADS)
        ones = jnp.ones((BLOCK, LANES), BF16)
        qh = [qs[pl.ds(r0, BLOCK), hd * B_DK:(hd + 1) * B_DK] for hd in heads]
        kh = [ks[pl.ds(r0, BLOCK), hd * B_DK:(hd + 1) * B_DK] for hd in heads]
        vh = [jnp.concatenate([vs[pl.ds(r0, BLOCK), hd * B_DV:(hd + 1) * B_DV], ones], axis=1) for hd in heads]
        c_prev = [cst[hd] for hd in heads]
        qk = [_dot_nt(qh[hd], kh[hd]) for hd in heads]
        inter_c = [_dot(qh[hd], c_prev[hd].astype(BF16)) for hd in heads]
        if next_qk is not None:
            for k in range(B_QK_PIECES):
                if k * n_chunks // B_QK_PIECES == c:
                    next_qk(k)

        sc, kw, w_inter, floor = [], [], [], []
        for hd in heads:
            dm_t = col_tile(decay_max[hd:hd + 1, :])
            w = jnp.exp(jnp.where(causal, b_row[hd:hd + 1, :] - dm_t, -jnp.inf))
            sc.append((qk[hd] * w).astype(BF16))
            kw.append((kh[hd].astype(F32) * col_tile(w_s[hd:hd + 1, :])).astype(BF16))
            w_inter.append(jnp.exp(m_prev[hd:hd + 1, :] - dm_t))
            floor.append(jnp.exp(-col_tile(m_t[hd:hd + 1, :])))

        pv = [_dot(sc[hd], vh[hd]) for hd in heads]
        kv = [_dot_tn(kw[hd], vh[hd]) for hd in heads]
        for hd in heads:
            den = w_inter[hd] * inter_c[hd][:, B_DV:] + pv[hd][:, B_DV:]
            scale = 1.0 / jnp.maximum(jnp.abs(den), floor[hd])
            for half in range(B_DV // LANES):
                ls = slice(half * LANES, (half + 1) * LANES)
                hb[pl.ds(r0, BLOCK), hd * B_DV + half * LANES:hd * B_DV + (half + 1) * LANES] = (
                    (w_inter[hd] * inter_c[hd][:, ls] + pv[hd][:, ls]) * scale)
            cst[hd] = w_old[hd:hd + 1, :] * c_prev[hd] + kv[hd]

    for c in range(n_chunks):
        chunk(c)


def _mlstm_post_head(hd, rows, u, wo_ref, wz_ref, hg_ref, hb, gated):
    cs = slice(hd * B_DV, (hd + 1) * B_DV)
    hh = _sigmoid(_dot(u(), wo_ref[:, cs])) * hb[0:rows, cs]
    gated[0:rows, cs] = (_rmsnorm(hh, hg_ref[:, cs]) * _silu(_dot(u(), wz_ref[:, cs]))).astype(BF16)


def _mlstm_kernel(x_ref, xn_ref, hm_ref, gain_ref, wqk_ref, wv_ref, wo_ref, wz_ref, wg_ref, cw_ref, cb_ref,
                  bi_ref, bf_ref, hg_ref, wout_ref, y_ref, hm_out_ref,
                  ub, qkraw, qkg, vs, hb, gates, gated, cst, mst, *, tb):
    qs, ks = qkg.at[:, 0:B_QK], qkg.at[:, B_QK:2 * B_QK]
    conv_refs = (wv_ref, wg_ref, cw_ref, cb_ref, qkraw, qs, ks, vs, gates)
    chunk_refs = (bi_ref, bf_ref, qs, ks, vs, hb, gates, cst, mst)
    post_refs = (wo_ref, wz_ref, hg_ref, hb, gated)
    slot = pl.program_id(1) % 2

    @pl.when(pl.program_id(1) == 0)
    def _():
        qkraw[0:B_HIST, :] = jnp.zeros((B_HIST, 2 * B_QK), F32)
        cst[...] = jnp.zeros(cst.shape, F32)
        mst[...] = jnp.zeros(mst.shape, F32)
        hm = hm_ref[...]
        ub[1, 0:BLOCK, :] = _rmsnorm(hm, gain_ref[...]).astype(BF16)
        um = lambda: ub[1, 0:BLOCK, :]
        qkraw[B_HIST:B_HIST + BLOCK, :] = _dot(um(), wqk_ref[...])
        _mlstm_conv(BLOCK, um, *conv_refs)
        _mlstm_chunks(BLOCK, True, *chunk_refs, None)
        for hd in range(B_HEADS):
            _mlstm_post_head(hd, BLOCK, um, *post_refs)
        ym = _dot(gated[0:BLOCK, :], wout_ref[...])
        hm_out_ref[...] = hm + jnp.where(_valid_rows(BLOCK, D_MODEL), ym, 0.0)
        ub[0] = _rmsnorm(x_ref[0], gain_ref[...]).astype(BF16)
        qkraw[B_HIST:B_HIST + tb, :] = _dot(ub[0], wqk_ref[...])
        _mlstm_conv(tb, lambda: ub[0], *conv_refs)

    ub[1 - slot] = _rmsnorm(xn_ref[0], gain_ref[...]).astype(BF16)
    qk_piece = 2 * B_QK // B_QK_PIECES

    def next_qk(k):
        cols = slice(k * qk_piece, (k + 1) * qk_piece)
        qkraw[B_HIST:B_HIST + tb, cols] = _dot(ub[1 - slot], wqk_ref[:, cols])

    _mlstm_chunks(tb, False, *chunk_refs, next_qk)
    per_head = 2 * B_QK // B_CONV_COLS // B_HEADS

    def between(blk):
        if blk % per_head == per_head - 1:
            _mlstm_post_head(blk // per_head, tb, lambda: ub[slot], *post_refs)

    _mlstm_conv(tb, lambda: ub[1 - slot], *conv_refs, between=between)
    y_ref[0] = x_ref[0] + _dot(gated[...], wout_ref[...])


def _mlstm_layer(x, hm, gain, wqk, wv, wo, wz, wg, conv_w, conv_b, bias_i, bias_f, h_gain, w_out, tb):
    bsz, seq, _ = x.shape
    n_tiles = seq // tb
    row_spec = pl.BlockSpec((1, tb, D_MODEL), lambda b, i: (b, i, 0))
    next_spec = pl.BlockSpec((1, tb, D_MODEL), lambda b, i: (b, jnp.minimum(i + 1, n_tiles - 1), 0))
    return pl.pallas_call(
        functools.partial(_mlstm_kernel, tb=tb),
        grid=(bsz, n_tiles),
        in_specs=[row_spec, next_spec, _const_spec((BLOCK, D_MODEL)), _const_spec((1, D_MODEL)),
                  _const_spec((D_MODEL, 2 * B_QK)), _const_spec((D_MODEL, B_W)),
                  _const_spec((D_MODEL, B_W)), _const_spec((D_MODEL, B_W)),
                  _const_spec((D_MODEL, LANES)), _const_spec((B_CONV, 2 * B_QK)),
                  _const_spec((1, 2 * B_QK)), _const_spec((B_HEADS, BLOCK)),
                  _const_spec((B_HEADS, BLOCK)), _const_spec((1, B_W)), _const_spec((B_W, D_MODEL))],
        out_specs=[row_spec, _meta_out_spec()],
        out_shape=[jax.ShapeDtypeStruct(x.shape, F32), jax.ShapeDtypeStruct((BLOCK, D_MODEL), F32)],
        scratch_shapes=[pltpu.VMEM((2, tb, D_MODEL), BF16),
                        pltpu.VMEM((tb + B_HIST, 2 * B_QK), F32), pltpu.VMEM((tb, 2 * B_QK), BF16),
                        pltpu.VMEM((tb, B_W), BF16), pltpu.VMEM((tb, B_W), F32),
                        pltpu.VMEM((tb, LANES), F32), pltpu.VMEM((tb, B_W), BF16),
                        pltpu.VMEM((B_HEADS, B_DK, B_DV + LANES), F32), pltpu.VMEM((B_HEADS, LANES), F32)],
        compiler_params=pltpu.CompilerParams(dimension_semantics=("arbitrary", "arbitrary"),
                                             vmem_limit_bytes=B_VMEM_LIMIT),
        name="mlstm_layer",
    )(x, x, hm, gain, wqk, wv, wo, wz, wg, conv_w, conv_b, bias_i, bias_f, h_gain, w_out)


def _hgrn_level_masks():
    t_idx = lax.broadcasted_iota(jnp.int32, (C_CHUNK, C_CHUNK), 0)
    s_idx = lax.broadcasted_iota(jnp.int32, (C_CHUNK, C_CHUNK), 1)
    masks = []
    for b in C_LEVELS:
        same = (t_idx // (2 * b)) == (s_idx // (2 * b))
        masks.append(same & ((t_idx % (2 * b)) >= b) & ((s_idx % (2 * b)) < b))
    return t_idx == s_idx, masks


def _hgrn_reference_rows(ab, r0, b):
    sub = lax.broadcasted_iota(jnp.int32, (8, C_K), 0)
    pieces = []
    if 2 * b >= 8:
        for p in range(C_CHUNK // (2 * b)):
            mid = r0 + 2 * b * p + b - 1
            pieces.append(jnp.broadcast_to(ab[pl.ds(mid, 1), :], (2 * b, C_K)))
    else:
        for v in range(C_CHUNK // 8):
            cand = [jnp.broadcast_to(ab[pl.ds(r0 + 8 * v + 2 * b * p + b - 1, 1), :], (8, C_K))
                    for p in range(8 // (2 * b))]
            sel = cand[-1]
            for p in reversed(range(len(cand) - 1)):
                sel = jnp.where(sub < 2 * b * (p + 1), cand[p], sel)
            pieces.append(sel)
    return jnp.concatenate(pieces, axis=0)


def _chunk_cumsum(x):
    rows, cols = x.shape
    x3 = x.reshape(rows // SUBLANES, SUBLANES, cols)
    sub = lax.broadcasted_iota(jnp.int32, x3.shape, 1)
    for s in (1, 2, 4):
        x3 = x3 + jnp.where(sub >= s, pltpu.roll(x3, s, 1), 0.0)
    x = x3.reshape(rows, cols)
    out = []
    for c in range(rows // C_CHUNK):
        run = None
        for g in range(C_CHUNK // SUBLANES):
            r = c * C_CHUNK + g * SUBLANES
            blk = x[r:r + SUBLANES]
            if run is not None:
                blk = blk + run
            out.append(blk)
            run = jnp.broadcast_to(blk[SUBLANES - 1:SUBLANES], (SUBLANES, cols))
    return jnp.concatenate(out, axis=0)


def _hgrn_rows(h, rows, meta, lb_ref, og_ref, wout_ref, proj, qb, kb, ab, vb, zb, gated, st,
               qin, kneg, kout, ob, next_piece):
    refill = _PieceRefill(proj, next_piece)

    def unpack_gate(lo):
        zb[0:rows, lo:lo + C_PIECE] = _silu(proj.cols(rows, 3 * C_K + lo, 3 * C_K + lo + C_PIECE))
        refill.done(3 * C_K + lo, 3 * C_K + lo + C_PIECE)

    def unpack_value(lo):
        vb[0:rows, lo:lo + C_PIECE] = proj.cols(rows, 2 * C_K + lo, 2 * C_K + lo + C_PIECE).astype(BF16)
        refill.done(2 * C_K + lo, 2 * C_K + lo + C_PIECE)

    def unpack_query(lo):
        qb[0:rows, lo:lo + C_PIECE] = _silu(proj.cols(rows, lo, lo + C_PIECE))
        refill.done(lo, lo + C_PIECE)

    def unpack_forget(lo):
        cs = slice(lo, lo + LANES)
        fpre = proj.cols(rows, C_K + lo, C_K + lo + LANES)
        log_lb, log_1mlb, one_mlb = lb_ref[0:1, cs], lb_ref[1:2, cs], lb_ref[2:3, cs]
        soft = jnp.log(1.0 + jnp.exp(-jnp.abs(fpre)))
        log_sig = jnp.minimum(fpre, 0.0) - soft
        kk = one_mlb * jnp.exp(log_sig - fpre)
        grow = log_1mlb + log_sig
        lf = jnp.maximum(log_lb, grow) + jnp.log(1.0 + jnp.exp(-jnp.abs(log_lb - grow)))
        if meta:
            valid = _valid_rows(rows, LANES)
            lf = jnp.where(valid, lf, 0.0)
            kk = jnp.where(valid, kk, 0.0)
        ab[0:rows, cs] = _chunk_cumsum(lf)
        kb[0:rows, cs] = kk
        refill.done(C_K + lo, C_K + lo + LANES)

    light = ([functools.partial(unpack_gate, lo) for lo in range(0, C_W, C_PIECE)]
             + [functools.partial(unpack_query, lo) for lo in range(0, C_K, C_PIECE)]
             + [functools.partial(unpack_value, lo) for lo in range(0, C_W, C_PIECE)])
    heavy = [functools.partial(unpack_forget, lo) for lo in range(0, C_K, LANES)]
    for i in range(max(len(light), len(heavy))):
        if i < len(light):
            light[i]()
            if i < len(heavy):
                refill.emit()
        if i < len(heavy):
            heavy[i]()
            refill.emit()

    n_chunks = rows // C_CHUNK
    a_end = ab[C_CHUNK - 1:C_CHUNK, :]
    for c in range(1, n_chunks):
        a_end = jnp.minimum(a_end, ab[(c + 1) * C_CHUNK - 1:(c + 1) * C_CHUNK, :])
    mild = jnp.min(a_end) >= -C_SAFE_DECAY

    t_idx = lax.broadcasted_iota(jnp.int32, (C_CHUNK, C_CHUNK), 0)
    s_idx = lax.broadcasted_iota(jnp.int32, (C_CHUNK, C_CHUNK), 1)
    causal = s_idx <= t_idx
    eye, masks = _hgrn_level_masks()

    def chunk(c, carry, factored, emit=None):
        r0 = c * C_CHUNK if isinstance(c, int) else pl.multiple_of(c * C_CHUNK, C_CHUNK)
        v = vb[pl.ds(r0, C_CHUNK), :]
        a_last = ab[pl.ds(r0 + C_CHUNK - 1, 1), :]
        decay = jnp.exp(a_last)
        if factored:
            q_in = qin[pl.ds(r0, C_CHUNK), :]
            k_out = kout[pl.ds(r0, C_CHUNK), :]
            q_lv, k_lv, level_masks = [q_in], [kneg[pl.ds(r0, C_CHUNK), :]], [causal]
        else:
            a_c = ab[pl.ds(r0, C_CHUNK), :]
            q = qb[pl.ds(r0, C_CHUNK), :]
            k = kb[pl.ds(r0, C_CHUNK), :]
            q_in = (q * jnp.exp(a_c)).astype(BF16)
            k_out = (k * jnp.exp(a_last - a_c)).astype(BF16)
            q_lv, k_lv, level_masks = [q.astype(BF16)], [k.astype(BF16)], [eye] + masks
            for b in C_LEVELS:
                w = jnp.exp(-jnp.abs(a_c - _hgrn_reference_rows(ab, r0, b)))
                q_lv.append((q * w).astype(BF16))
                k_lv.append((k * w).astype(BF16))
        lanes = [slice(hd * C_DK, (hd + 1) * C_DK) for hd in range(C_HEADS)]
        atts = []
        for ln in lanes:
            att = None
            for ql, kl, mk in zip(q_lv, k_lv, level_masks):
                part = jnp.where(mk, _dot_nt(ql[:, ln], kl[:, ln]), 0.0)
                att = part if att is None else att + part
            atts.append(att.astype(BF16))
        states = [st[hd] for hd in range(C_HEADS)]
        inter = [_dot_nt(q_in[:, ln], s_t.astype(BF16)) for ln, s_t in zip(lanes, states)]
        if emit is not None:
            emit()
        for hd, ln in enumerate(lanes):
            ob[pl.ds(r0, C_CHUNK), ln] = inter[hd] + _dot(atts[hd], v[:, ln])
        for hd, ln in enumerate(lanes):
            st[hd] = states[hd] * decay[:, ln] + _dot_tn(v[:, ln], k_out[:, ln])
        return carry

    @pl.when(mild)
    def _():
        arm = refill.fork()
        for lo in range(0, C_K, C_PIECE):
            cs = slice(lo, lo + C_PIECE)
            a = ab[0:rows, cs]
            k = kb[0:rows, cs]
            a_last = jnp.concatenate(
                [jnp.broadcast_to(ab[(c + 1) * C_CHUNK - 1:(c + 1) * C_CHUNK, cs], (C_CHUNK, C_PIECE))
                 for c in range(n_chunks)], axis=0)
            qin[0:rows, cs] = (qb[0:rows, cs] * jnp.exp(a)).astype(BF16)
            kneg[0:rows, cs] = (k * jnp.exp(-a)).astype(BF16)
            kout[0:rows, cs] = (k * jnp.exp(a_last - a)).astype(BF16)
        left = arm.n_pieces - len(arm.emitted)
        for c in range(n_chunks):
            share = -(-left * (c + 1) // n_chunks) - (-(-left * c // n_chunks))
            chunk(c, 0, True, functools.partial(arm.emit, share))
        arm.flush()

    @pl.when(jnp.logical_not(mild))
    def _():
        refill.fork().flush()
        lax.fori_loop(0, n_chunks, functools.partial(chunk, factored=False), 0)

    y = None
    for lo in range(0, C_W, MXU_TILE):
        for ln in (slice(lo, lo + C_DV), slice(lo + C_DV, lo + 2 * C_DV)):
            on = _rmsnorm(ob[0:rows, ln], og_ref[:, ln])
            gated[0:rows, ln] = (on * zb[0:rows, ln]).astype(BF16)
        part = _dot(gated[0:rows, lo:lo + MXU_TILE], wout_ref[lo:lo + MXU_TILE, :])
        y = part if y is None else y + part
    if meta:
        y = jnp.where(_valid_rows(rows, D_MODEL), y, 0.0)
    return h + y


def _hgrn_kernel(x_ref, xn_ref, hm_ref, gain_ref, win_ref, lb_ref, og_ref, wout_ref,
                 y_ref, hm_out_ref, ub, qb, kb, ab, vb, zb, gated, st, qin, kneg, kout, ob, *pieces, tb):
    proj = _PieceBuffers(pieces)
    n_pieces = len(pieces)
    params = (lb_ref, og_ref, wout_ref, proj, qb, kb, ab, vb, zb, gated, st, qin, kneg, kout, ob)

    @pl.when(pl.program_id(1) == 0)
    def _():
        st[...] = jnp.zeros(st.shape, F32)
        hm = hm_ref[...]
        um = _rmsnorm(hm, gain_ref[...]).astype(BF16)
        for k in range(n_pieces):
            proj.project(k, BLOCK, um, win_ref)
        hm_out_ref[...] = _hgrn_rows(hm, BLOCK, True, *params, None)
        u0 = _rmsnorm(x_ref[0], gain_ref[...]).astype(BF16)
        for k in range(n_pieces):
            proj.project(k, tb, u0, win_ref)

    ub[...] = _rmsnorm(xn_ref[0], gain_ref[...]).astype(BF16)

    def next_piece(k):
        proj.project(k, tb, ub[...], win_ref)

    y_ref[0] = _hgrn_rows(x_ref[0], tb, False, *params, next_piece)


def _hgrn_layer(x, hm, gain, w_in, lb, o_gain, w_out, tb):
    bsz, seq, _ = x.shape
    n_tiles = seq // tb
    row_spec = pl.BlockSpec((1, tb, D_MODEL), lambda b, i: (b, i, 0))
    next_spec = pl.BlockSpec((1, tb, D_MODEL), lambda b, i: (b, jnp.minimum(i + 1, n_tiles - 1), 0))
    return pl.pallas_call(
        functools.partial(_hgrn_kernel, tb=tb),
        grid=(bsz, n_tiles),
        in_specs=[row_spec, next_spec, _const_spec((BLOCK, D_MODEL)), _const_spec((1, D_MODEL)),
                  _const_spec((D_MODEL, C_IN)), _const_spec((SUBLANES, C_K)), _const_spec((1, C_W)),
                  _const_spec((C_W, D_MODEL))],
        out_specs=[row_spec, _meta_out_spec()],
        out_shape=[jax.ShapeDtypeStruct(x.shape, F32), jax.ShapeDtypeStruct((BLOCK, D_MODEL), F32)],
        scratch_shapes=[pltpu.VMEM((tb, D_MODEL), BF16),
                        pltpu.VMEM((tb, C_K), F32), pltpu.VMEM((tb, C_K), F32),
                        pltpu.VMEM((tb, C_K), F32), pltpu.VMEM((tb, C_W), BF16),
                        pltpu.VMEM((tb, C_W), F32), pltpu.VMEM((tb, C_W), BF16),
                        pltpu.VMEM((C_HEADS, C_DV, C_DK), F32),
                        pltpu.VMEM((tb, C_K), BF16), pltpu.VMEM((tb, C_K), BF16),
                        pltpu.VMEM((tb, C_K), BF16), pltpu.VMEM((tb, C_W), F32)]
                       + [pltpu.VMEM((tb, C_PIECE), F32)] * (C_IN // C_PIECE),
        compiler_params=pltpu.CompilerParams(dimension_semantics=("arbitrary", "arbitrary"),
                                             vmem_limit_bytes=VMEM_LIMIT),
        name="hgrn2_layer",
    )(x, x, hm, gain, w_in, lb, o_gain, w_out)


def _rope_tables(first, n):
    half = A_HD // 2
    inv = ROPE_THETA ** (-jnp.arange(half, dtype=F32) / half)
    pos = (jnp.arange(first, first + n) - FRONT_PAD).astype(jnp.int32)
    ang = pos.astype(F32)[:, None] * inv[None, :]
    reps = LANES // half
    sign = jnp.tile(jnp.asarray([-1.0, 1.0], F32), reps // 2)
    cos = jnp.broadcast_to(jnp.cos(ang)[:, None, :], (n, reps, half))
    sin = jnp.sin(ang)[:, None, :] * sign[None, :, None]
    return jnp.concatenate([cos, sin], axis=1).reshape(n, 2 * LANES)


def _attn_gain_rows(q_gain, k_gain):
    half = A_HD // 2
    reps = LANES // A_HD
    rows = []
    for g, scale in ((q_gain.astype(F32), A_HD ** -0.5 * LOG2E), (k_gain.astype(F32), 1.0)):
        g_rot = jnp.concatenate([g[half:], g[:half]])
        rows += [jnp.tile(g * scale, reps), jnp.tile(g_rot * scale, reps)]
    return jnp.concatenate([jnp.stack(rows), jnp.zeros((SUBLANES - len(rows), LANES), F32)], axis=0)


def _row_tile(seq):
    tb = ROW_TILE
    while seq % tb:
        tb //= 2
    assert tb >= BLOCK, "sequence length must be a multiple of the 128-token block"
    return tb


def kernel(x, meta, norm_gain, a_w_in, a_q_gain, a_k_gain, a_sinks, a_w_out, b_w_in, b_conv_w,
           b_conv_b, b_gate_bias, b_h_gain, b_w_out, c_w_in, c_gamma, c_o_gain, c_w_out):
    bsz, seq, _ = x.shape
    depth = norm_gain.shape[0]
    tab, tabm = _rope_tables(BLOCK, seq), _rope_tables(0, BLOCK)
    hm = jnp.concatenate([jnp.zeros((FRONT_PAD, D_MODEL), x.dtype), meta.astype(x.dtype)], axis=0)
    p = jax.nn.softmax(c_gamma.astype(F32), axis=0)
    lower_bounds = jnp.cumsum(p, axis=0) - p

    def pair_heads(w, axis):
        shape = w.shape[:axis] + (A_KV // 2, 2, A_GROUP, A_HD) + w.shape[axis + 1:]
        perm = list(range(len(shape)))
        perm[axis + 1], perm[axis + 2] = axis + 2, axis + 1
        return w.reshape(shape).transpose(perm).reshape(w.shape)

    h = x
    for i in range(depth):
        kind, j = i % 3, i // 3
        gain = norm_gain[i].reshape(1, D_MODEL).astype(F32)
        if kind == 0:
            w = a_w_in[j].astype(BF16)
            w_in = jnp.concatenate([pair_heads(w[:, :A_W], 1), w[:, A_W:A_W + 2 * A_KVW],
                                    pair_heads(w[:, A_W + 2 * A_KVW:], 1)], axis=1)
            h, hm = _attn_layer(h, hm, tab, tabm, _attn_gain_rows(a_q_gain[j], a_k_gain[j]), gain,
                                w_in, a_sinks[j].astype(F32) * LOG2E,
                                pair_heads(a_w_out[j].astype(BF16), 0), _row_tile(seq))
        elif kind == 1:
            w = b_w_in[j].astype(BF16)
            o_v, o_i, o_o, o_z = 2 * B_QK, 2 * B_QK + B_W, 2 * B_QK + B_W + 2 * B_HEADS, 2 * B_QK + 2 * B_W + 2 * B_HEADS
            wg = jnp.pad(w[:, o_i:o_o], ((0, 0), (0, LANES - 2 * B_HEADS)))
            gb = b_gate_bias[j].astype(F32)
            h, hm = _mlstm_layer(
                h, hm, gain, w[:, :o_v], w[:, o_v:o_i], w[:, o_o:o_z], w[:, o_z:], wg,
                b_conv_w[j].astype(F32), b_conv_b[j].reshape(1, 2 * B_QK).astype(F32),
                jnp.broadcast_to(gb[:B_HEADS, None], (B_HEADS, BLOCK)),
                jnp.broadcast_to(gb[B_HEADS:, None], (B_HEADS, BLOCK)),
                b_h_gain[j].reshape(1, B_W).astype(F32), b_w_out[j].astype(BF16), _row_tile(seq))
        else:
            w = c_w_in[j].astype(BF16)
            lb = lower_bounds[i]
            lb_rows = jnp.zeros((SUBLANES, C_K), F32)
            lb_rows = lb_rows.at[0].set(jnp.log(lb)).at[1].set(jnp.log1p(-lb)).at[2].set(1.0 - lb)
            h, hm = _hgrn_layer(
                h, hm, gain, w, lb_rows, c_o_gain[j].reshape(1, C_W).astype(F32),
                c_w_out[j].astype(BF16), _row_tile(seq))
    return h
```
